```python
import math
import jax, jax.numpy as jnp
from jax import lax
import numpy as np

D_MODEL = 1024
BATCH = 8
SEQ = 8192
DEPTH = 2
DEC_BATCH = 16
DEC_SEQ = 32
PAST_LEN = 4096

F32 = jnp.float32
CHUNK = 64
Q_BLOCK = 128
D_MIX = D_MODEL
MLA_HEADS = 6
NOPE_DIM = 64
ROPE_DIM = 32
V_DIM = 64
Q_RANK = 256
KV_RANK = 128
ROPE_BASE = 10000.0
MLA_W = MLA_HEADS * V_DIM
MLA_SCALE = (NOPE_DIM + ROPE_DIM) ** -0.5
MLA_PROJ = Q_RANK + KV_RANK + ROPE_DIM
S5_GROUP_CH = 16
S5_W = 256
S5_GROUPS = S5_W // S5_GROUP_CH
S5_STATE = 64
RWKV_HEADS = 6
RWKV_HEAD = 64
RWKV_W = RWKV_HEADS * RWKV_HEAD
DECAY_LORA = 32
AAA_LORA = 32
GATE_LORA = 64
RWKV_PROJ = 3 * RWKV_W + DECAY_LORA + AAA_LORA + GATE_LORA
RWKV_SPLITS = (RWKV_W, 2 * RWKV_W, 3 * RWKV_W, 3 * RWKV_W + DECAY_LORA, 3 * RWKV_W + DECAY_LORA + AAA_LORA)
N_IN = MLA_PROJ + S5_W + RWKV_PROJ
IN_SPLITS = (Q_RANK, Q_RANK + KV_RANK, MLA_PROJ, MLA_PROJ + S5_W)
D_FF = 4 * D_MODEL
ALPHA = (2 * DEPTH) ** 0.25
BETA = (8 * DEPTH) ** -0.25
LN_EPS = 1e-5
RMS_EPS = 1e-6
GN_EPS = 64e-5
NEG_INF = -1e30

kernel_name = 'hybrid_mla_s5_rwkv7_streaming_step'


def layer_norm(x, g, b):
    xf = x.astype(F32)
    mu = jnp.mean(xf, -1, keepdims=True)
    var = jnp.mean(jnp.square(xf - mu), -1, keepdims=True)
    return ((xf - mu) * lax.rsqrt(var + LN_EPS) * g.astype(F32) + b.astype(F32)).astype(x.dtype)


def rms_norm(x, g):
    xf = x.astype(F32)
    inv = lax.rsqrt(jnp.mean(jnp.square(xf), -1, keepdims=True) + RMS_EPS)
    return (xf * inv * g.astype(F32)).astype(x.dtype)


def rope_tables(pos):
    inv_freq = ROPE_BASE ** (-jnp.arange(0, ROPE_DIM, 2, dtype=F32) / ROPE_DIM)
    ang = pos.astype(F32)[:, None] * inv_freq[None, :]
    ang = jnp.concatenate([ang, ang], -1)
    return jnp.cos(ang), jnp.sin(ang)


def apply_rope(x, cos, sin):
    shape = (1, cos.shape[0]) + (1,) * (x.ndim - 3) + (ROPE_DIM,)
    c, s = cos.reshape(shape), sin.reshape(shape)
    xf = x.astype(F32)
    x1, x2 = jnp.split(xf, 2, axis=-1)
    return (xf * c + jnp.concatenate([-x2, x1], -1) * s).astype(x.dtype)


def mla_scores(q_nope, q_rope, k_nope, k_rope):
    s = jnp.einsum('bqhd,bkhd->bhqk', q_nope, k_nope) + jnp.einsum('bqhr,bkr->bhqk', q_rope, k_rope)
    return s.astype(F32) * MLA_SCALE


def attend(s, v, mask=None):
    if mask is not None:
        s = jnp.where(mask, s, NEG_INF)
    p = jax.nn.softmax(s, axis=-1).astype(v.dtype)
    return jnp.einsum('bhqk,bkhd->bqhd', p, v)


def mla_prompt_attention(q_nope, q_rope, k_nope, k_rope, v):
    b, s = q_nope.shape[:2]
    key_chunk = jnp.arange(s) // CHUNK

    def query_block(i):
        start = i * Q_BLOCK
        qn = lax.dynamic_slice_in_dim(q_nope, start, Q_BLOCK, axis=1)
        qr = lax.dynamic_slice_in_dim(q_rope, start, Q_BLOCK, axis=1)
        q_chunk = (start + jnp.arange(Q_BLOCK)) // CHUNK
        mask = key_chunk[None, :] <= q_chunk[:, None]
        return attend(mla_scores(qn, qr, k_nope, k_rope), v, mask)

    out = lax.map(query_block, jnp.arange(s // Q_BLOCK))
    return jnp.swapaxes(out, 0, 1).reshape(b, s, MLA_W)


def mla_mixer(q_lat, kv_lat, k_rope_raw, pos, ckv_past, krope_past, prm):
    b, t = q_lat.shape[:2]
    cos, sin = rope_tables(pos)
    q = (rms_norm(q_lat, prm['q_norm_g']) @ prm['w_qb']).reshape(b, t, MLA_HEADS, NOPE_DIM + ROPE_DIM)
    q_nope = q[..., :NOPE_DIM]
    q_rope = apply_rope(q[..., NOPE_DIM:], cos, sin)
    ckv = rms_norm(kv_lat, prm['kv_norm_g'])
    krope = apply_rope(k_rope_raw, cos, sin)
    if ckv_past is None:
        ckv_all, krope_all = ckv, krope
    else:
        ckv_all = jnp.concatenate([ckv_past.astype(ckv.dtype), ckv], 1)
        krope_all = jnp.concatenate([krope_past.astype(krope.dtype), krope], 1)
    kv = (ckv_all @ prm['w_kvb']).reshape(b, ckv_all.shape[1], MLA_HEADS, NOPE_DIM + V_DIM)
    k_nope, v = kv[..., :NOPE_DIM], kv[..., NOPE_DIM:]
    if ckv_past is None:
        out = mla_prompt_attention(q_nope, q_rope, k_nope, krope_all, v)
    else:
        out = attend(mla_scores(q_nope, q_rope, k_nope, krope_all), v).reshape(b, t, MLA_W)
    return out, ckv, krope


def s5_discretize(prm):
    lam = lax.complex(prm['lam_re'].astype(F32), prm['lam_im'].astype(F32))
    dt = jnp.exp(prm['log_dt'].astype(F32))[:, None]
    lam_bar = jnp.exp(lam * dt)
    b = lax.complex(prm['b_re'].astype(F32), prm['b_im'].astype(F32))
    b_bar = ((lam_bar - 1.0) / lam)[..., None] * b
    c = lax.complex(prm['c_re'].astype(F32), prm['c_im'].astype(F32))
    return lam_bar, b_bar, c


def _linear_recurrence_combine(e1, e2):
    a1, b1 = e1
    a2, b2 = e2
    return a1 * a2, a2 * b1 + b2


def s5_block(u_blk, x0, lam_bar, b_bar, c):
    bu = jnp.einsum('gpc,btgc->btgp', b_bar, u_blk.astype(jnp.complex64))
    a = jnp.broadcast_to(lam_bar, bu.shape)
    a_cum, xs = lax.associative_scan(_linear_recurrence_combine, (a, bu), axis=1)
    xs = xs + a_cum * x0[:, None]
    y = jnp.einsum('gcp,btgp->btgc', c, xs).real
    return y, xs[:, -1]


def s5_mixer(u, x0, block_len, prm):
    b, t = u.shape[:2]
    lam_bar, b_bar, c = s5_discretize(prm)
    uf = u.astype(F32)
    ub = jnp.swapaxes(uf.reshape(b, t // block_len, block_len, S5_GROUPS, S5_GROUP_CH), 0, 1)

    def step(state, u_blk):
        y, state = s5_block(u_blk, state, lam_bar, b_bar, c)
        return state, y

    x_last, ys = lax.scan(step, x0, ub)
    y = jnp.swapaxes(ys, 0, 1).reshape(b, t, S5_W) + prm['s5_d'].astype(F32) * uf
    z = jax.nn.gelu(y)
    out = z * jax.nn.sigmoid(z @ prm['w_glu'].astype(F32) + prm['b_glu'].astype(F32))
    return out.astype(u.dtype), jnp.stack([x_last.real, x_last.imag], -1)


def rwkv_scan(r, w, k, v, kk, a, s0):
    xs = tuple(jnp.moveaxis(z.astype(F32), 1, 0) for z in (r, w, k, v, kk, a))

    def step(S, inp):
        r_t, w_t, k_t, v_t, kk_t, a_t = inp
        sa = jnp.einsum('bhij,bhj->bhi', S, -kk_t)
        S = (S * w_t[:, :, None, :] + sa[..., None] * (kk_t * a_t)[:, :, None, :]
             + v_t[..., None] * k_t[:, :, None, :])
        return S, jnp.einsum('bhij,bhj->bhi', S, r_t)

    s_last, ys = lax.scan(step, s0.astype(F32), xs)
    return jnp.moveaxis(ys, 0, 1), s_last


def rwkv_mixer(p, shift0, s0, prm):
    b, t = p.shape[:2]
    heads = (RWKV_HEADS, RWKV_HEAD)
    prev = jnp.concatenate([shift0.astype(p.dtype), p[:, :-1]], 1)
    ps = (p + (prev - p) * prm['mu_shift']).astype(F32)
    r, k, v, wd, ad, gd = jnp.split(ps, RWKV_SPLITS, axis=-1)
    w_log = -jax.nn.softplus(-(prm['w0'] + jnp.tanh(wd) @ prm['w_w2'])) - 0.5
    decay = jnp.exp(-jnp.exp(w_log))
    a = jax.nn.sigmoid(prm['a0'] + ad @ prm['w_a2'])
    g = jax.nn.sigmoid(gd) @ prm['w_g2']
    hs = lambda z: z.reshape(b, t, RWKV_HEADS, RWKV_HEAD)
    r, k, v, decay, a = hs(r), hs(k), hs(v), hs(decay), hs(a)
    kk = k * prm['k_k'].reshape(heads)
    kk = kk / jnp.maximum(jnp.linalg.norm(kk, axis=-1, keepdims=True), 1e-12)
    k = k * (1.0 + (a - 1.0) * prm['k_a'].reshape(heads))
    y, s_last = rwkv_scan(r, decay, k, v, kk, a, s0)
    mu = jnp.mean(y, -1, keepdims=True)
    var = jnp.mean(jnp.square(y - mu), -1, keepdims=True)
    y = ((y - mu) * lax.rsqrt(var + GN_EPS)).reshape(b, t, RWKV_W) * prm['gn_g'] + prm['gn_b']
    bonus = jnp.sum(r * k * prm['r_k'], -1, keepdims=True) * v
    y = (y + bonus.reshape(b, t, RWKV_W)) * g
    return y.astype(p.dtype), s_last, p[:, -1:]


def trunk_layer(x, pos, ckv_past, krope_past, s5_x0, rwkv_s0, shift0, s5_block_len, prm):
    proj = x @ prm['w_in']
    q_lat, kv_lat, k_rope_raw, u, p_rwkv = jnp.split(proj, IN_SPLITS, axis=-1)
    mla_out, ckv, krope = mla_mixer(q_lat, kv_lat, k_rope_raw, pos, ckv_past, krope_past, prm)
    s5_out, s5_state = s5_mixer(u, s5_x0, s5_block_len, prm)
    rwkv_out, rwkv_state, shift = rwkv_mixer(p_rwkv, shift0, rwkv_s0, prm)
    merged = jnp.concatenate([mla_out.astype(x.dtype), s5_out.astype(x.dtype), rwkv_out.astype(x.dtype)], -1)
    x = layer_norm(ALPHA * x + merged @ prm['w_out'], prm['ln1_g'], prm['ln1_b'])
    hidden = jnp.square(jax.nn.relu(x @ prm['w_up']))
    x = layer_norm(ALPHA * x + hidden @ prm['w_down'], prm['ln2_g'], prm['ln2_b'])
    return x, ckv, krope, s5_state, rwkv_state, shift


def setup_inputs(seed: int = 0) -> dict:
    key = jax.random.key(seed)
    ks = iter(jax.random.split(key, 64))

    def nrm(shape, scale):
        return scale * jax.random.normal(next(ks), shape, F32)

    def unif(shape, lo, hi):
        return jax.random.uniform(next(ks), shape, F32, lo, hi)

    L = DEPTH
    G, P, GC = S5_GROUPS, S5_STATE, S5_GROUP_CH
    H, N = RWKV_HEADS, RWKV_HEAD
    return {
        'x_prompt': nrm((BATCH, SEQ, D_MODEL), 1.0),
        'x_sample': nrm((DEC_BATCH, DEC_SEQ, D_MODEL), 1.0),
        'cache_mla_ckv': nrm((L, DEC_BATCH, PAST_LEN, KV_RANK), 1.0),
        'cache_mla_krope': nrm((L, DEC_BATCH, PAST_LEN, ROPE_DIM), 1.0),
        'state_s5': nrm((L, DEC_BATCH, G, P, 2), 0.1),
        'state_rwkv': nrm((L, DEC_BATCH, H, N, N), 0.3),
        'state_rwkv_shift': nrm((L, DEC_BATCH, 1, RWKV_PROJ), 1.0),
        'w_in': nrm((L, D_MODEL, N_IN), D_MODEL ** -0.5),
        'q_norm_g': 1.0 + nrm((L, Q_RANK), 0.02),
        'w_qb': nrm((L, Q_RANK, MLA_HEADS * (NOPE_DIM + ROPE_DIM)), Q_RANK ** -0.5),
        'kv_norm_g': 1.0 + nrm((L, KV_RANK), 0.02),
        'w_kvb': nrm((L, KV_RANK, MLA_HEADS * (NOPE_DIM + V_DIM)), KV_RANK ** -0.5),
        'lam_re': -0.5 + nrm((L, G, P), 0.01),
        'lam_im': math.pi * jnp.arange(P, dtype=F32) + nrm((L, G, P), 0.01),
        'log_dt': unif((L, G), math.log(1e-3), math.log(1e-1)),
        'b_re': nrm((L, G, P, GC), (0.5 / GC) ** 0.5),
        'b_im': nrm((L, G, P, GC), (0.5 / GC) ** 0.5),
        'c_re': nrm((L, G, GC, P), P ** -0.5),
        'c_im': nrm((L, G, GC, P), P ** -0.5),
        's5_d': nrm((L, S5_W), 1.0),
        'w_glu': nrm((L, S5_W, S5_W), S5_W ** -0.5),
        'b_glu': nrm((L, S5_W), 0.02),
        'mu_shift': unif((L, RWKV_PROJ), 0.0, 1.0),
        'w0': unif((L, RWKV_W), -6.0, 1.0),
        'w_w2': nrm((L, DECAY_LORA, RWKV_W), 0.1 * DECAY_LORA ** -0.5),
        'a0': nrm((L, RWKV_W), 0.5),
        'w_a2': nrm((L, AAA_LORA, RWKV_W), 0.1 * AAA_LORA ** -0.5),
        'w_g2': nrm((L, GATE_LORA, RWKV_W), GATE_LORA ** -0.5),
        'k_k': 0.85 + nrm((L, RWKV_W), 0.05),
        'k_a': 1.0 + nrm((L, RWKV_W), 0.05),
        'r_k': nrm((L, H, N), 0.1),
        'gn_g': 1.0 + nrm((L, RWKV_W), 0.02),
        'gn_b': nrm((L, RWKV_W), 0.02),
        'w_out': nrm((L, D_MIX, D_MODEL), BETA * D_MIX ** -0.5),
        'ln1_g': 1.0 + nrm((L, D_MODEL), 0.02),
        'ln1_b': nrm((L, D_MODEL), 0.02),
        'w_up': nrm((L, D_MODEL, D_FF), D_MODEL ** -0.5),
        'w_down': nrm((L, D_FF, D_MODEL), BETA * D_FF ** -0.5),
        'ln2_g': 1.0 + nrm((L, D_MODEL), 0.02),
        'ln2_b': nrm((L, D_MODEL), 0.02),
    }


def reference(x_prompt, x_sample, cache_mla_ckv, cache_mla_krope, state_s5, state_rwkv, state_rwkv_shift,
              w_in, q_norm_g, w_qb, kv_norm_g, w_kvb, lam_re, lam_im, log_dt, b_re, b_im, c_re, c_im,
              s5_d, w_glu, b_glu, mu_shift, w0, w_w2, a0, w_a2, w_g2, k_k, k_a, r_k, gn_g, gn_b,
              w_out, ln1_g, ln1_b, w_up, w_down, ln2_g, ln2_b):
    bp, sp = x_prompt.shape[:2]
    ts = x_sample.shape[1]
    past = cache_mla_ckv.shape[2]
    pos_p = jnp.arange(sp)
    pos_s = past + jnp.arange(ts)
    s5_zero = jnp.zeros((bp, S5_GROUPS, S5_STATE), jnp.complex64)
    rwkv_zero = jnp.zeros((bp, RWKV_HEADS, RWKV_HEAD, RWKV_HEAD), F32)
    shift_zero = jnp.zeros((bp, 1, RWKV_PROJ), x_prompt.dtype)

    xp, xs = x_prompt, x_sample
    ckv_p, krope_p, s5_p, rwkv_p, shift_p = [], [], [], [], []
    ckv_s, krope_s, s5_s, rwkv_s, shift_s = [], [], [], [], []
    for l in range(DEPTH):
        prm = dict(w_in=w_in[l], q_norm_g=q_norm_g[l], w_qb=w_qb[l], kv_norm_g=kv_norm_g[l], w_kvb=w_kvb[l],
                   lam_re=lam_re[l], lam_im=lam_im[l], log_dt=log_dt[l], b_re=b_re[l], b_im=b_im[l],
                   c_re=c_re[l], c_im=c_im[l], s5_d=s5_d[l], w_glu=w_glu[l], b_glu=b_glu[l],
                   mu_shift=mu_shift[l], w0=w0[l], w_w2=w_w2[l], a0=a0[l], w_a2=w_a2[l], w_g2=w_g2[l],
                   k_k=k_k[l], k_a=k_a[l], r_k=r_k[l], gn_g=gn_g[l], gn_b=gn_b[l], w_out=w_out[l],
                   ln1_g=ln1_g[l], ln1_b=ln1_b[l], w_up=w_up[l], w_down=w_down[l], ln2_g=ln2_g[l], ln2_b=ln2_b[l])
        xp, c1, k1, s1, r1, h1 = trunk_layer(xp, pos_p, None, None, s5_zero, rwkv_zero, shift_zero, CHUNK, prm)
        ckv_p.append(c1); krope_p.append(k1); s5_p.append(s1); rwkv_p.append(r1); shift_p.append(h1)
        s5_x0 = lax.complex(state_s5[l, ..., 0].astype(F32), state_s5[l, ..., 1].astype(F32))
        xs, c2, k2, s2, r2, h2 = trunk_layer(xs, pos_s, cache_mla_ckv[l], cache_mla_krope[l], s5_x0,
                                             state_rwkv[l], state_rwkv_shift[l], ts, prm)
        ckv_s.append(c2); krope_s.append(k2); s5_s.append(s2); rwkv_s.append(r2); shift_s.append(h2)

    return (xp, xs,
            jnp.stack(ckv_p), jnp.stack(krope_p), jnp.stack(s5_p), jnp.stack(rwkv_p), jnp.stack(shift_p),
            jnp.stack(ckv_s), jnp.stack(krope_s), jnp.stack(s5_s), jnp.stack(rwkv_s), jnp.stack(shift_s))
```

```python
import functools
import math

import jax
import jax.numpy as jnp
from jax import lax
from jax.experimental import pallas as pl
from jax.experimental.pallas import tpu as pltpu

F32 = jnp.float32
BF16 = jnp.bfloat16

CHUNK = 64
MLA_HEADS = 6
NOPE_DIM = 64
ROPE_DIM = 32
V_DIM = 64
Q_RANK = 256
KV_RANK = 128
ROPE_BASE = 10000.0
MLA_SCALE = (NOPE_DIM + ROPE_DIM) ** -0.5
S5_GROUP_CH = 16
S5_W = 256
S5_GROUPS = S5_W // S5_GROUP_CH
S5_STATE = 64
S5_N = S5_GROUPS * S5_STATE
RWKV_HEADS = 6
RWKV_HEAD = 64
RWKV_W = RWKV_HEADS * RWKV_HEAD
DECAY_LORA = 32
AAA_LORA = 32
GATE_LORA = 64
RWKV_PROJ = 3 * RWKV_W + DECAY_LORA + AAA_LORA + GATE_LORA
LN_EPS = 1e-5
RMS_EPS = 1e-6
GN_EPS = 64e-5
NEG_INF = -1e30

V7X_LANES = 128
V7X_VMEM_BYTES = 64 * 1024 * 1024
VMEM_LIMIT_BYTES = V7X_VMEM_BYTES - 8 * 1024 * 1024
HEAD_PAD = V7X_LANES


def _cparams(*sem):
    return pltpu.CompilerParams(dimension_semantics=sem, vmem_limit_bytes=VMEM_LIMIT_BYTES)


def _const_spec(shape):
    zeros = (0,) * len(shape)
    return pl.BlockSpec(shape, lambda *_: zeros)


def _tile(n, target):
    if n <= target:
        return n
    t = target
    while n % t:
        t //= 2
    return t


def _layer_norm(y, g, b):
    mu = jnp.mean(y, -1, keepdims=True)
    yc = y - mu
    var = jnp.mean(yc * yc, -1, keepdims=True)
    return yc * lax.rsqrt(var + LN_EPS) * g + b


def _inproj_kernel(x_ref, wm_ref, wu_ref, wp_ref, wq1_ref, wq2_ref, qg_ref, kg_ref, csk_ref, cq_ref, sq_ref,
                   qcat_ref, ckv_ref, krope_ref, u_ref, p_ref):
    x = x_ref[0].astype(BF16)
    m = jnp.dot(x, wm_ref[...], preferred_element_type=F32)
    q_lat = m[:, :Q_RANK]
    qn = q_lat * lax.rsqrt(jnp.mean(q_lat * q_lat, -1, keepdims=True) + RMS_EPS) * qg_ref[...]
    kv_lat = m[:, Q_RANK:Q_RANK + KV_RANK]
    ckv_ref[0] = kv_lat * lax.rsqrt(jnp.mean(kv_lat * kv_lat, -1, keepdims=True) + RMS_EPS) * kg_ref[...]
    kr = m[:, Q_RANK + KV_RANK:] * csk_ref[...]
    kr = kr + pltpu.roll(kr, V7X_LANES - ROPE_DIM, 1)
    krope_ref[0] = kr[:, :ROPE_DIM]
    qb = qn.astype(BF16)
    q1 = jnp.dot(qb, wq1_ref[...], preferred_element_type=F32)
    q2 = jnp.dot(qb, wq2_ref[...], preferred_element_type=F32)
    cq = cq_ref[...]
    sq = sq_ref[...]
    for h in range(MLA_HEADS):
        sl = slice(h * HEAD_PAD, (h + 1) * HEAD_PAD)
        qcat_ref[0, :, sl] = (q1[:, sl] * cq + q2[:, sl] * sq).astype(BF16)
    u_ref[...] = jnp.dot(x, wu_ref[...], preferred_element_type=F32)
    p_ref[...] = jnp.dot(x, wp_ref[...], preferred_element_type=F32)


def _inproj(x, w, tabs):
    B, T, D = x.shape
    tt = _tile(T, 512)
    nq = MLA_HEADS * HEAD_PAD
    grid = (B, T // tt)
    row = lambda b, i: (b, i, 0)
    tab = lambda b, i: (i, 0)
    tm = lambda b, i: (i, b)
    return pl.pallas_call(
        _inproj_kernel,
        grid=grid,
        in_specs=[
            pl.BlockSpec((1, tt, D), row),
            _const_spec(w['wm'].shape), _const_spec(w['wu'].shape), _const_spec(w['wp'].shape),
            _const_spec(w['wq1'].shape), _const_spec(w['wq2'].shape),
            _const_spec((1, Q_RANK)), _const_spec((1, KV_RANK)),
            pl.BlockSpec((tt, V7X_LANES), tab), pl.BlockSpec((tt, HEAD_PAD), tab), pl.BlockSpec((tt, HEAD_PAD), tab),
        ],
        out_specs=[
            pl.BlockSpec((1, tt, nq), row),
            pl.BlockSpec((1, tt, KV_RANK), row),
            pl.BlockSpec((1, tt, ROPE_DIM), row),
            pl.BlockSpec((tt, S5_W), tm),
            pl.BlockSpec((tt, RWKV_PROJ), tm),
        ],
        out_shape=[
            jax.ShapeDtypeStruct((B, T, nq), BF16),
            jax.ShapeDtypeStruct((B, T, KV_RANK), F32),
            jax.ShapeDtypeStruct((B, T, ROPE_DIM), F32),
            jax.ShapeDtypeStruct((T, B * S5_W), F32),
            jax.ShapeDtypeStruct((T, B * RWKV_PROJ), F32),
        ],
        compiler_params=_cparams("parallel", "parallel"),
        name="inproj",
    )(x, w['wm'], w['wu'], w['wp'], w['wq1'], w['wq2'], w['qg'], w['kg'], tabs['csk'], tabs['cq'], tabs['sq'])


def _kvup_kernel(ckv_ref, kr_ref, wk_ref, pk_ref, wv_ref, kcat_ref, v_ref):
    c = ckv_ref[0].astype(BF16)
    kr = kr_ref[0].astype(BF16)
    kcat = jnp.dot(c, wk_ref[...], preferred_element_type=F32) + jnp.dot(kr, pk_ref[...], preferred_element_type=F32)
    kcat_ref[0] = kcat.astype(BF16)
    v_ref[0] = jnp.dot(c, wv_ref[...], preferred_element_type=F32).astype(BF16)


def _kvup(ckv, krope, w):
    B, T, _ = ckv.shape
    tt = _tile(T, 512)
    nk = MLA_HEADS * HEAD_PAD
    nv = MLA_HEADS * V_DIM
    row = lambda b, i: (b, i, 0)
    return pl.pallas_call(
        _kvup_kernel,
        grid=(B, T // tt),
        in_specs=[pl.BlockSpec((1, tt, KV_RANK), row), pl.BlockSpec((1, tt, ROPE_DIM), row),
                  _const_spec(w['wk'].shape), _const_spec(w['pk'].shape), _const_spec(w['wv'].shape)],
        out_specs=[pl.BlockSpec((1, tt, nk), row), pl.BlockSpec((1, tt, nv), row)],
        out_shape=[jax.ShapeDtypeStruct((B, T, nk), BF16), jax.ShapeDtypeStruct((B, T, nv), BF16)],
        compiler_params=_cparams("parallel", "parallel"),
        name="kvup",
    )(ckv, krope, w['wk'], w['pk'], w['wv'])


def _attn_kernel(q_ref, k_ref, v_ref, o_ref, *, bq, bk, kv_len, chunk_causal):
    qi = pl.program_id(2)
    q = q_ref[0]

    def step(j, carry, masked):
        m, l, acc = carry
        k = k_ref[0, pl.ds(pl.multiple_of(j * bk, bk), bk), :]
        s = lax.dot_general(k, q, (((1,), (1,)), ((), ())), preferred_element_type=F32)
        if masked:
            kidx = j * bk + lax.broadcasted_iota(jnp.int32, (bk, bq), 0)
            ok = kidx < kv_len
            if chunk_causal:
                qidx = qi * bq + lax.broadcasted_iota(jnp.int32, (bk, bq), 1)
                ok = jnp.logical_and(ok, (kidx // CHUNK) <= (qidx // CHUNK))
            s = jnp.where(ok, s, NEG_INF)
        m_new = jnp.maximum(m, jnp.max(s, axis=0, keepdims=True))
        p = jnp.exp(s - m_new)
        alpha = jnp.exp(m - m_new)
        l = alpha * l + jnp.sum(p, axis=0, keepdims=True)
        acc = alpha * acc + jnp.dot(v_ref[0, 0, j], p.astype(BF16), preferred_element_type=F32)
        return m_new, l, acc

    init = (jnp.full((1, bq), NEG_INF, F32), jnp.zeros((1, bq), F32), jnp.zeros((V_DIM, bq), F32))
    nblk = -(-kv_len // bk)
    if chunk_causal:
        carry = lax.fori_loop(0, qi, lambda j, c: step(j, c, False), init)
        m, l, acc = step(qi, carry, True)
    else:
        n_full = kv_len // bk
        carry = lax.fori_loop(0, n_full, lambda j, c: step(j, c, False), init)
        for j in range(n_full, nblk):
            carry = step(j, carry, True)
        m, l, acc = carry
    o_ref[0, 0] = (acc / l).astype(BF16)


def _attention(qcat, kcat, v, *, kv_len, chunk_causal):
    B, Tq, _ = qcat.shape
    Tk = kcat.shape[1]
    bk = _tile(Tk, 512) if chunk_causal else 512
    bq = bk if chunk_causal else Tq
    if not chunk_causal:
        pad = -Tk % bk
        if pad:
            kcat = jnp.pad(kcat, ((0, 0), (0, pad), (0, 0)))
            v = jnp.pad(v, ((0, 0), (0, pad), (0, 0)))
            Tk += pad
    nblk = Tk // bk
    vt = v.reshape(B, nblk, bk, MLA_HEADS, V_DIM).transpose(0, 3, 1, 4, 2)
    out_t = pl.pallas_call(
        functools.partial(_attn_kernel, bq=bq, bk=bk, kv_len=kv_len, chunk_causal=chunk_causal),
        grid=(B, MLA_HEADS, Tq // bq),
        in_specs=[
            pl.BlockSpec((1, bq, HEAD_PAD), lambda b, h, i: (b, i, h)),
            pl.BlockSpec((1, Tk, HEAD_PAD), lambda b, h, i: (b, 0, h)),
            pl.BlockSpec((1, 1, nblk, V_DIM, bk), lambda b, h, i: (b, h, 0, 0, 0)),
        ],
        out_specs=pl.BlockSpec((1, 1, V_DIM, bq), lambda b, h, i: (b, h, 0, i)),
        out_shape=jax.ShapeDtypeStruct((B, MLA_HEADS, V_DIM, Tq), BF16),
        compiler_params=_cparams("parallel", "parallel", "arbitrary"),
        name="mla_attention",
    )(qcat, kcat, vt)
    return out_t.transpose(0, 3, 1, 2).reshape(B, Tq, MLA_HEADS * V_DIM)


def _gelu_tanh(x):
    return 0.5 * x * (1.0 + jnp.tanh(math.sqrt(2.0 / math.pi) * (x + 0.044715 * (x * x * x))))


def _s5_kernel(u_ref, x0_ref, a_ref, wb_ref, wc_ref, d_ref, wg_ref, bg_ref, o_ref, xl_ref, xs_ref, st_ref, *, L, B):
    @pl.when(pl.program_id(0) == 0)
    def _():
        st_ref[...] = x0_ref[...]

    u = u_ref[...].reshape(L * B, S5_W)
    xs_ref[...] = jnp.dot(u.astype(BF16), wb_ref[...], preferred_element_type=F32)
    a_re = a_ref[0:1, :]
    a_im = a_ref[1:2, :]

    def body(t, carry):
        x_re, x_im = carry
        r0 = pl.multiple_of(t * B, B)
        n_re = a_re * x_re - a_im * x_im + xs_ref[pl.ds(r0, B), :S5_N]
        n_im = a_re * x_im + a_im * x_re + xs_ref[pl.ds(r0, B), S5_N:]
        xs_ref[pl.ds(r0, B), :S5_N] = n_re
        xs_ref[pl.ds(r0, B), S5_N:] = n_im
        return n_re, n_im

    x_re, x_im = lax.fori_loop(0, L, body, (st_ref[:, :S5_N], st_ref[:, S5_N:]))
    st_ref[:, :S5_N] = x_re
    st_ref[:, S5_N:] = x_im
    xl_ref[...] = st_ref[...]

    y = jnp.dot(xs_ref[...].astype(BF16), wc_ref[...], preferred_element_type=F32) + d_ref[...] * u
    z = _gelu_tanh(y)
    gate = jax.nn.sigmoid(jnp.dot(z.astype(BF16), wg_ref[...], preferred_element_type=F32) + bg_ref[...])
    o_ref[...] = (z * gate).reshape(L, B, S5_W)


def _s5(u_tm, x0, w):
    T, B, _ = u_tm.shape
    L = _tile(T, 1024 // B)
    return pl.pallas_call(
        functools.partial(_s5_kernel, L=L, B=B),
        grid=(T // L,),
        in_specs=[
            pl.BlockSpec((L, B, S5_W), lambda i: (i, 0, 0)),
            _const_spec((B, 2 * S5_N)), _const_spec((2, S5_N)),
            _const_spec((S5_W, 2 * S5_N)), _const_spec((2 * S5_N, S5_W)),
            _const_spec((1, S5_W)), _const_spec((S5_W, S5_W)), _const_spec((1, S5_W)),
        ],
        out_specs=[pl.BlockSpec((L, B, S5_W), lambda i: (i, 0, 0)), _const_spec((B, 2 * S5_N))],
        out_shape=[jax.ShapeDtypeStruct((T, B, S5_W), F32), jax.ShapeDtypeStruct((B, 2 * S5_N), F32)],
        scratch_shapes=[pltpu.VMEM((L * B, 2 * S5_N), F32), pltpu.VMEM((B, 2 * S5_N), F32)],
        compiler_params=_cparams("arbitrary"),
        name="s5_scan",
    )(u_tm, x0, w['a'], w['wb'], w['wc'], w['d'], w['wglu'], w['bglu'])


def _softplus(x):
    return jnp.maximum(x, 0.0) + jnp.log(1.0 + jnp.exp(-jnp.abs(x)))


def _head_sum(x, ones_ref):
    return jnp.dot(x, ones_ref[...], preferred_element_type=F32, precision=lax.Precision.HIGHEST)


def _rwkv_prep_kernel(p_ref, sh0_ref, mu_ref, ww_ref, wa_ref, wg_ref, w0_ref, a0_ref, kk_ref, ka_ref, rk_ref,
                      ones_ref, r_o, w_o, k_o, v_o, n_o, b_o, g_o, bon_o, sh_o, last_ref, *, L, B):
    @pl.when(pl.program_id(0) == 0)
    def _():
        last_ref[...] = sh0_ref[...]

    p = p_ref[...].reshape(L * B, RWKV_PROJ)
    if L > 1:
        prev = jnp.concatenate([last_ref[...], p[:(L - 1) * B]], axis=0)
    else:
        prev = last_ref[...]
    last_ref[...] = p[(L - 1) * B:]
    sh_o[...] = p[(L - 1) * B:]
    ps = p + (prev - p) * mu_ref[...]
    r = ps[:, :RWKV_W]
    k = ps[:, RWKV_W:2 * RWKV_W]
    v = ps[:, 2 * RWKV_W:3 * RWKV_W]
    tail = ps[:, 3 * RWKV_W:]
    lw = jnp.dot(jnp.tanh(tail).astype(BF16), ww_ref[...], preferred_element_type=F32)
    la = jnp.dot(tail.astype(BF16), wa_ref[...], preferred_element_type=F32)
    g = jnp.dot(jax.nn.sigmoid(tail).astype(BF16), wg_ref[...], preferred_element_type=F32)
    w_log = -_softplus(-(w0_ref[...] + lw)) - 0.5
    decay = jnp.exp(-jnp.exp(w_log))
    a = jax.nn.sigmoid(a0_ref[...] + la)
    kk = k * kk_ref[...]
    nrm = jnp.maximum(jnp.sqrt(_head_sum(kk * kk, ones_ref)), 1e-12)
    kk = kk / nrm
    k2 = k * (1.0 + (a - 1.0) * ka_ref[...])
    bonus = _head_sum(r * k2 * rk_ref[...], ones_ref) * v
    shp = (L, B, RWKV_W)
    r_o[...] = r.reshape(shp)
    w_o[...] = decay.reshape(shp)
    k_o[...] = k2.reshape(shp)
    v_o[...] = v.reshape(shp)
    n_o[...] = (-kk).reshape(shp)
    b_o[...] = (kk * a).reshape(shp)
    g_o[...] = g.reshape(shp)
    bon_o[...] = bonus.reshape(shp)


def _rwkv_prep(p_tm, shift0, w):
    T, B, _ = p_tm.shape
    L = _tile(T, 512 // B)
    blk = pl.BlockSpec((L, B, RWKV_W), lambda i: (i, 0, 0))
    vec = _const_spec((1, RWKV_W))
    outs = pl.pallas_call(
        functools.partial(_rwkv_prep_kernel, L=L, B=B),
        grid=(T // L,),
        in_specs=[pl.BlockSpec((L, B, RWKV_PROJ), lambda i: (i, 0, 0)), _const_spec((B, RWKV_PROJ)),
                  _const_spec((1, RWKV_PROJ)),
                  _const_spec((V7X_LANES, RWKV_W)), _const_spec((V7X_LANES, RWKV_W)), _const_spec((V7X_LANES, RWKV_W)),
                  vec, vec, vec, vec, vec, _const_spec((RWKV_W, RWKV_W))],
        out_specs=[blk] * 8 + [_const_spec((B, RWKV_PROJ))],
        out_shape=[jax.ShapeDtypeStruct((T, B, RWKV_W), F32)] * 8 + [jax.ShapeDtypeStruct((B, RWKV_PROJ), F32)],
        scratch_shapes=[pltpu.VMEM((B, RWKV_PROJ), F32)],
        compiler_params=_cparams("arbitrary"),
        name="rwkv_prep",
    )(p_tm, shift0, w['mu'], w['ww'], w['wa'], w['wg'], w['w0'], w['a0'], w['kk'], w['ka'], w['rk'], w['ones'])
    return outs


def _rwkv_scan_kernel(w_ref, b_ref, k_ref, r_ref, nn_ref, v_ref, n0_ref, s0_ref, y_ref, so_ref, s_ref, sa_ref, *, Tc, IL):
    @pl.when(pl.program_id(0) == 0)
    def _():
        s_ref[...] = s0_ref[...]
        acc = jnp.zeros((IL, V7X_LANES), F32)
        for j in range(RWKV_HEAD):
            acc = acc + s0_ref[j] * n0_ref[j:j + 1, :]
        sa_ref[...] = acc

    def body(t, _):
        sa = sa_ref[...]
        vt = v_ref[t]
        yacc = jnp.zeros((IL, V7X_LANES), F32)
        san = jnp.zeros((IL, V7X_LANES), F32)
        for j in range(RWKV_HEAD):
            sn = s_ref[j] * w_ref[t, j:j + 1, :] + sa * b_ref[t, j:j + 1, :] + vt * k_ref[t, j:j + 1, :]
            s_ref[j] = sn
            yacc = yacc + sn * r_ref[t, j:j + 1, :]
            san = san + sn * nn_ref[t, j:j + 1, :]
        y_ref[t] = yacc
        sa_ref[...] = san
        return 0

    lax.fori_loop(0, Tc, body, 0)
    so_ref[...] = s_ref[...]


def _rwkv_scan(ops, v_l, n0, s0_l):
    T = v_l.shape[0]
    IL = v_l.shape[1]
    Tc = _tile(T, 32)
    opblk = pl.BlockSpec((Tc, RWKV_HEAD, V7X_LANES), lambda i: (i, 0, 0))
    vblk = pl.BlockSpec((Tc, IL, V7X_LANES), lambda i: (i, 0, 0))
    sblk = _const_spec((RWKV_HEAD, IL, V7X_LANES))
    return pl.pallas_call(
        functools.partial(_rwkv_scan_kernel, Tc=Tc, IL=IL),
        grid=(T // Tc,),
        in_specs=[opblk] * 5 + [vblk, _const_spec((RWKV_HEAD, V7X_LANES)), sblk],
        out_specs=[vblk, sblk],
        out_shape=[jax.ShapeDtypeStruct((T, IL, V7X_LANES), F32),
                   jax.ShapeDtypeStruct((RWKV_HEAD, IL, V7X_LANES), F32)],
        scratch_shapes=[pltpu.VMEM((RWKV_HEAD, IL, V7X_LANES), F32), pltpu.VMEM((IL, V7X_LANES), F32)],
        compiler_params=_cparams("arbitrary"),
        name="rwkv_scan",
    )(*ops, v_l, n0, s0_l)


def _rwkv_post_kernel(y_ref, bon_ref, g_ref, gg_ref, gb_ref, ones_ref, o_ref, *, L, B):
    y = y_ref[...].reshape(L * B, RWKV_W)
    inv_n = 1.0 / RWKV_HEAD
    mu = _head_sum(y, ones_ref) * inv_n
    yc = y - mu
    var = _head_sum(yc * yc, ones_ref) * inv_n
    yn = yc * lax.rsqrt(var + GN_EPS) * gg_ref[...] + gb_ref[...]
    out = (yn + bon_ref[...].reshape(L * B, RWKV_W)) * g_ref[...].reshape(L * B, RWKV_W)
    o_ref[...] = out.reshape(L, B, RWKV_W)


def _rwkv_post(y_tm, bonus, g, w):
    T, B, _ = y_tm.shape
    L = _tile(T, 1024 // B)
    blk = pl.BlockSpec((L, B, RWKV_W), lambda i: (i, 0, 0))
    vec = _const_spec((1, RWKV_W))
    return pl.pallas_call(
        functools.partial(_rwkv_post_kernel, L=L, B=B),
        grid=(T // L,),
        in_specs=[blk, blk, blk, vec, vec, _const_spec((RWKV_W, RWKV_W))],
        out_specs=blk,
        out_shape=jax.ShapeDtypeStruct((T, B, RWKV_W), F32),
        compiler_params=_cparams("parallel"),
        name="rwkv_post",
    )(y_tm, bonus, g, w['gng'], w['gnb'], w['ones'])


def _rwkv_layout(B):
    bh = B * RWKV_HEADS
    rep = 1
    while 2 * rep * bh <= V7X_LANES and RWKV_HEAD % (2 * rep) == 0:
        rep *= 2
    return bh, rep, RWKV_HEAD // rep


def _rwkv_mixer(p_tm, shift0, s0, w):
    T, B, _ = p_tm.shape
    bh, rep, IL = _rwkv_layout(B)
    lanes = rep * bh
    r, dec, k2, v, nkk, beta, g, bonus, shift = _rwkv_prep(p_tm, shift0, w)

    def to_ops(x):
        x = x.reshape(T, bh, RWKV_HEAD).transpose(0, 2, 1)
        x = jnp.tile(x, (1, 1, rep))
        return jnp.pad(x, ((0, 0), (0, 0), (0, V7X_LANES - lanes)))

    nkk_l = to_ops(nkk)
    nkk_next = jnp.concatenate([nkk_l[1:], jnp.zeros_like(nkk_l[:1])], axis=0)
    ops = [to_ops(dec), to_ops(beta), to_ops(k2), to_ops(r), nkk_next]
    v_l = v.reshape(T, bh, rep, IL).transpose(0, 3, 2, 1).reshape(T, IL, lanes)
    v_l = jnp.pad(v_l, ((0, 0), (0, 0), (0, V7X_LANES - lanes)))
    s0_l = s0.reshape(bh, rep, IL, RWKV_HEAD).transpose(3, 2, 1, 0).reshape(RWKV_HEAD, IL, lanes)
    s0_l = jnp.pad(s0_l, ((0, 0), (0, 0), (0, V7X_LANES - lanes)))
    y_l, s_l = _rwkv_scan(ops, v_l, nkk_l[0], s0_l)
    y_tm = y_l[:, :, :lanes].reshape(T, IL, rep, bh).transpose(0, 3, 2, 1).reshape(T, B, RWKV_W)
    s_last = s_l[:, :, :lanes].reshape(RWKV_HEAD, IL, rep, bh).transpose(3, 2, 1, 0)
    s_last = s_last.reshape(B, RWKV_HEADS, RWKV_HEAD, RWKV_HEAD)
    out = _rwkv_post(y_tm, bonus, g, w)
    return out, s_last, shift


def _outproj_kernel(m_ref, s_ref, r_ref, x_ref, wm_ref, ws_ref, wr_ref, g_ref, b_ref, o_ref, *, alpha):
    acc = jnp.dot(m_ref[0], wm_ref[...], preferred_element_type=F32)
    acc = acc + jnp.dot(s_ref[...].astype(BF16), ws_ref[...], preferred_element_type=F32)
    acc = acc + jnp.dot(r_ref[...].astype(BF16), wr_ref[...], preferred_element_type=F32)
    o_ref[0] = _layer_norm(alpha * x_ref[0] + acc, g_ref[...], b_ref[...])


def _outproj(mla, s5_tm, rw_tm, x, w, alpha):
    B, T, D = x.shape
    tt = _tile(T, 512)
    row = lambda b, i: (b, i, 0)
    tm = lambda b, i: (i, b)
    return pl.pallas_call(
        functools.partial(_outproj_kernel, alpha=alpha),
        grid=(B, T // tt),
        in_specs=[pl.BlockSpec((1, tt, MLA_HEADS * V_DIM), row), pl.BlockSpec((tt, S5_W), tm),
                  pl.BlockSpec((tt, RWKV_W), tm), pl.BlockSpec((1, tt, D), row),
                  _const_spec(w['wo_m'].shape), _const_spec(w['wo_s'].shape), _const_spec(w['wo_r'].shape),
                  _const_spec((1, D)), _const_spec((1, D))],
        out_specs=pl.BlockSpec((1, tt, D), row),
        out_shape=jax.ShapeDtypeStruct((B, T, D), F32),
        compiler_params=_cparams("parallel", "parallel"),
        name="outproj_ln",
    )(mla, s5_tm, rw_tm, x, w['wo_m'], w['wo_s'], w['wo_r'], w['ln1g'], w['ln1b'])


def _mlp_kernel(x_ref, wu_ref, wd_ref, g_ref, b_ref, o_ref, *, alpha, fc):
    x = x_ref[0]
    xb = x.astype(BF16)
    acc = alpha * x
    for c in range(wu_ref.shape[1] // fc):
        h = jnp.maximum(jnp.dot(xb, wu_ref[:, c * fc:(c + 1) * fc], preferred_element_type=F32), 0.0)
        acc = acc + jnp.dot((h * h).astype(BF16), wd_ref[c * fc:(c + 1) * fc, :], preferred_element_type=F32)
    o_ref[0] = _layer_norm(acc, g_ref[...], b_ref[...])


def _mlp(x, w, alpha):
    B, T, D = x.shape
    tt = _tile(T, 512)
    row = lambda b, i: (b, i, 0)
    return pl.pallas_call(
        functools.partial(_mlp_kernel, alpha=alpha, fc=1024),
        grid=(B, T // tt),
        in_specs=[pl.BlockSpec((1, tt, D), row), _const_spec(w['wup'].shape), _const_spec(w['wdown'].shape),
                  _const_spec((1, D)), _const_spec((1, D))],
        out_specs=pl.BlockSpec((1, tt, D), row),
        out_shape=jax.ShapeDtypeStruct((B, T, D), F32),
        compiler_params=_cparams("parallel", "parallel"),
        name="mlp_ln",
    )(x, w['wup'], w['wdown'], w['ln2g'], w['ln2b'])


def _rot_cols(wr):
    half = ROPE_DIM // 2
    return jnp.concatenate([-wr[..., half:], wr[..., :half]], axis=-1)


def _prep_layer(p):
    D = p['w_in'].shape[0]
    w_in = p['w_in']
    o_kv = Q_RANK
    o_kr = Q_RANK + KV_RANK
    o_u = o_kr + ROPE_DIM
    o_p = o_u + S5_W
    w_kr = w_in[:, o_kr:o_u]
    wm = jnp.concatenate([w_in[:, :o_kr], w_kr, _rot_cols(w_kr),
                          jnp.zeros((D, V7X_LANES - 2 * ROPE_DIM), F32)], axis=1)
    wqb = p['w_qb'].reshape(Q_RANK, MLA_HEADS, NOPE_DIM + ROPE_DIM) * MLA_SCALE
    zq = jnp.zeros((Q_RANK, MLA_HEADS, HEAD_PAD - NOPE_DIM - ROPE_DIM), F32)
    wq1 = jnp.concatenate([wqb, zq], axis=-1).reshape(Q_RANK, MLA_HEADS * HEAD_PAD)
    wq2 = jnp.concatenate([jnp.zeros((Q_RANK, MLA_HEADS, NOPE_DIM), F32), _rot_cols(wqb[..., NOPE_DIM:]), zq],
                          axis=-1).reshape(Q_RANK, MLA_HEADS * HEAD_PAD)
    wkvb = p['w_kvb'].reshape(KV_RANK, MLA_HEADS, NOPE_DIM + V_DIM)
    wk = jnp.concatenate([wkvb[..., :NOPE_DIM], jnp.zeros((KV_RANK, MLA_HEADS, HEAD_PAD - NOPE_DIM), F32)],
                         axis=-1).reshape(KV_RANK, MLA_HEADS * HEAD_PAD)
    pk = jnp.concatenate([jnp.zeros((ROPE_DIM, NOPE_DIM), F32), jnp.eye(ROPE_DIM, dtype=F32),
                          jnp.zeros((ROPE_DIM, HEAD_PAD - NOPE_DIM - ROPE_DIM), F32)], axis=1)
    pk = jnp.tile(pk, (1, MLA_HEADS))
    wv = wkvb[..., NOPE_DIM:].reshape(KV_RANK, MLA_HEADS * V_DIM)

    lam = lax.complex(p['lam_re'], p['lam_im'])
    dt = jnp.exp(p['log_dt'])[:, None]
    lam_bar = jnp.exp(lam * dt)
    b_bar = ((lam_bar - 1.0) / lam)[..., None] * lax.complex(p['b_re'], p['b_im'])
    eye_g = jnp.eye(S5_GROUPS, dtype=F32)
    bd_in = lambda m: jnp.einsum('gpc,gh->gchp', m, eye_g).reshape(S5_W, S5_N)
    bd_out = lambda m: jnp.einsum('gcp,gh->gphc', m, eye_g).reshape(S5_N, S5_W)
    wb = jnp.concatenate([bd_in(b_bar.real), bd_in(b_bar.imag)], axis=1)
    wc = jnp.concatenate([bd_out(p['c_re']), -bd_out(p['c_im'])], axis=0)
    a = jnp.stack([lam_bar.real.reshape(S5_N), lam_bar.imag.reshape(S5_N)])

    def lora_pad(w, off):
        z = jnp.zeros((V7X_LANES, RWKV_W), F32)
        return z.at[off:off + w.shape[0]].set(w)

    seg = jnp.arange(RWKV_W) // RWKV_HEAD
    ones = (seg[:, None] == seg[None, :]).astype(F32)
    row = lambda v: v.reshape(1, -1).astype(F32)
    wo = p['w_out']
    return dict(
        inproj=dict(wm=wm.astype(BF16), wu=w_in[:, o_u:o_p].astype(BF16), wp=w_in[:, o_p:].astype(BF16),
                    wq1=wq1.astype(BF16), wq2=wq2.astype(BF16), qg=row(p['q_norm_g']), kg=row(p['kv_norm_g'])),
        kvup=dict(wk=wk.astype(BF16), pk=pk.astype(BF16), wv=wv.astype(BF16)),
        s5=dict(a=a, wb=wb.astype(BF16), wc=wc.astype(BF16), d=row(p['s5_d']), wglu=p['w_glu'].astype(BF16),
                bglu=row(p['b_glu'])),
        rwkv=dict(mu=row(p['mu_shift']), ww=lora_pad(p['w_w2'], 0).astype(BF16),
                  wa=lora_pad(p['w_a2'], DECAY_LORA).astype(BF16),
                  wg=lora_pad(p['w_g2'], DECAY_LORA + AAA_LORA).astype(BF16),
                  w0=row(p['w0']), a0=row(p['a0']), kk=row(p['k_k']), ka=row(p['k_a']), rk=row(p['r_k']),
                  gng=row(p['gn_g']), gnb=row(p['gn_b']), ones=ones),
        out=dict(wo_m=wo[:MLA_HEADS * V_DIM].astype(BF16),
                 wo_s=wo[MLA_HEADS * V_DIM:MLA_HEADS * V_DIM + S5_W].astype(BF16),
                 wo_r=wo[MLA_HEADS * V_DIM + S5_W:].astype(BF16), ln1g=row(p['ln1_g']), ln1b=row(p['ln1_b'])),
        mlp=dict(wup=p['w_up'].astype(BF16), wdown=p['w_down'].astype(BF16), ln2g=row(p['ln2_g']),
                 ln2b=row(p['ln2_b'])),
    )


def _rope_tables(pos):
    inv_freq = ROPE_BASE ** (-jnp.arange(0, ROPE_DIM, 2, dtype=F32) / ROPE_DIM)
    ang = pos.astype(F32)[:, None] * inv_freq[None, :]
    ang = jnp.concatenate([ang, ang], -1)
    cos, sin = jnp.cos(ang), jnp.sin(ang)
    T = pos.shape[0]
    z = lambda n: jnp.zeros((T, n), F32)
    return dict(
        csk=jnp.concatenate([cos, sin, z(V7X_LANES - 2 * ROPE_DIM)], axis=1),
        cq=jnp.concatenate([jnp.ones((T, NOPE_DIM), F32), cos, z(HEAD_PAD - NOPE_DIM - ROPE_DIM)], axis=1),
        sq=jnp.concatenate([z(NOPE_DIM), sin, z(HEAD_PAD - NOPE_DIM - ROPE_DIM)], axis=1),
    )


def _trunk_layer(x, tabs, past, s5_state, rwkv_state, shift0, w, alpha):
    B, T, D = x.shape
    qcat, ckv, krope, u_tm, p_tm = _inproj(x, w['inproj'], tabs)
    kcat, v = _kvup(ckv, krope, w['kvup'])
    if past is None:
        mla = _attention(qcat, kcat, v, kv_len=T, chunk_causal=True)
    else:
        kcat = jnp.concatenate([past[0], kcat], axis=1)
        v = jnp.concatenate([past[1], v], axis=1)
        mla = _attention(qcat, kcat, v, kv_len=kcat.shape[1], chunk_causal=False)

    x0 = jnp.concatenate([s5_state[..., 0].reshape(B, S5_N), s5_state[..., 1].reshape(B, S5_N)], axis=1)
    s5_out, x_last = _s5(u_tm.reshape(T, B, S5_W), x0, w['s5'])
    new_s5 = jnp.stack([x_last[:, :S5_N].reshape(B, S5_GROUPS, S5_STATE),
                        x_last[:, S5_N:].reshape(B, S5_GROUPS, S5_STATE)], axis=-1)

    rw_out, new_rwkv, shift = _rwkv_mixer(p_tm.reshape(T, B, RWKV_PROJ), shift0.reshape(B, RWKV_PROJ), rwkv_state,
                                          w['rwkv'])
    x = _outproj(mla, s5_out.reshape(T, B * S5_W), rw_out.reshape(T, B * RWKV_W), x, w['out'], alpha)
    x = _mlp(x, w['mlp'], alpha)
    return x, ckv, krope, new_s5, new_rwkv, shift.reshape(B, 1, RWKV_PROJ)


def kernel(x_prompt, x_sample, cache_mla_ckv, cache_mla_krope, state_s5, state_rwkv, state_rwkv_shift, w_in, q_norm_g, w_qb, kv_norm_g, w_kvb, lam_re, lam_im, log_dt, b_re, b_im, c_re, c_im, s5_d, w_glu, b_glu, mu_shift, w0, w_w2, a0, w_a2, w_g2, k_k, k_a, r_k, gn_g, gn_b, w_out, ln1_g, ln1_b, w_up, w_down, ln2_g, ln2_b):
    depth = w_in.shape[0]
    alpha = (2 * depth) ** 0.25
    bp, sp = x_prompt.shape[:2]
    ts = x_sample.shape[1]
    past = cache_mla_ckv.shape[2]
    tabs_p = _rope_tables(jnp.arange(sp))
    tabs_s = _rope_tables(past + jnp.arange(ts))
    names = dict(w_in=w_in, q_norm_g=q_norm_g, w_qb=w_qb, kv_norm_g=kv_norm_g, w_kvb=w_kvb, lam_re=lam_re,
                 lam_im=lam_im, log_dt=log_dt, b_re=b_re, b_im=b_im, c_re=c_re, c_im=c_im, s5_d=s5_d, w_glu=w_glu,
                 b_glu=b_glu, mu_shift=mu_shift, w0=w0, w_w2=w_w2, a0=a0, w_a2=w_a2, w_g2=w_g2, k_k=k_k, k_a=k_a,
                 r_k=r_k, gn_g=gn_g, gn_b=gn_b, w_out=w_out, ln1_g=ln1_g, ln1_b=ln1_b, w_up=w_up, w_down=w_down,
                 ln2_g=ln2_g, ln2_b=ln2_b)
    s5_zero = jnp.zeros((bp, S5_GROUPS, S5_STATE, 2), F32)
    rwkv_zero = jnp.zeros((bp, RWKV_HEADS, RWKV_HEAD, RWKV_HEAD), F32)
    shift_zero = jnp.zeros((bp, 1, RWKV_PROJ), F32)

    xp, xs = x_prompt, x_sample
    outs_p = [[] for _ in range(5)]
    outs_s = [[] for _ in range(5)]
    for l in range(depth):
        w = _prep_layer({k: v[l] for k, v in names.items()})
        xp, *rest = _trunk_layer(xp, tabs_p, None, s5_zero, rwkv_zero, shift_zero, w, alpha)
        for acc, val in zip(outs_p, rest):
            acc.append(val)
        past_kv = _kvup(cache_mla_ckv[l], cache_mla_krope[l], w['kvup'])
        xs, *rest = _trunk_layer(xs, tabs_s, past_kv, state_s5[l], state_rwkv[l], state_rwkv_shift[l], w, alpha)
        for acc, val in zip(outs_s, rest):
            acc.append(val)
    return (xp, xs, *[jnp.stack(a) for a in outs_p], *[jnp.stack(a) for a in outs_s])
```

```python
import functools
import math

import jax
import jax.numpy as jnp
from jax import lax
from jax.experimental import pallas as pl
from jax.experimental.pallas import tpu as pltpu

F32 = jnp.float32
BF16 = jnp.bfloat16

CHUNK = 64
MLA_HEADS = 6
NOPE_DIM = 64
ROPE_DIM = 32
V_DIM = 64
Q_RANK = 256
KV_RANK = 128
ROPE_BASE = 10000.0
MLA_SCALE = (NOPE_DIM + ROPE_DIM) ** -0.5
S5_GROUP_CH = 16
S5_W = 256
S5_GROUPS = S5_W // S5_GROUP_CH
S5_STATE = 64
S5_N = S5_GROUPS * S5_STATE
RWKV_HEADS = 6
RWKV_HEAD = 64
RWKV_W = RWKV_HEADS * RWKV_HEAD
DECAY_LORA = 32
AAA_LORA = 32
GATE_LORA = 64
RWKV_PROJ = 3 * RWKV_W + DECAY_LORA + AAA_LORA + GATE_LORA
LN_EPS = 1e-5
RMS_EPS = 1e-6
GN_EPS = 64e-5
NEG_INF = -1e30

V7X_LANES = 128
V7X_VMEM_BYTES = 64 * 1024 * 1024
VMEM_LIMIT_BYTES = V7X_VMEM_BYTES - 8 * 1024 * 1024
HEAD_PAD = V7X_LANES
ATTN_BLOCK = 512
ATTN_HEAD_GROUP = 3
V_ROWS = V_DIM + 16
LOG2_E = 1.4426950408889634
RWKV_PACK = 8


def _cparams(*sem):
    return pltpu.CompilerParams(dimension_semantics=sem, vmem_limit_bytes=VMEM_LIMIT_BYTES)


def _const_spec(shape):
    zeros = (0,) * len(shape)
    return pl.BlockSpec(shape, lambda *_: zeros)


def _tile(n, target):
    if n <= target:
        return n
    t = target
    while n % t:
        t //= 2
    return t


def _layer_norm(y, g, b):
    mu = jnp.mean(y, -1, keepdims=True)
    yc = y - mu
    var = jnp.mean(yc * yc, -1, keepdims=True)
    return yc * lax.rsqrt(var + LN_EPS) * g + b


def _inproj_kernel(x_ref, wm_ref, wu_ref, wp_ref, wq1_ref, wq2_ref, qg_ref, kg_ref, csk_ref, cq_ref, sq_ref,
                   qcat_ref, ckv_ref, krope_ref, u_ref, p_ref):
    x = x_ref[0].astype(BF16)
    m = jnp.dot(x, wm_ref[...], preferred_element_type=F32)
    q_lat = m[:, :Q_RANK]
    qn = q_lat * lax.rsqrt(jnp.mean(q_lat * q_lat, -1, keepdims=True) + RMS_EPS) * qg_ref[...]
    kv_lat = m[:, Q_RANK:Q_RANK + KV_RANK]
    ckv_ref[0] = kv_lat * lax.rsqrt(jnp.mean(kv_lat * kv_lat, -1, keepdims=True) + RMS_EPS) * kg_ref[...]
    kr = m[:, Q_RANK + KV_RANK:] * csk_ref[...]
    kr = kr + pltpu.roll(kr, V7X_LANES - ROPE_DIM, 1)
    krope_ref[0] = kr[:, :ROPE_DIM]
    qb = qn.astype(BF16)
    q1 = jnp.dot(qb, wq1_ref[...], preferred_element_type=F32)
    q2 = jnp.dot(qb, wq2_ref[...], preferred_element_type=F32)
    cq = cq_ref[...]
    sq = sq_ref[...]
    for h in range(MLA_HEADS):
        sl = slice(h * HEAD_PAD, (h + 1) * HEAD_PAD)
        qcat_ref[0, :, sl] = (q1[:, sl] * cq + q2[:, sl] * sq).astype(BF16)
    u_ref[...] = jnp.dot(x, wu_ref[...], preferred_element_type=F32)
    p_ref[...] = jnp.dot(x, wp_ref[...], preferred_element_type=F32)


def _inproj(x, w, tabs):
    B, T, D = x.shape
    tt = _tile(T, 512)
    nq = MLA_HEADS * HEAD_PAD
    grid = (B, T // tt)
    row = lambda b, i: (b, i, 0)
    tab = lambda b, i: (i, 0)
    tm = lambda b, i: (i, b)
    return pl.pallas_call(
        _inproj_kernel,
        grid=grid,
        in_specs=[
            pl.BlockSpec((1, tt, D), row),
            _const_spec(w['wm'].shape), _const_spec(w['wu'].shape), _const_spec(w['wp'].shape),
            _const_spec(w['wq1'].shape), _const_spec(w['wq2'].shape),
            _const_spec((1, Q_RANK)), _const_spec((1, KV_RANK)),
            pl.BlockSpec((tt, V7X_LANES), tab), pl.BlockSpec((tt, HEAD_PAD), tab), pl.BlockSpec((tt, HEAD_PAD), tab),
        ],
        out_specs=[
            pl.BlockSpec((1, tt, nq), row),
            pl.BlockSpec((1, tt, KV_RANK), row),
            pl.BlockSpec((1, tt, ROPE_DIM), row),
            pl.BlockSpec((tt, S5_W), tm),
            pl.BlockSpec((tt, RWKV_PROJ), tm),
        ],
        out_shape=[
            jax.ShapeDtypeStruct((B, T, nq), BF16),
            jax.ShapeDtypeStruct((B, T, KV_RANK), F32),
            jax.ShapeDtypeStruct((B, T, ROPE_DIM), F32),
            jax.ShapeDtypeStruct((T, B * S5_W), F32),
            jax.ShapeDtypeStruct((T, B * RWKV_PROJ), F32),
        ],
        compiler_params=_cparams("parallel", "parallel"),
        name="inproj",
    )(x, w['wm'], w['wu'], w['wp'], w['wq1'], w['wq2'], w['qg'], w['kg'], tabs['csk'], tabs['cq'], tabs['sq'])


def _kvup_kernel(ckv_ref, kr_ref, wk_ref, pk_ref, wvt_ref, kcat_ref, vt_ref):
    c = ckv_ref[0].astype(BF16)
    kr = kr_ref[0].astype(BF16)
    kcat = jnp.dot(c, wk_ref[...], preferred_element_type=F32) + jnp.dot(kr, pk_ref[...], preferred_element_type=F32)
    kcat_ref[0] = kcat.astype(BF16)
    vt = lax.dot_general(wvt_ref[...], c, (((1,), (1,)), ((), ())), preferred_element_type=F32).astype(BF16)
    ones = jnp.ones((V_ROWS - V_DIM, vt.shape[1]), BF16)
    for h in range(MLA_HEADS):
        vt_ref[0, h, 0, :V_DIM, :] = vt[h * V_DIM:(h + 1) * V_DIM, :]
        vt_ref[0, h, 0, V_DIM:, :] = ones


def _kvup(ckv, krope, w):
    B, T, _ = ckv.shape
    tt = _tile(T, ATTN_BLOCK)
    nk = MLA_HEADS * HEAD_PAD
    row = lambda b, i: (b, i, 0)
    return pl.pallas_call(
        _kvup_kernel,
        grid=(B, T // tt),
        in_specs=[pl.BlockSpec((1, tt, KV_RANK), row), pl.BlockSpec((1, tt, ROPE_DIM), row),
                  _const_spec(w['wk'].shape), _const_spec(w['pk'].shape), _const_spec(w['wvt'].shape)],
        out_specs=[pl.BlockSpec((1, tt, nk), row),
                   pl.BlockSpec((1, MLA_HEADS, 1, V_ROWS, tt), lambda b, i: (b, 0, i, 0, 0))],
        out_shape=[jax.ShapeDtypeStruct((B, T, nk), BF16),
                   jax.ShapeDtypeStruct((B, MLA_HEADS, T // tt, V_ROWS, tt), BF16)],
        compiler_params=_cparams("parallel", "parallel"),
        name="kvup",
    )(ckv, krope, w['wk'], w['pk'], w['wvt'])


def _attn_kernel(q_ref, k_ref, v_ref, o_ref, sa_ref, sb_ref, m_ref, acc_ref, *, bq, bk, kv_len, chunk_causal):
    hg = ATTN_HEAD_GROUP
    qi = pl.program_id(2)
    last = qi if chunk_causal else jnp.int32(-(-kv_len // bk) - 1)
    nt = (((1,), (1,)), ((), ()))

    def scores(j, s_ref):
        r0 = pl.multiple_of(j * bk, bk)
        for h in range(hg):
            sl = slice(h * HEAD_PAD, (h + 1) * HEAD_PAD)
            s_ref[h] = lax.dot_general(k_ref[0, pl.ds(r0, bk), sl], q_ref[0, :, sl], nt,
                                       preferred_element_type=F32)

    def softmax_pv(j, s_ref, masked):
        if masked:
            kidx = j * bk + lax.broadcasted_iota(jnp.int32, (bk, bq), 0)
            ok = kidx < kv_len
            if chunk_causal:
                qidx = qi * bq + lax.broadcasted_iota(jnp.int32, (bk, bq), 1)
                ok = jnp.logical_and(ok, (kidx // CHUNK) <= (qidx // CHUNK))
        for h in range(hg):
            s = s_ref[h]
            if masked:
                s = jnp.where(ok, s, NEG_INF)
            m = m_ref[h]
            m_new = jnp.maximum(m, jnp.max(s, axis=0, keepdims=True))
            p = jnp.exp2(s - m_new)
            alpha = jnp.exp2(m - m_new)
            acc_ref[h] = alpha * acc_ref[h] + jnp.dot(v_ref[0, h, j], p.astype(BF16),
                                                      preferred_element_type=F32)
            m_ref[h] = m_new

    m_ref[...] = jnp.full(m_ref.shape, NEG_INF, F32)
    acc_ref[...] = jnp.zeros(acc_ref.shape, F32)
    scores(0, sa_ref)

    def pair(p, _):
        j = 2 * p
        scores(j + 1, sb_ref)
        softmax_pv(j, sa_ref, False)
        scores(j + 2, sa_ref)
        softmax_pv(j + 1, sb_ref, False)
        return 0

    lax.fori_loop(0, last // 2, pair, 0)

    @pl.when(last % 2 == 0)
    def _():
        softmax_pv(last, sa_ref, True)

    @pl.when(last % 2 == 1)
    def _():
        scores(last, sb_ref)
        softmax_pv(last - 1, sa_ref, False)
        softmax_pv(last, sb_ref, True)

    for h in range(hg):
        o_ref[0, h] = (acc_ref[h, :V_DIM, :] / acc_ref[h, V_DIM:V_DIM + 1, :]).astype(BF16)


def _attention(qcat, kcat, vt, *, kv_len, chunk_causal):
    B, Tq, _ = qcat.shape
    Tk = kcat.shape[1]
    nblk, bk = vt.shape[2], vt.shape[4]
    assert nblk * bk == Tk
    bq = bk if chunk_causal else Tq
    hg = ATTN_HEAD_GROUP
    out_t = pl.pallas_call(
        functools.partial(_attn_kernel, bq=bq, bk=bk, kv_len=kv_len, chunk_causal=chunk_causal),
        grid=(B, MLA_HEADS // hg, Tq // bq),
        in_specs=[
            pl.BlockSpec((1, bq, hg * HEAD_PAD), lambda b, g, i: (b, i, g)),
            pl.BlockSpec((1, Tk, hg * HEAD_PAD), lambda b, g, i: (b, 0, g)),
            pl.BlockSpec((1, hg, nblk, V_ROWS, bk), lambda b, g, i: (b, g, 0, 0, 0)),
        ],
        out_specs=pl.BlockSpec((1, hg, V_DIM, bq), lambda b, g, i: (b, g, 0, i)),
        out_shape=jax.ShapeDtypeStruct((B, MLA_HEADS, V_DIM, Tq), BF16),
        scratch_shapes=[pltpu.VMEM((hg, bk, bq), F32), pltpu.VMEM((hg, bk, bq), F32),
                        pltpu.VMEM((hg, 1, bq), F32), pltpu.VMEM((hg, V_ROWS, bq), F32)],
        compiler_params=_cparams("parallel", "parallel", "arbitrary"),
        name="mla_attention",
    )(qcat, kcat, vt)
    return out_t.transpose(0, 3, 1, 2).reshape(B, Tq, MLA_HEADS * V_DIM)


def _gelu_tanh(x):
    return 0.5 * x * (1.0 + jnp.tanh(math.sqrt(2.0 / math.pi) * (x + 0.044715 * (x * x * x))))


def _s5_kernel(u_ref, x0_ref, a_ref, wb_ref, wc_ref, d_ref, wg_ref, bg_ref, o_ref, xl_ref, xs_ref, st_ref, *, L, B):
    @pl.when(pl.program_id(0) == 0)
    def _():
        st_ref[...] = x0_ref[...]

    u = u_ref[...].reshape(L * B, S5_W)
    xs_ref[...] = jnp.dot(u.astype(BF16), wb_ref[...], preferred_element_type=F32)
    a_re = a_ref[0:1, :]
    a_im = a_ref[1:2, :]

    def body(t, carry):
        x_re, x_im = carry
        r0 = pl.multiple_of(t * B, B)
        n_re = a_re * x_re - a_im * x_im + xs_ref[pl.ds(r0, B), :S5_N]
        n_im = a_re * x_im + a_im * x_re + xs_ref[pl.ds(r0, B), S5_N:]
        xs_ref[pl.ds(r0, B), :S5_N] = n_re
        xs_ref[pl.ds(r0, B), S5_N:] = n_im
        return n_re, n_im

    x_re, x_im = lax.fori_loop(0, L, body, (st_ref[:, :S5_N], st_ref[:, S5_N:]))
    st_ref[:, :S5_N] = x_re
    st_ref[:, S5_N:] = x_im
    xl_ref[...] = st_ref[...]

    y = jnp.dot(xs_ref[...].astype(BF16), wc_ref[...], preferred_element_type=F32) + d_ref[...] * u
    z = _gelu_tanh(y)
    gate = jax.nn.sigmoid(jnp.dot(z.astype(BF16), wg_ref[...], preferred_element_type=F32) + bg_ref[...])
    o_ref[...] = (z * gate).reshape(L, B, S5_W)


def _s5(u_tm, x0, w):
    T, B, _ = u_tm.shape
    L = _tile(T, 1024 // B)
    return pl.pallas_call(
        functools.partial(_s5_kernel, L=L, B=B),
        grid=(T // L,),
        in_specs=[
            pl.BlockSpec((L, B, S5_W), lambda i: (i, 0, 0)),
            _const_spec((B, 2 * S5_N)), _const_spec((2, S5_N)),
            _const_spec((S5_W, 2 * S5_N)), _const_spec((2 * S5_N, S5_W)),
            _const_spec((1, S5_W)), _const_spec((S5_W, S5_W)), _const_spec((1, S5_W)),
        ],
        out_specs=[pl.BlockSpec((L, B, S5_W), lambda i: (i, 0, 0)), _const_spec((B, 2 * S5_N))],
        out_shape=[jax.ShapeDtypeStruct((T, B, S5_W), F32), jax.ShapeDtypeStruct((B, 2 * S5_N), F32)],
        scratch_shapes=[pltpu.VMEM((L * B, 2 * S5_N), F32), pltpu.VMEM((B, 2 * S5_N), F32)],
        compiler_params=_cparams("arbitrary"),
        name="s5_scan",
    )(u_tm, x0, w['a'], w['wb'], w['wc'], w['d'], w['wglu'], w['bglu'])


def _softplus(x):
    return jnp.maximum(x, 0.0) + jnp.log(1.0 + jnp.exp(-jnp.abs(x)))


def _head_sum(x, ones_ref):
    return jnp.dot(x, ones_ref[...], preferred_element_type=F32, precision=lax.Precision.HIGHEST)


def _rwkv_prep_kernel(p_ref, sh0_ref, mu_ref, ww_ref, wa_ref, wg_ref, w0_ref, a0_ref, kk_ref, ka_ref, rk_ref,
                      ones_ref, r_o, w_o, k_o, v_o, n_o, b_o, g_o, bon_o, sh_o, last_ref, *, L, B):
    @pl.when(pl.program_id(0) == 0)
    def _():
        last_ref[...] = sh0_ref[...]

    p = p_ref[...].reshape(L * B, RWKV_PROJ)
    if L > 1:
        prev = jnp.concatenate([last_ref[...], p[:(L - 1) * B]], axis=0)
    else:
        prev = last_ref[...]
    last_ref[...] = p[(L - 1) * B:]
    sh_o[...] = p[(L - 1) * B:]
    ps = p + (prev - p) * mu_ref[...]
    r = ps[:, :RWKV_W]
    k = ps[:, RWKV_W:2 * RWKV_W]
    v = ps[:, 2 * RWKV_W:3 * RWKV_W]
    tail = ps[:, 3 * RWKV_W:]
    lw = jnp.dot(jnp.tanh(tail).astype(BF16), ww_ref[...], preferred_element_type=F32)
    la = jnp.dot(tail.astype(BF16), wa_ref[...], preferred_element_type=F32)
    g = jnp.dot(jax.nn.sigmoid(tail).astype(BF16), wg_ref[...], preferred_element_type=F32)
    w_log = -_softplus(-(w0_ref[...] + lw)) - 0.5
    decay = jnp.exp(-jnp.exp(w_log))
    a = jax.nn.sigmoid(a0_ref[...] + la)
    kk = k * kk_ref[...]
    nrm = jnp.maximum(jnp.sqrt(_head_sum(kk * kk, ones_ref)), 1e-12)
    kk = kk / nrm
    k2 = k * (1.0 + (a - 1.0) * ka_ref[...])
    bonus = _head_sum(r * k2 * rk_ref[...], ones_ref) * v
    shp = (L, B, RWKV_W)
    r_o[...] = r.reshape(shp)
    w_o[...] = decay.reshape(shp)
    k_o[...] = k2.reshape(shp)
    v_o[...] = v.reshape(shp)
    n_o[...] = (-kk).reshape(shp)
    b_o[...] = (kk * a).reshape(shp)
    g_o[...] = g.reshape(shp)
    bon_o[...] = bonus.reshape(shp)


def _rwkv_prep(p_tm, shift0, w):
    T, B, _ = p_tm.shape
    L = _tile(T, 512 // B)
    blk = pl.BlockSpec((L, B, RWKV_W), lambda i: (i, 0, 0))
    vec = _const_spec((1, RWKV_W))
    outs = pl.pallas_call(
        functools.partial(_rwkv_prep_kernel, L=L, B=B),
        grid=(T // L,),
        in_specs=[pl.BlockSpec((L, B, RWKV_PROJ), lambda i: (i, 0, 0)), _const_spec((B, RWKV_PROJ)),
                  _const_spec((1, RWKV_PROJ)),
                  _const_spec((V7X_LANES, RWKV_W)), _const_spec((V7X_LANES, RWKV_W)), _const_spec((V7X_LANES, RWKV_W)),
                  vec, vec, vec, vec, vec, _const_spec((RWKV_W, RWKV_W))],
        out_specs=[blk] * 8 + [_const_spec((B, RWKV_PROJ))],
        out_shape=[jax.ShapeDtypeStruct((T, B, RWKV_W), F32)] * 8 + [jax.ShapeDtypeStruct((B, RWKV_PROJ), F32)],
        scratch_shapes=[pltpu.VMEM((B, RWKV_PROJ), F32)],
        compiler_params=_cparams("arbitrary"),
        name="rwkv_prep",
    )(p_tm, shift0, w['mu'], w['ww'], w['wa'], w['wg'], w['w0'], w['a0'], w['kk'], w['ka'], w['rk'], w['ones'])
    return outs


def _lane_window(slab, off, width, dst, lane):
    v0, lo = divmod(off, V7X_LANES)
    shift = (dst - lo) % V7X_LANES
    a = slab[:, v0 * V7X_LANES:(v0 + 1) * V7X_LANES]
    r = pltpu.roll(a, shift, 1) if shift else a
    if lo + width > V7X_LANES:
        b = slab[:, (v0 + 1) * V7X_LANES:(v0 + 2) * V7X_LANES]
        r2 = pltpu.roll(b, shift, 1) if shift else b
        r = jnp.where(lane < dst + (V7X_LANES - lo), r, r2)
    return r


def _chain_lanes(pieces, bh, lane):
    rep = len(pieces)
    out = jnp.where(lane < rep * bh, pieces[rep - 1], 0.0)
    for r in range(rep - 2, -1, -1):
        out = jnp.where(lane < (r + 1) * bh, pieces[r], out)
    return out


def _rwkv_scan_kernel(w_ref, b_ref, k_ref, r_ref, nn_ref, nnx_ref, v_ref, s0_ref, y_ref, so_ref,
                      s_ref, sa_ref, e_ref, nna_ref, *, Tm, IL, bh, rep):
    lane = lax.broadcasted_iota(jnp.int32, (RWKV_HEAD, V7X_LANES), 1)
    lane_il = lax.broadcasted_iota(jnp.int32, (IL, V7X_LANES), 1)
    ncol = RWKV_PACK * bh // V7X_LANES

    def expand(slab, t8):
        return _chain_lanes([_lane_window(slab, t8 * bh, bh, r * bh, lane) for r in range(rep)], bh, lane)

    nna_ref[0:Tm] = nn_ref[...]
    nna_ref[Tm] = nnx_ref[0]

    @pl.when(pl.program_id(0) == 0)
    def _():
        s_ref[...] = s0_ref[...]
        e_ref[0, 4] = expand(nn_ref[0], 0)
        acc = jnp.zeros((IL, V7X_LANES), F32)
        for j in range(RWKV_HEAD):
            acc = acc + s0_ref[j] * e_ref[0, 4, j:j + 1, :]
        sa_ref[...] = acc

    def body(m, sa):
        slabs = [w_ref[m], b_ref[m], k_ref[m], r_ref[m]]
        nn0 = nna_ref[m]
        nn1 = nna_ref[m + 1]
        vs = v_ref[m]
        for t8 in range(RWKV_PACK):
            for o, slab in enumerate(slabs):
                e_ref[t8, o] = expand(slab, t8)
            e_ref[t8, 4] = expand(nn0, t8 + 1) if t8 + 1 < RWKV_PACK else expand(nn1, 0)
        ycols = [[jnp.zeros((IL, V7X_LANES), F32) for _ in range(rep)] for _ in range(ncol)]
        for t8 in range(RWKV_PACK):
            off = t8 * bh
            vt = _chain_lanes([_lane_window(vs[r * IL:(r + 1) * IL], off, bh, r * bh, lane_il) for r in range(rep)],
                              bh, lane_il)
            yacc = jnp.zeros((IL, V7X_LANES), F32)
            san = jnp.zeros((IL, V7X_LANES), F32)
            for j in range(RWKV_HEAD):
                row = lambda o: e_ref[t8, o, j:j + 1, :]
                sn = s_ref[j] * row(0) + sa * row(1) + vt * row(2)
                s_ref[j] = sn
                yacc = yacc + sn * row(3)
                san = san + sn * row(4)
            sa = san
            for r in range(rep):
                shift = (off - r * bh) % V7X_LANES
                rolled = pltpu.roll(yacc, shift, 1) if shift else yacc
                for c in range(ncol):
                    lo, hi = max(off, c * V7X_LANES), min(off + bh, (c + 1) * V7X_LANES)
                    if lo < hi:
                        inside = jnp.logical_and(lane_il >= lo - c * V7X_LANES, lane_il < hi - c * V7X_LANES)
                        ycols[c][r] = jnp.where(inside, rolled, ycols[c][r])
        for c in range(ncol):
            for r in range(rep):
                y_ref[m, r * IL:(r + 1) * IL, c * V7X_LANES:(c + 1) * V7X_LANES] = ycols[c][r]
        return sa

    sa_ref[...] = lax.fori_loop(0, Tm, body, sa_ref[...])
    so_ref[...] = s_ref[...]


def _rwkv_scan(ops, v_p, s0_l, bh, rep):
    TP, _, W = v_p.shape
    IL = RWKV_HEAD // rep
    Tm = _tile(TP, 8)
    blk = pl.BlockSpec((Tm, RWKV_HEAD, W), lambda i: (i, 0, 0))
    nxt = pl.BlockSpec((1, RWKV_HEAD, W), lambda i: (jnp.minimum((i + 1) * Tm, TP - 1), 0, 0))
    sblk = _const_spec((RWKV_HEAD, IL, V7X_LANES))
    return pl.pallas_call(
        functools.partial(_rwkv_scan_kernel, Tm=Tm, IL=IL, bh=bh, rep=rep),
        grid=(TP // Tm,),
        in_specs=[blk] * 5 + [nxt, blk, sblk],
        out_specs=[blk, sblk],
        out_shape=[jax.ShapeDtypeStruct((TP, RWKV_HEAD, W), F32),
                   jax.ShapeDtypeStruct((RWKV_HEAD, IL, V7X_LANES), F32)],
        scratch_shapes=[pltpu.VMEM((RWKV_HEAD, IL, V7X_LANES), F32), pltpu.VMEM((IL, V7X_LANES), F32),
                        pltpu.VMEM((RWKV_PACK, 5, RWKV_HEAD, V7X_LANES), F32),
                        pltpu.VMEM((Tm + 1, RWKV_HEAD, W), F32)],
        compiler_params=_cparams("arbitrary"),
        name="rwkv_scan",
    )(*ops, ops[4], v_p, s0_l)


def _rwkv_post_kernel(y_ref, bon_ref, g_ref, gg_ref, gb_ref, ones_ref, o_ref, *, L, B):
    y = y_ref[...].reshape(L * B, RWKV_W)
    inv_n = 1.0 / RWKV_HEAD
    mu = _head_sum(y, ones_ref) * inv_n
    yc = y - mu
    var = _head_sum(yc * yc, ones_ref) * inv_n
    yn = yc * lax.rsqrt(var + GN_EPS) * gg_ref[...] + gb_ref[...]
    out = (yn + bon_ref[...].reshape(L * B, RWKV_W)) * g_ref[...].reshape(L * B, RWKV_W)
    o_ref[...] = out.reshape(L, B, RWKV_W)


def _rwkv_post(y_tm, bonus, g, w):
    T, B, _ = y_tm.shape
    L = _tile(T, 1024 // B)
    blk = pl.BlockSpec((L, B, RWKV_W), lambda i: (i, 0, 0))
    vec = _const_spec((1, RWKV_W))
    return pl.pallas_call(
        functools.partial(_rwkv_post_kernel, L=L, B=B),
        grid=(T // L,),
        in_specs=[blk, blk, blk, vec, vec, _const_spec((RWKV_W, RWKV_W))],
        out_specs=blk,
        out_shape=jax.ShapeDtypeStruct((T, B, RWKV_W), F32),
        compiler_params=_cparams("parallel"),
        name="rwkv_post",
    )(y_tm, bonus, g, w['gng'], w['gnb'], w['ones'])


def _rwkv_layout(B):
    bh = B * RWKV_HEADS
    rep = 1
    while 2 * rep * bh <= V7X_LANES and RWKV_HEAD % (2 * rep) == 0:
        rep *= 2
    return bh, rep, RWKV_HEAD // rep


def _rwkv_mixer(p_tm, shift0, s0, w):
    T, B, _ = p_tm.shape
    bh, rep, IL = _rwkv_layout(B)
    lanes = rep * bh
    r, dec, k2, v, nkk, beta, g, bonus, shift = _rwkv_prep(p_tm, shift0, w)
    TP = T // RWKV_PACK

    def pack(x):
        x = x.reshape(TP, RWKV_PACK, bh, RWKV_HEAD).transpose(0, 3, 1, 2)
        return x.reshape(TP, RWKV_HEAD, RWKV_PACK * bh)

    ops = [pack(dec), pack(beta), pack(k2), pack(r), pack(nkk)]
    s0_l = s0.reshape(bh, rep, IL, RWKV_HEAD).transpose(3, 2, 1, 0).reshape(RWKV_HEAD, IL, lanes)
    s0_l = jnp.pad(s0_l, ((0, 0), (0, 0), (0, V7X_LANES - lanes)))
    y_p, s_l = _rwkv_scan(ops, pack(v), s0_l, bh, rep)
    y_tm = y_p.reshape(TP, RWKV_HEAD, RWKV_PACK, bh).transpose(0, 2, 3, 1).reshape(T, B, RWKV_W)
    s_last = s_l[:, :, :lanes].reshape(RWKV_HEAD, IL, rep, bh).transpose(3, 2, 1, 0)
    s_last = s_last.reshape(B, RWKV_HEADS, RWKV_HEAD, RWKV_HEAD)
    out = _rwkv_post(y_tm, bonus, g, w)
    return out, s_last, shift


def _outproj_kernel(m_ref, s_ref, r_ref, x_ref, wm_ref, ws_ref, wr_ref, g_ref, b_ref, o_ref, *, alpha):
    acc = jnp.dot(m_ref[0], wm_ref[...], preferred_element_type=F32)
    acc = acc + jnp.dot(s_ref[...].astype(BF16), ws_ref[...], preferred_element_type=F32)
    acc = acc + jnp.dot(r_ref[...].astype(BF16), wr_ref[...], preferred_element_type=F32)
    o_ref[0] = _layer_norm(alpha * x_ref[0] + acc, g_ref[...], b_ref[...])


def _outproj(mla, s5_tm, rw_tm, x, w, alpha):
    B, T, D = x.shape
    tt = _tile(T, 512)
    row = lambda b, i: (b, i, 0)
    tm = lambda b, i: (i, b)
    return pl.pallas_call(
        functools.partial(_outproj_kernel, alpha=alpha),
        grid=(B, T // tt),
        in_specs=[pl.BlockSpec((1, tt, MLA_HEADS * V_DIM), row), pl.BlockSpec((tt, S5_W), tm),
                  pl.BlockSpec((tt, RWKV_W), tm), pl.BlockSpec((1, tt, D), row),
                  _const_spec(w['wo_m'].shape), _const_spec(w['wo_s'].shape), _const_spec(w['wo_r'].shape),
                  _const_spec((1, D)), _const_spec((1, D))],
        out_specs=pl.BlockSpec((1, tt, D), row),
        out_shape=jax.ShapeDtypeStruct((B, T, D), F32),
        compiler_params=_cparams("parallel", "parallel"),
        name="outproj_ln",
    )(mla, s5_tm, rw_tm, x, w['wo_m'], w['wo_s'], w['wo_r'], w['ln1g'], w['ln1b'])


def _mlp_kernel(x_ref, wu_ref, wd_ref, g_ref, b_ref, o_ref, *, alpha, fc):
    x = x_ref[0]
    xb = x.astype(BF16)
    acc = alpha * x
    for c in range(wu_ref.shape[1] // fc):
        h = jnp.maximum(jnp.dot(xb, wu_ref[:, c * fc:(c + 1) * fc], preferred_element_type=F32), 0.0)
        acc = acc + jnp.dot((h * h).astype(BF16), wd_ref[c * fc:(c + 1) * fc, :], preferred_element_type=F32)
    o_ref[0] = _layer_norm(acc, g_ref[...], b_ref[...])


def _mlp(x, w, alpha):
    B, T, D = x.shape
    tt = _tile(T, 512)
    row = lambda b, i: (b, i, 0)
    return pl.pallas_call(
        functools.partial(_mlp_kernel, alpha=alpha, fc=1024),
        grid=(B, T // tt),
        in_specs=[pl.BlockSpec((1, tt, D), row), _const_spec(w['wup'].shape), _const_spec(w['wdown'].shape),
                  _const_spec((1, D)), _const_spec((1, D))],
        out_specs=pl.BlockSpec((1, tt, D), row),
        out_shape=jax.ShapeDtypeStruct((B, T, D), F32),
        compiler_params=_cparams("parallel", "parallel"),
        name="mlp_ln",
    )(x, w['wup'], w['wdown'], w['ln2g'], w['ln2b'])


def _rot_cols(wr):
    half = ROPE_DIM // 2
    return jnp.concatenate([-wr[..., half:], wr[..., :half]], axis=-1)


def _prep_layer(p):
    D = p['w_in'].shape[0]
    w_in = p['w_in']
    o_kv = Q_RANK
    o_kr = Q_RANK + KV_RANK
    o_u = o_kr + ROPE_DIM
    o_p = o_u + S5_W
    w_kr = w_in[:, o_kr:o_u]
    wm = jnp.concatenate([w_in[:, :o_kr], w_kr, _rot_cols(w_kr),
                          jnp.zeros((D, V7X_LANES - 2 * ROPE_DIM), F32)], axis=1)
    wqb = p['w_qb'].reshape(Q_RANK, MLA_HEADS, NOPE_DIM + ROPE_DIM) * (MLA_SCALE * LOG2_E)
    zq = jnp.zeros((Q_RANK, MLA_HEADS, HEAD_PAD - NOPE_DIM - ROPE_DIM), F32)
    wq1 = jnp.concatenate([wqb, zq], axis=-1).reshape(Q_RANK, MLA_HEADS * HEAD_PAD)
    wq2 = jnp.concatenate([jnp.zeros((Q_RANK, MLA_HEADS, NOPE_DIM), F32), _rot_cols(wqb[..., NOPE_DIM:]), zq],
                          axis=-1).reshape(Q_RANK, MLA_HEADS * HEAD_PAD)
    wkvb = p['w_kvb'].reshape(KV_RANK, MLA_HEADS, NOPE_DIM + V_DIM)
    wk = jnp.concatenate([wkvb[..., :NOPE_DIM], jnp.zeros((KV_RANK, MLA_HEADS, HEAD_PAD - NOPE_DIM), F32)],
                         axis=-1).reshape(KV_RANK, MLA_HEADS * HEAD_PAD)
    pk = jnp.concatenate([jnp.zeros((ROPE_DIM, NOPE_DIM), F32), jnp.eye(ROPE_DIM, dtype=F32),
                          jnp.zeros((ROPE_DIM, HEAD_PAD - NOPE_DIM - ROPE_DIM), F32)], axis=1)
    pk = jnp.tile(pk, (1, MLA_HEADS))
    wvt = wkvb[..., NOPE_DIM:].reshape(KV_RANK, MLA_HEADS * V_DIM).T

    lr, li = p['lam_re'], p['lam_im']
    dt = jnp.exp(p['log_dt'])[:, None]
    mag = jnp.exp(lr * dt)
    ar, ai = mag * jnp.cos(li * dt), mag * jnp.sin(li * dt)
    den = lr * lr + li * li
    cr = ((ar - 1.0) * lr + ai * li) / den
    ci = (ai * lr - (ar - 1.0) * li) / den
    bbr = cr[..., None] * p['b_re'] - ci[..., None] * p['b_im']
    bbi = cr[..., None] * p['b_im'] + ci[..., None] * p['b_re']
    eye_g = jnp.eye(S5_GROUPS, dtype=F32)
    bd_in = lambda m: (m.transpose(0, 2, 1)[:, :, None, :] * eye_g[:, None, :, None]).reshape(S5_W, S5_N)
    bd_out = lambda m: (m.transpose(0, 2, 1)[:, :, None, :] * eye_g[:, None, :, None]).reshape(S5_N, S5_W)
    wb = jnp.concatenate([bd_in(bbr), bd_in(bbi)], axis=1)
    wc = jnp.concatenate([bd_out(p['c_re']), -bd_out(p['c_im'])], axis=0)
    a = jnp.stack([ar.reshape(S5_N), ai.reshape(S5_N)])

    def lora_pad(w, off):
        z = jnp.zeros((V7X_LANES, RWKV_W), F32)
        return z.at[off:off + w.shape[0]].set(w)

    seg = jnp.arange(RWKV_W) // RWKV_HEAD
    ones = (seg[:, None] == seg[None, :]).astype(F32)
    row = lambda v: v.reshape(1, -1).astype(F32)
    wo = p['w_out']
    return dict(
        inproj=dict(wm=wm.astype(BF16), wu=w_in[:, o_u:o_p].astype(BF16), wp=w_in[:, o_p:].astype(BF16),
                    wq1=wq1.astype(BF16), wq2=wq2.astype(BF16), qg=row(p['q_norm_g']), kg=row(p['kv_norm_g'])),
        kvup=dict(wk=wk.astype(BF16), pk=pk.astype(BF16), wvt=wvt.astype(BF16)),
        s5=dict(a=a, wb=wb.astype(BF16), wc=wc.astype(BF16), d=row(p['s5_d']), wglu=p['w_glu'].astype(BF16),
                bglu=row(p['b_glu'])),
        rwkv=dict(mu=row(p['mu_shift']), ww=lora_pad(p['w_w2'], 0).astype(BF16),
                  wa=lora_pad(p['w_a2'], DECAY_LORA).astype(BF16),
                  wg=lora_pad(p['w_g2'], DECAY_LORA + AAA_LORA).astype(BF16),
                  w0=row(p['w0']), a0=row(p['a0']), kk=row(p['k_k']), ka=row(p['k_a']), rk=row(p['r_k']),
                  gng=row(p['gn_g']), gnb=row(p['gn_b']), ones=ones),
        out=dict(wo_m=wo[:MLA_HEADS * V_DIM].astype(BF16),
                 wo_s=wo[MLA_HEADS * V_DIM:MLA_HEADS * V_DIM + S5_W].astype(BF16),
                 wo_r=wo[MLA_HEADS * V_DIM + S5_W:].astype(BF16), ln1g=row(p['ln1_g']), ln1b=row(p['ln1_b'])),
        mlp=dict(wup=p['w_up'].astype(BF16), wdown=p['w_down'].astype(BF16), ln2g=row(p['ln2_g']),
                 ln2b=row(p['ln2_b'])),
    )


def _rope_tables(pos):
    inv_freq = ROPE_BASE ** (-jnp.arange(0, ROPE_DIM, 2, dtype=F32) / ROPE_DIM)
    ang = pos.astype(F32)[:, None] * inv_freq[None, :]
    ang = jnp.concatenate([ang, ang], -1)
    cos, sin = jnp.cos(ang), jnp.sin(ang)
    T = pos.shape[0]
    z = lambda n: jnp.zeros((T, n), F32)
    return dict(
        csk=jnp.concatenate([cos, sin, z(V7X_LANES - 2 * ROPE_DIM)], axis=1),
        cq=jnp.concatenate([jnp.ones((T, NOPE_DIM), F32), cos, z(HEAD_PAD - NOPE_DIM - ROPE_DIM)], axis=1),
        sq=jnp.concatenate([z(NOPE_DIM), sin, z(HEAD_PAD - NOPE_DIM - ROPE_DIM)], axis=1),
    )


def _trunk_layer(x, tabs, past, s5_state, rwkv_state, shift0, w, alpha):
    B, T, D = x.shape
    qcat, ckv, krope, u_tm, p_tm = _inproj(x, w['inproj'], tabs)
    kcat, vt = _kvup(ckv, krope, w['kvup'])
    if past is None:
        mla = _attention(qcat, kcat, vt, kv_len=T, chunk_causal=True)
    else:
        kv_len = past[0].shape[1] + T
        bk = past[1].shape[4]
        kcat = jnp.concatenate([past[0], kcat], axis=1)
        kcat = jnp.pad(kcat, ((0, 0), (0, -kv_len % bk), (0, 0)))
        vt = jnp.pad(vt, ((0, 0), (0, 0), (0, 0), (0, 0), (0, bk - vt.shape[4])))
        vt = jnp.concatenate([past[1], vt], axis=2)
        mla = _attention(qcat, kcat, vt, kv_len=kv_len, chunk_causal=False)

    x0 = jnp.concatenate([s5_state[..., 0].reshape(B, S5_N), s5_state[..., 1].reshape(B, S5_N)], axis=1)
    s5_out, x_last = _s5(u_tm.reshape(T, B, S5_W), x0, w['s5'])
    new_s5 = jnp.stack([x_last[:, :S5_N].reshape(B, S5_GROUPS, S5_STATE),
                        x_last[:, S5_N:].reshape(B, S5_GROUPS, S5_STATE)], axis=-1)

    rw_out, new_rwkv, shift = _rwkv_mixer(p_tm.reshape(T, B, RWKV_PROJ), shift0.reshape(B, RWKV_PROJ), rwkv_state,
                                          w['rwkv'])
    x = _outproj(mla, s5_out.reshape(T, B * S5_W), rw_out.reshape(T, B * RWKV_W), x, w['out'], alpha)
    x = _mlp(x, w['mlp'], alpha)
    return x, ckv, krope, new_s5, new_rwkv, shift.reshape(B, 1, RWKV_PROJ)


def kernel(x_prompt, x_sample, cache_mla_ckv, cache_mla_krope, state_s5, state_rwkv, state_rwkv_shift, w_in, q_norm_g, w_qb, kv_norm_g, w_kvb, lam_re, lam_im, log_dt, b_re, b_im, c_re, c_im, s5_d, w_glu, b_glu, mu_shift, w0, w_w2, a0, w_a2, w_g2, k_k, k_a, r_k, gn_g, gn_b, w_out, ln1_g, ln1_b, w_up, w_down, ln2_g, ln2_b):
    depth = w_in.shape[0]
    alpha = (2 * depth) ** 0.25
    bp, sp = x_prompt.shape[:2]
    ts = x_sample.shape[1]
    past = cache_mla_ckv.shape[2]
    tabs_p = _rope_tables(jnp.arange(sp))
    tabs_s = _rope_tables(past + jnp.arange(ts))
    names = dict(w_in=w_in, q_norm_g=q_norm_g, w_qb=w_qb, kv_norm_g=kv_norm_g, w_kvb=w_kvb, lam_re=lam_re,
                 lam_im=lam_im, log_dt=log_dt, b_re=b_re, b_im=b_im, c_re=c_re, c_im=c_im, s5_d=s5_d, w_glu=w_glu,
                 b_glu=b_glu, mu_shift=mu_shift, w0=w0, w_w2=w_w2, a0=a0, w_a2=w_a2, w_g2=w_g2, k_k=k_k, k_a=k_a,
                 r_k=r_k, gn_g=gn_g, gn_b=gn_b, w_out=w_out, ln1_g=ln1_g, ln1_b=ln1_b, w_up=w_up, w_down=w_down,
                 ln2_g=ln2_g, ln2_b=ln2_b)
    s5_zero = jnp.zeros((bp, S5_GROUPS, S5_STATE, 2), F32)
    rwkv_zero = jnp.zeros((bp, RWKV_HEADS, RWKV_HEAD, RWKV_HEAD), F32)
    shift_zero = jnp.zeros((bp, 1, RWKV_PROJ), F32)

    xp, xs = x_prompt, x_sample
    outs_p = [[] for _ in range(5)]
    outs_s = [[] for _ in range(5)]
    for l in range(depth):
        w = _prep_layer({k: v[l] for k, v in names.items()})
        xp, *rest = _trunk_layer(xp, tabs_p, None, s5_zero, rwkv_zero, shift_zero, w, alpha)
        for acc, val in zip(outs_p, rest):
            acc.append(val)
        past_kv = _kvup(cache_mla_ckv[l], cache_mla_krope[l], w['kvup'])
        xs, *rest = _trunk_layer(xs, tabs_s, past_kv, state_s5[l], state_rwkv[l], state_rwkv_shift[l], w, alpha)
        for acc, val in zip(outs_s, rest):
            acc.append(val)
    return (xp, xs, *[jnp.stack(a) for a in outs_p], *[jnp.stack(a) for a in outs_s])
```

```python
import functools
import math

import jax
import jax.numpy as jnp
from jax import lax
from jax.experimental import pallas as pl
from jax.experimental.pallas import tpu as pltpu

F32 = jnp.float32
BF16 = jnp.bfloat16

CHUNK = 64
MLA_HEADS = 6
NOPE_DIM = 64
ROPE_DIM = 32
V_DIM = 64
Q_RANK = 256
KV_RANK = 128
ROPE_BASE = 10000.0
MLA_SCALE = (NOPE_DIM + ROPE_DIM) ** -0.5
S5_GROUP_CH = 16
S5_W = 256
S5_GROUPS = S5_W // S5_GROUP_CH
S5_STATE = 64
S5_N = S5_GROUPS * S5_STATE
RWKV_HEADS = 6
RWKV_HEAD = 64
RWKV_W = RWKV_HEADS * RWKV_HEAD
DECAY_LORA = 32
AAA_LORA = 32
GATE_LORA = 64
RWKV_PROJ = 3 * RWKV_W + DECAY_LORA + AAA_LORA + GATE_LORA
LN_EPS = 1e-5
RMS_EPS = 1e-6
GN_EPS = 64e-5
NEG_INF = -1e30

V7X_LANES = 128
V7X_VMEM_BYTES = 64 * 1024 * 1024
VMEM_LIMIT_BYTES = V7X_VMEM_BYTES - 8 * 1024 * 1024
HEAD_PAD = V7X_LANES
ATTN_BLOCK = 512
ATTN_HEAD_GROUP = 3
V_ROWS = V_DIM + 16
LOG2_E = 1.4426950408889634
RWKV_PACK = 8


def _cparams(*sem):
    return pltpu.CompilerParams(dimension_semantics=sem, vmem_limit_bytes=VMEM_LIMIT_BYTES)


def _const_spec(shape):
    zeros = (0,) * len(shape)
    return pl.BlockSpec(shape, lambda *_: zeros)


def _tile(n, target):
    if n <= target:
        return n
    t = target
    while n % t:
        t //= 2
    return t


def _layer_norm(y, g, b):
    mu = jnp.mean(y, -1, keepdims=True)
    yc = y - mu
    var = jnp.mean(yc * yc, -1, keepdims=True)
    return yc * lax.rsqrt(var + LN_EPS) * g + b


def _inproj_kernel(x_ref, wm_ref, wu_ref, wp_ref, wq1_ref, wq2_ref, qg_ref, kg_ref, csk_ref, cq_ref, sq_ref,
                   qcat_ref, ckv_ref, krope_ref, u_ref, p_ref):
    x = x_ref[0].astype(BF16)
    m = jnp.dot(x, wm_ref[...], preferred_element_type=F32)
    q_lat = m[:, :Q_RANK]
    qn = q_lat * lax.rsqrt(jnp.mean(q_lat * q_lat, -1, keepdims=True) + RMS_EPS) * qg_ref[...]
    kv_lat = m[:, Q_RANK:Q_RANK + KV_RANK]
    ckv_ref[0] = kv_lat * lax.rsqrt(jnp.mean(kv_lat * kv_lat, -1, keepdims=True) + RMS_EPS) * kg_ref[...]
    kr = m[:, Q_RANK + KV_RANK:] * csk_ref[...]
    kr = kr + pltpu.roll(kr, V7X_LANES - ROPE_DIM, 1)
    krope_ref[0] = kr[:, :ROPE_DIM]
    qb = qn.astype(BF16)
    q1 = jnp.dot(qb, wq1_ref[...], preferred_element_type=F32)
    q2 = jnp.dot(qb, wq2_ref[...], preferred_element_type=F32)
    cq = cq_ref[...]
    sq = sq_ref[...]
    for h in range(MLA_HEADS):
        sl = slice(h * HEAD_PAD, (h + 1) * HEAD_PAD)
        qcat_ref[0, :, sl] = (q1[:, sl] * cq + q2[:, sl] * sq).astype(BF16)
    u_ref[...] = jnp.dot(x, wu_ref[...], preferred_element_type=F32)
    p_ref[...] = jnp.dot(x, wp_ref[...], preferred_element_type=F32)


def _inproj(x, w, tabs):
    B, T, D = x.shape
    tt = _tile(T, 512)
    nq = MLA_HEADS * HEAD_PAD
    grid = (B, T // tt)
    row = lambda b, i: (b, i, 0)
    tab = lambda b, i: (i, 0)
    tm = lambda b, i: (i, b)
    return pl.pallas_call(
        _inproj_kernel,
        grid=grid,
        in_specs=[
            pl.BlockSpec((1, tt, D), row),
            _const_spec(w['wm'].shape), _const_spec(w['wu'].shape), _const_spec(w['wp'].shape),
            _const_spec(w['wq1'].shape), _const_spec(w['wq2'].shape),
            _const_spec((1, Q_RANK)), _const_spec((1, KV_RANK)),
            pl.BlockSpec((tt, V7X_LANES), tab), pl.BlockSpec((tt, HEAD_PAD), tab), pl.BlockSpec((tt, HEAD_PAD), tab),
        ],
        out_specs=[
            pl.BlockSpec((1, tt, nq), row),
            pl.BlockSpec((1, tt, KV_RANK), row),
            pl.BlockSpec((1, tt, ROPE_DIM), row),
            pl.BlockSpec((tt, S5_W), tm),
            pl.BlockSpec((tt, RWKV_PROJ), tm),
        ],
        out_shape=[
            jax.ShapeDtypeStruct((B, T, nq), BF16),
            jax.ShapeDtypeStruct((B, T, KV_RANK), F32),
            jax.ShapeDtypeStruct((B, T, ROPE_DIM), F32),
            jax.ShapeDtypeStruct((T, B * S5_W), F32),
            jax.ShapeDtypeStruct((T, B * RWKV_PROJ), F32),
        ],
        compiler_params=_cparams("parallel", "parallel"),
        name="inproj",
    )(x, w['wm'], w['wu'], w['wp'], w['wq1'], w['wq2'], w['qg'], w['kg'], tabs['csk'], tabs['cq'], tabs['sq'])


def _kvup_kernel(ckv_ref, kr_ref, wk_ref, pk_ref, wvt_ref, kcat_ref, vt_ref):
    c = ckv_ref[0].astype(BF16)
    kr = kr_ref[0].astype(BF16)
    kcat = jnp.dot(c, wk_ref[...], preferred_element_type=F32) + jnp.dot(kr, pk_ref[...], preferred_element_type=F32)
    kcat_ref[0] = kcat.astype(BF16)
    vt = lax.dot_general(wvt_ref[...], c, (((1,), (1,)), ((), ())), preferred_element_type=F32).astype(BF16)
    ones = jnp.ones((V_ROWS - V_DIM, vt.shape[1]), BF16)
    for h in range(MLA_HEADS):
        vt_ref[0, h, 0, :V_DIM, :] = vt[h * V_DIM:(h + 1) * V_DIM, :]
        vt_ref[0, h, 0, V_DIM:, :] = ones


def _kvup(ckv, krope, w):
    B, T, _ = ckv.shape
    tt = _tile(T, ATTN_BLOCK)
    nk = MLA_HEADS * HEAD_PAD
    row = lambda b, i: (b, i, 0)
    return pl.pallas_call(
        _kvup_kernel,
        grid=(B, T // tt),
        in_specs=[pl.BlockSpec((1, tt, KV_RANK), row), pl.BlockSpec((1, tt, ROPE_DIM), row),
                  _const_spec(w['wk'].shape), _const_spec(w['pk'].shape), _const_spec(w['wvt'].shape)],
        out_specs=[pl.BlockSpec((1, tt, nk), row),
                   pl.BlockSpec((1, MLA_HEADS, 1, V_ROWS, tt), lambda b, i: (b, 0, i, 0, 0))],
        out_shape=[jax.ShapeDtypeStruct((B, T, nk), BF16),
                   jax.ShapeDtypeStruct((B, MLA_HEADS, T // tt, V_ROWS, tt), BF16)],
        compiler_params=_cparams("parallel", "parallel"),
        name="kvup",
    )(ckv, krope, w['wk'], w['pk'], w['wvt'])


def _attn_kernel(q_ref, k_ref, v_ref, o_ref, sa_ref, sb_ref, m_ref, acc_ref, *, bq, bk, kv_len, chunk_causal):
    hg = ATTN_HEAD_GROUP
    qi = pl.program_id(2)
    last = qi if chunk_causal else jnp.int32(-(-kv_len // bk) - 1)
    nt = (((1,), (1,)), ((), ()))

    def scores(j, s_ref, h):
        r0 = pl.multiple_of(j * bk, bk)
        sl = slice(h * HEAD_PAD, (h + 1) * HEAD_PAD)
        s_ref[h] = lax.dot_general(k_ref[0, pl.ds(r0, bk), sl], q_ref[0, :, sl], nt,
                                   preferred_element_type=F32)

    def softmax_pv(j, s_ref, h, ok):
        s = s_ref[h]
        if ok is not None:
            s = jnp.where(ok, s, NEG_INF)
        m = m_ref[h]
        m_new = jnp.maximum(m, jnp.max(s, axis=0, keepdims=True))
        p = jnp.exp2(s - m_new)
        alpha = jnp.exp2(m - m_new)
        acc_ref[h] = alpha * acc_ref[h] + jnp.dot(v_ref[0, h, j], p.astype(BF16),
                                                  preferred_element_type=F32)
        m_ref[h] = m_new

    def mask(j):
        kidx = j * bk + lax.broadcasted_iota(jnp.int32, (bk, bq), 0)
        ok = kidx < kv_len
        if chunk_causal:
            qidx = qi * bq + lax.broadcasted_iota(jnp.int32, (bk, bq), 1)
            ok = jnp.logical_and(ok, (kidx // CHUNK) <= (qidx // CHUNK))
        return ok

    def advance(j_next, s_next, j, s_cur, ok=None):
        for h in range(hg):
            if j_next is not None:
                scores(j_next, s_next, h)
            softmax_pv(j, s_cur, h, ok)

    m_ref[...] = jnp.full(m_ref.shape, NEG_INF, F32)
    acc_ref[...] = jnp.zeros(acc_ref.shape, F32)
    for h in range(hg):
        scores(0, sa_ref, h)

    def pair(p, _):
        j = 2 * p
        advance(j + 1, sb_ref, j, sa_ref)
        advance(j + 2, sa_ref, j + 1, sb_ref)
        return 0

    lax.fori_loop(0, last // 2, pair, 0)

    @pl.when(last % 2 == 0)
    def _():
        advance(None, None, last, sa_ref, mask(last))

    @pl.when(last % 2 == 1)
    def _():
        advance(last, sb_ref, last - 1, sa_ref)
        advance(None, None, last, sb_ref, mask(last))

    for h in range(hg):
        o_ref[0, h] = (acc_ref[h, :V_DIM, :] / acc_ref[h, V_DIM:V_DIM + 1, :]).astype(BF16)


def _attention(qcat, kcat, vt, *, kv_len, chunk_causal):
    B, Tq, _ = qcat.shape
    Tk = kcat.shape[1]
    nblk, bk = vt.shape[2], vt.shape[4]
    assert nblk * bk == Tk
    bq = bk if chunk_causal else Tq
    hg = ATTN_HEAD_GROUP
    out_t = pl.pallas_call(
        functools.partial(_attn_kernel, bq=bq, bk=bk, kv_len=kv_len, chunk_causal=chunk_causal),
        grid=(B, MLA_HEADS // hg, Tq // bq),
        in_specs=[
            pl.BlockSpec((1, bq, hg * HEAD_PAD), lambda b, g, i: (b, i, g)),
            pl.BlockSpec((1, Tk, hg * HEAD_PAD), lambda b, g, i: (b, 0, g)),
            pl.BlockSpec((1, hg, nblk, V_ROWS, bk), lambda b, g, i: (b, g, 0, 0, 0)),
        ],
        out_specs=pl.BlockSpec((1, hg, V_DIM, bq), lambda b, g, i: (b, g, 0, i)),
        out_shape=jax.ShapeDtypeStruct((B, MLA_HEADS, V_DIM, Tq), BF16),
        scratch_shapes=[pltpu.VMEM((hg, bk, bq), F32), pltpu.VMEM((hg, bk, bq), F32),
                        pltpu.VMEM((hg, 1, bq), F32), pltpu.VMEM((hg, V_ROWS, bq), F32)],
        compiler_params=_cparams("parallel", "parallel", "arbitrary"),
        name="mla_attention",
    )(qcat, kcat, vt)
    return out_t


def _gelu_tanh(x):
    return 0.5 * x * (1.0 + jnp.tanh(math.sqrt(2.0 / math.pi) * (x + 0.044715 * (x * x * x))))


def _s5_kernel(u_ref, x0_ref, a_ref, wb_ref, wc_ref, d_ref, wg_ref, bg_ref, o_ref, xl_ref, xs_ref, st_ref, *, L, B):
    @pl.when(pl.program_id(0) == 0)
    def _():
        st_ref[...] = x0_ref[...]

    u = u_ref[...].reshape(L * B, S5_W)
    xs_ref[...] = jnp.dot(u.astype(BF16), wb_ref[...], preferred_element_type=F32)
    a_re = a_ref[0:1, :]
    a_im = a_ref[1:2, :]

    def body(t, carry):
        x_re, x_im = carry
        r0 = pl.multiple_of(t * B, B)
        n_re = a_re * x_re - a_im * x_im + xs_ref[pl.ds(r0, B), :S5_N]
        n_im = a_re * x_im + a_im * x_re + xs_ref[pl.ds(r0, B), S5_N:]
        xs_ref[pl.ds(r0, B), :S5_N] = n_re
        xs_ref[pl.ds(r0, B), S5_N:] = n_im
        return n_re, n_im

    x_re, x_im = lax.fori_loop(0, L, body, (st_ref[:, :S5_N], st_ref[:, S5_N:]), unroll=8)
    st_ref[:, :S5_N] = x_re
    st_ref[:, S5_N:] = x_im
    xl_ref[...] = st_ref[...]

    y = jnp.dot(xs_ref[...].astype(BF16), wc_ref[...], preferred_element_type=F32) + d_ref[...] * u
    z = _gelu_tanh(y)
    gate = jax.nn.sigmoid(jnp.dot(z.astype(BF16), wg_ref[...], preferred_element_type=F32) + bg_ref[...])
    o_ref[...] = (z * gate).reshape(L, B, S5_W)


def _s5(u_tm, x0, w):
    T, B, _ = u_tm.shape
    L = _tile(T, 1024 // B)
    return pl.pallas_call(
        functools.partial(_s5_kernel, L=L, B=B),
        grid=(T // L,),
        in_specs=[
            pl.BlockSpec((L, B, S5_W), lambda i: (i, 0, 0)),
            _const_spec((B, 2 * S5_N)), _const_spec((2, S5_N)),
            _const_spec((S5_W, 2 * S5_N)), _const_spec((2 * S5_N, S5_W)),
            _const_spec((1, S5_W)), _const_spec((S5_W, S5_W)), _const_spec((1, S5_W)),
        ],
        out_specs=[pl.BlockSpec((L, B, S5_W), lambda i: (i, 0, 0)), _const_spec((B, 2 * S5_N))],
        out_shape=[jax.ShapeDtypeStruct((T, B, S5_W), F32), jax.ShapeDtypeStruct((B, 2 * S5_N), F32)],
        scratch_shapes=[pltpu.VMEM((L * B, 2 * S5_N), F32), pltpu.VMEM((B, 2 * S5_N), F32)],
        compiler_params=_cparams("arbitrary"),
        name="s5_scan",
    )(u_tm, x0, w['a'], w['wb'], w['wc'], w['d'], w['wglu'], w['bglu'])


def _softplus(x):
    return jnp.maximum(x, 0.0) + jnp.log(1.0 + jnp.exp(-jnp.abs(x)))


def _head_sum(x, ones_ref):
    hi = x.astype(BF16)
    lo = (x - hi.astype(F32)).astype(BF16)
    ones = ones_ref[...]
    return jnp.dot(hi, ones, preferred_element_type=F32) + jnp.dot(lo, ones, preferred_element_type=F32)


def _rwkv_prep_kernel(p_ref, sh0_ref, mu_ref, ww_ref, wa_ref, wg_ref, w0_ref, a0_ref, kk_ref, ka_ref, rk_ref,
                      ones_ref, r_o, w_o, k_o, v_o, n_o, b_o, g_o, bon_o, sh_o, last_ref, *, L):
    @pl.when(pl.program_id(1) == 0)
    def _():
        last_ref[...] = sh0_ref[0]

    p = p_ref[...]
    row = lax.broadcasted_iota(jnp.int32, p.shape, 0)
    prev = jnp.where(row == 0, last_ref[...], pltpu.roll(p, 1, 0))
    last_ref[...] = p[L - 1:L]
    sh_o[0] = p[L - 1:L]
    ps = p + (prev - p) * mu_ref[...]
    r = ps[:, :RWKV_W]
    k = ps[:, RWKV_W:2 * RWKV_W]
    v = ps[:, 2 * RWKV_W:3 * RWKV_W]
    tail = ps[:, 3 * RWKV_W:]
    lw = jnp.dot(jnp.tanh(tail).astype(BF16), ww_ref[...], preferred_element_type=F32)
    la = jnp.dot(tail.astype(BF16), wa_ref[...], preferred_element_type=F32)
    g = jnp.dot(jax.nn.sigmoid(tail).astype(BF16), wg_ref[...], preferred_element_type=F32)
    w_log = -_softplus(-(w0_ref[...] + lw)) - 0.5
    decay = jnp.exp(-jnp.exp(w_log))
    a = jax.nn.sigmoid(a0_ref[...] + la)
    kk = k * kk_ref[...]
    nrm = jnp.maximum(jnp.sqrt(_head_sum(kk * kk, ones_ref)), 1e-12)
    kk = kk / nrm
    k2 = k * (1.0 + (a - 1.0) * ka_ref[...])
    bonus = _head_sum(r * k2 * rk_ref[...], ones_ref) * v
    r_o[...] = r
    w_o[...] = decay
    k_o[...] = k2
    v_o[...] = v
    n_o[...] = -kk
    b_o[...] = kk * a
    g_o[...] = g
    bon_o[...] = bonus


def _rwkv_prep(p2d, shift0, B, w):
    T = p2d.shape[0]
    L = _tile(T, 512)
    blk = pl.BlockSpec((L, RWKV_W), lambda b, i: (i, b))
    vec = _const_spec((1, RWKV_W))
    one = pl.BlockSpec((1, 1, RWKV_PROJ), lambda b, i: (b, 0, 0))
    outs = pl.pallas_call(
        functools.partial(_rwkv_prep_kernel, L=L),
        grid=(B, T // L),
        in_specs=[pl.BlockSpec((L, RWKV_PROJ), lambda b, i: (i, b)), one, _const_spec((1, RWKV_PROJ)),
                  _const_spec((V7X_LANES, RWKV_W)), _const_spec((V7X_LANES, RWKV_W)), _const_spec((V7X_LANES, RWKV_W)),
                  vec, vec, vec, vec, vec, _const_spec((RWKV_W, RWKV_W))],
        out_specs=[blk] * 8 + [one],
        out_shape=[jax.ShapeDtypeStruct((T, B * RWKV_W), F32)] * 8 + [jax.ShapeDtypeStruct((B, 1, RWKV_PROJ), F32)],
        scratch_shapes=[pltpu.VMEM((1, RWKV_PROJ), F32)],
        compiler_params=_cparams("parallel", "arbitrary"),
        name="rwkv_prep",
    )(p2d, shift0, w['mu'], w['ww'], w['wa'], w['wg'], w['w0'], w['a0'], w['kk'], w['ka'], w['rk'], w['ones'])
    return outs


def _lane_window(tile, off, width, dst, lane):
    v0, lo = divmod(off, V7X_LANES)
    shift = (dst - lo) % V7X_LANES
    a = tile(v0)
    r = pltpu.roll(a, shift, 1) if shift else a
    if lo + width > V7X_LANES:
        b = tile(v0 + 1)
        r2 = pltpu.roll(b, shift, 1) if shift else b
        r = jnp.where(lane < dst + (V7X_LANES - lo), r, r2)
    return r


def _chain_lanes(pieces, bh, lane):
    rep = len(pieces)
    out = jnp.where(lane < rep * bh, pieces[rep - 1], 0.0)
    for r in range(rep - 2, -1, -1):
        out = jnp.where(lane < (r + 1) * bh, pieces[r], out)
    return out


def _rwkv_scan_kernel(w_ref, b_ref, k_ref, r_ref, nn_ref, nnx_ref, v_ref, s0_ref, y_ref, so_ref,
                      s_ref, sa_ref, e_ref, nna_ref, *, Tm, IL, bh, rep):
    lane = lax.broadcasted_iota(jnp.int32, (RWKV_HEAD, V7X_LANES), 1)
    lane_il = lax.broadcasted_iota(jnp.int32, (IL, V7X_LANES), 1)
    ncol = RWKV_PACK * bh // V7X_LANES

    def tiles(slab, rows=slice(None)):
        return lambda c: slab[rows, c * V7X_LANES:(c + 1) * V7X_LANES]

    def expand(slab, t8):
        return _chain_lanes([_lane_window(tiles(slab), t8 * bh, bh, r * bh, lane) for r in range(rep)], bh, lane)

    nna_ref[0:Tm] = nn_ref[...]
    nna_ref[Tm] = nnx_ref[0]

    @pl.when(pl.program_id(0) == 0)
    def _():
        s_ref[...] = s0_ref[...]
        e_ref[0, 4] = expand(nn_ref[0], 0)
        acc = jnp.zeros((IL, V7X_LANES), F32)
        for j in range(RWKV_HEAD):
            acc = acc + s0_ref[j] * e_ref[0, 4, j:j + 1, :]
        sa_ref[...] = acc

    def body(m, sa):
        slabs = [w_ref[m], b_ref[m], k_ref[m], r_ref[m]]
        nn0 = nna_ref[m]
        nn1 = nna_ref[m + 1]
        vs = v_ref[m]
        for t8 in range(RWKV_PACK):
            for o, slab in enumerate(slabs):
                e_ref[t8, o] = expand(slab, t8)
            e_ref[t8, 4] = expand(nn0, t8 + 1) if t8 + 1 < RWKV_PACK else expand(nn1, 0)
        ycols = [[jnp.zeros((IL, V7X_LANES), F32) for _ in range(rep)] for _ in range(ncol)]
        for t8 in range(RWKV_PACK):
            off = t8 * bh
            vt = _chain_lanes([_lane_window(tiles(vs, slice(r * IL, (r + 1) * IL)), off, bh, r * bh, lane_il)
                               for r in range(rep)], bh, lane_il)
            yacc = jnp.zeros((IL, V7X_LANES), F32)
            san = jnp.zeros((IL, V7X_LANES), F32)
            for j in range(RWKV_HEAD):
                row = lambda o: e_ref[t8, o, j:j + 1, :]
                sn = s_ref[j] * row(0) + sa * row(1) + vt * row(2)
                s_ref[j] = sn
                yacc = yacc + sn * row(3)
                san = san + sn * row(4)
            sa = san
            for r in range(rep):
                shift = (off - r * bh) % V7X_LANES
                rolled = pltpu.roll(yacc, shift, 1) if shift else yacc
                for c in range(ncol):
                    lo, hi = max(off, c * V7X_LANES), min(off + bh, (c + 1) * V7X_LANES)
                    if lo < hi:
                        inside = jnp.logical_and(lane_il >= lo - c * V7X_LANES, lane_il < hi - c * V7X_LANES)
                        ycols[c][r] = jnp.where(inside, rolled, ycols[c][r])
        for c in range(ncol):
            for r in range(rep):
                y_ref[m, r * IL:(r + 1) * IL, c * V7X_LANES:(c + 1) * V7X_LANES] = ycols[c][r]
        return sa

    sa_ref[...] = lax.fori_loop(0, Tm, body, sa_ref[...])
    so_ref[...] = s_ref[...]


def _rwkv_scan(ops, v_p, s0_l, bh, rep):
    TP, _, W = v_p.shape
    IL = RWKV_HEAD // rep
    Tm = _tile(TP, 8)
    blk = pl.BlockSpec((Tm, RWKV_HEAD, W), lambda i: (i, 0, 0))
    nxt = pl.BlockSpec((1, RWKV_HEAD, W), lambda i: (jnp.minimum((i + 1) * Tm, TP - 1), 0, 0))
    sblk = _const_spec((RWKV_HEAD, IL, V7X_LANES))
    return pl.pallas_call(
        functools.partial(_rwkv_scan_kernel, Tm=Tm, IL=IL, bh=bh, rep=rep),
        grid=(TP // Tm,),
        in_specs=[blk] * 5 + [nxt, blk, sblk],
        out_specs=[blk, sblk],
        out_shape=[jax.ShapeDtypeStruct((TP, RWKV_HEAD, W), F32),
                   jax.ShapeDtypeStruct((RWKV_HEAD, IL, V7X_LANES), F32)],
        scratch_shapes=[pltpu.VMEM((RWKV_HEAD, IL, V7X_LANES), F32), pltpu.VMEM((IL, V7X_LANES), F32),
                        pltpu.VMEM((RWKV_PACK, 5, RWKV_HEAD, V7X_LANES), F32),
                        pltpu.VMEM((Tm + 1, RWKV_HEAD, W), F32)],
        compiler_params=_cparams("arbitrary"),
        name="rwkv_scan",
    )(*ops, ops[4], v_p, s0_l)


def _rwkv_layout(B):
    bh = B * RWKV_HEADS
    rep = 1
    while 2 * rep * bh <= V7X_LANES and RWKV_HEAD % (2 * rep) == 0:
        rep *= 2
    return bh, rep, RWKV_HEAD // rep


def _rwkv_mixer(p2d, shift0, s0, B, w):
    T = p2d.shape[0]
    bh, rep, IL = _rwkv_layout(B)
    lanes = rep * bh
    r, dec, k2, v, nkk, beta, g, bonus, shift = _rwkv_prep(p2d, shift0, B, w)
    TP = T // RWKV_PACK

    def pack(x):
        x = x.reshape(TP, RWKV_PACK, bh, RWKV_HEAD).transpose(0, 3, 1, 2)
        return x.reshape(TP, RWKV_HEAD, RWKV_PACK * bh)

    ops = [pack(dec), pack(beta), pack(k2), pack(r), pack(nkk)]
    s0_l = s0.reshape(bh, rep, IL, RWKV_HEAD).transpose(3, 2, 1, 0).reshape(RWKV_HEAD, IL, lanes)
    s0_l = jnp.pad(s0_l, ((0, 0), (0, 0), (0, V7X_LANES - lanes)))
    y_p, s_l = _rwkv_scan(ops, pack(v), s0_l, bh, rep)
    y2d = y_p.reshape(TP, RWKV_HEAD, RWKV_PACK, bh).transpose(0, 2, 3, 1).reshape(T, B * RWKV_W)
    s_last = s_l[:, :, :lanes].reshape(RWKV_HEAD, IL, rep, bh).transpose(3, 2, 1, 0)
    s_last = s_last.reshape(B, RWKV_HEADS, RWKV_HEAD, RWKV_HEAD)
    return (y2d, bonus, g), s_last, shift


def _outproj_kernel(m_ref, s_ref, y_ref, bon_ref, gate_ref, x_ref, wm_ref, ws_ref, wr_ref, gng_ref, gnb_ref,
                    ones_ref, g_ref, b_ref, o_ref, *, alpha):
    tt = m_ref.shape[-1]
    mla = m_ref[0].reshape(MLA_HEADS * V_DIM, tt).T
    acc = jnp.dot(mla, wm_ref[...], preferred_element_type=F32)
    acc = acc + jnp.dot(s_ref[...].astype(BF16), ws_ref[...], preferred_element_type=F32)
    y = y_ref[...]
    inv_n = 1.0 / RWKV_HEAD
    mu = _head_sum(y, ones_ref) * inv_n
    yc = y - mu
    var = _head_sum(yc * yc, ones_ref) * inv_n
    yn = yc * lax.rsqrt(var + GN_EPS) * gng_ref[...] + gnb_ref[...]
    rw = (yn + bon_ref[...]) * gate_ref[...]
    acc = acc + jnp.dot(rw.astype(BF16), wr_ref[...], preferred_element_type=F32)
    o_ref[0] = _layer_norm(alpha * x_ref[0] + acc, g_ref[...], b_ref[...])


def _outproj(mla_t, s5_2d, rwkv, x, w, wr, alpha):
    B, T, D = x.shape
    tt = _tile(T, 512)
    row = lambda b, i: (b, i, 0)
    tm = lambda b, i: (i, b)
    rblk = pl.BlockSpec((tt, RWKV_W), tm)
    vec = _const_spec((1, RWKV_W))
    return pl.pallas_call(
        functools.partial(_outproj_kernel, alpha=alpha),
        grid=(B, T // tt),
        in_specs=[pl.BlockSpec((1, MLA_HEADS, V_DIM, tt), lambda b, i: (b, 0, 0, i)), pl.BlockSpec((tt, S5_W), tm),
                  rblk, rblk, rblk, pl.BlockSpec((1, tt, D), row),
                  _const_spec(w['wo_m'].shape), _const_spec(w['wo_s'].shape), _const_spec(w['wo_r'].shape),
                  vec, vec, _const_spec((RWKV_W, RWKV_W)), _const_spec((1, D)), _const_spec((1, D))],
        out_specs=pl.BlockSpec((1, tt, D), row),
        out_shape=jax.ShapeDtypeStruct((B, T, D), F32),
        compiler_params=_cparams("parallel", "parallel"),
        name="outproj_ln",
    )(mla_t, s5_2d, *rwkv, x, w['wo_m'], w['wo_s'], w['wo_r'], wr['gng'], wr['gnb'], wr['ones'], w['ln1g'],
      w['ln1b'])


def _mlp_kernel(x_ref, wu_ref, wd_ref, g_ref, b_ref, o_ref, *, alpha, fc):
    x = x_ref[0]
    xb = x.astype(BF16)
    acc = alpha * x
    for c in range(wu_ref.shape[1] // fc):
        h = jnp.maximum(jnp.dot(xb, wu_ref[:, c * fc:(c + 1) * fc], preferred_element_type=F32), 0.0)
        acc = acc + jnp.dot((h * h).astype(BF16), wd_ref[c * fc:(c + 1) * fc, :], preferred_element_type=F32)
    o_ref[0] = _layer_norm(acc, g_ref[...], b_ref[...])


def _mlp(x, w, alpha):
    B, T, D = x.shape
    tt = _tile(T, 512)
    row = lambda b, i: (b, i, 0)
    return pl.pallas_call(
        functools.partial(_mlp_kernel, alpha=alpha, fc=1024),
        grid=(B, T // tt),
        in_specs=[pl.BlockSpec((1, tt, D), row), _const_spec(w['wup'].shape), _const_spec(w['wdown'].shape),
                  _const_spec((1, D)), _const_spec((1, D))],
        out_specs=pl.BlockSpec((1, tt, D), row),
        out_shape=jax.ShapeDtypeStruct((B, T, D), F32),
        compiler_params=_cparams("parallel", "parallel"),
        name="mlp_ln",
    )(x, w['wup'], w['wdown'], w['ln2g'], w['ln2b'])


def _rot_cols(wr):
    half = ROPE_DIM // 2
    return jnp.concatenate([-wr[..., half:], wr[..., :half]], axis=-1)


def _prep_layer(p):
    D = p['w_in'].shape[0]
    w_in = p['w_in']
    o_kv = Q_RANK
    o_kr = Q_RANK + KV_RANK
    o_u = o_kr + ROPE_DIM
    o_p = o_u + S5_W
    w_kr = w_in[:, o_kr:o_u]
    wm = jnp.concatenate([w_in[:, :o_kr], w_kr, _rot_cols(w_kr),
                          jnp.zeros((D, V7X_LANES - 2 * ROPE_DIM), F32)], axis=1)
    wqb = p['w_qb'].reshape(Q_RANK, MLA_HEADS, NOPE_DIM + ROPE_DIM) * (MLA_SCALE * LOG2_E)
    zq = jnp.zeros((Q_RANK, MLA_HEADS, HEAD_PAD - NOPE_DIM - ROPE_DIM), F32)
    wq1 = jnp.concatenate([wqb, zq], axis=-1).reshape(Q_RANK, MLA_HEADS * HEAD_PAD)
    wq2 = jnp.concatenate([jnp.zeros((Q_RANK, MLA_HEADS, NOPE_DIM), F32), _rot_cols(wqb[..., NOPE_DIM:]), zq],
                          axis=-1).reshape(Q_RANK, MLA_HEADS * HEAD_PAD)
    wkvb = p['w_kvb'].reshape(KV_RANK, MLA_HEADS, NOPE_DIM + V_DIM)
    wk = jnp.concatenate([wkvb[..., :NOPE_DIM], jnp.zeros((KV_RANK, MLA_HEADS, HEAD_PAD - NOPE_DIM), F32)],
                         axis=-1).reshape(KV_RANK, MLA_HEADS * HEAD_PAD)
    pk = jnp.concatenate([jnp.zeros((ROPE_DIM, NOPE_DIM), F32), jnp.eye(ROPE_DIM, dtype=F32),
                          jnp.zeros((ROPE_DIM, HEAD_PAD - NOPE_DIM - ROPE_DIM), F32)], axis=1)
    pk = jnp.tile(pk, (1, MLA_HEADS))
    wvt = wkvb[..., NOPE_DIM:].reshape(KV_RANK, MLA_HEADS * V_DIM).T

    lr, li = p['lam_re'], p['lam_im']
    dt = jnp.exp(p['log_dt'])[:, None]
    mag = jnp.exp(lr * dt)
    ar, ai = mag * jnp.cos(li * dt), mag * jnp.sin(li * dt)
    den = lr * lr + li * li
    cr = ((ar - 1.0) * lr + ai * li) / den
    ci = (ai * lr - (ar - 1.0) * li) / den
    bbr = cr[..., None] * p['b_re'] - ci[..., None] * p['b_im']
    bbi = cr[..., None] * p['b_im'] + ci[..., None] * p['b_re']
    eye_g = jnp.eye(S5_GROUPS, dtype=F32)
    bd_in = lambda m: (m.transpose(0, 2, 1)[:, :, None, :] * eye_g[:, None, :, None]).reshape(S5_W, S5_N)
    bd_out = lambda m: (m.transpose(0, 2, 1)[:, :, None, :] * eye_g[:, None, :, None]).reshape(S5_N, S5_W)
    wb = jnp.concatenate([bd_in(bbr), bd_in(bbi)], axis=1)
    wc = jnp.concatenate([bd_out(p['c_re']), -bd_out(p['c_im'])], axis=0)
    a = jnp.stack([ar.reshape(S5_N), ai.reshape(S5_N)])

    def lora_pad(w, off):
        z = jnp.zeros((V7X_LANES, RWKV_W), F32)
        return z.at[off:off + w.shape[0]].set(w)

    seg = jnp.arange(RWKV_W) // RWKV_HEAD
    ones = (seg[:, None] == seg[None, :]).astype(F32)
    row = lambda v: v.reshape(1, -1).astype(F32)
    wo = p['w_out']
    return dict(
        inproj=dict(wm=wm.astype(BF16), wu=w_in[:, o_u:o_p].astype(BF16), wp=w_in[:, o_p:].astype(BF16),
                    wq1=wq1.astype(BF16), wq2=wq2.astype(BF16), qg=row(p['q_norm_g']), kg=row(p['kv_norm_g'])),
        kvup=dict(wk=wk.astype(BF16), pk=pk.astype(BF16), wvt=wvt.astype(BF16)),
        s5=dict(a=a, wb=wb.astype(BF16), wc=wc.astype(BF16), d=row(p['s5_d']), wglu=p['w_glu'].astype(BF16),
                bglu=row(p['b_glu'])),
        rwkv=dict(mu=row(p['mu_shift']), ww=lora_pad(p['w_w2'], 0).astype(BF16),
                  wa=lora_pad(p['w_a2'], DECAY_LORA).astype(BF16),
                  wg=lora_pad(p['w_g2'], DECAY_LORA + AAA_LORA).astype(BF16),
                  w0=row(p['w0']), a0=row(p['a0']), kk=row(p['k_k']), ka=row(p['k_a']), rk=row(p['r_k']),
                  gng=row(p['gn_g']), gnb=row(p['gn_b']), ones=ones.astype(BF16)),
        out=dict(wo_m=wo[:MLA_HEADS * V_DIM].astype(BF16),
                 wo_s=wo[MLA_HEADS * V_DIM:MLA_HEADS * V_DIM + S5_W].astype(BF16),
                 wo_r=wo[MLA_HEADS * V_DIM + S5_W:].astype(BF16), ln1g=row(p['ln1_g']), ln1b=row(p['ln1_b'])),
        mlp=dict(wup=p['w_up'].astype(BF16), wdown=p['w_down'].astype(BF16), ln2g=row(p['ln2_g']),
                 ln2b=row(p['ln2_b'])),
    )


def _rope_tables(pos):
    inv_freq = ROPE_BASE ** (-jnp.arange(0, ROPE_DIM, 2, dtype=F32) / ROPE_DIM)
    ang = pos.astype(F32)[:, None] * inv_freq[None, :]
    ang = jnp.concatenate([ang, ang], -1)
    cos, sin = jnp.cos(ang), jnp.sin(ang)
    T = pos.shape[0]
    z = lambda n: jnp.zeros((T, n), F32)
    return dict(
        csk=jnp.concatenate([cos, sin, z(V7X_LANES - 2 * ROPE_DIM)], axis=1),
        cq=jnp.concatenate([jnp.ones((T, NOPE_DIM), F32), cos, z(HEAD_PAD - NOPE_DIM - ROPE_DIM)], axis=1),
        sq=jnp.concatenate([z(NOPE_DIM), sin, z(HEAD_PAD - NOPE_DIM - ROPE_DIM)], axis=1),
    )


def _trunk_layer(x, tabs, past, s5_state, rwkv_state, shift0, w, alpha):
    B, T, D = x.shape
    qcat, ckv, krope, u_tm, p_tm = _inproj(x, w['inproj'], tabs)
    kcat, vt = _kvup(ckv, krope, w['kvup'])
    if past is None:
        mla = _attention(qcat, kcat, vt, kv_len=T, chunk_causal=True)
    else:
        kv_len = past[0].shape[1] + T
        bk = past[1].shape[4]
        kcat = jnp.concatenate([past[0], kcat], axis=1)
        kcat = jnp.pad(kcat, ((0, 0), (0, -kv_len % bk), (0, 0)))
        vt = jnp.pad(vt, ((0, 0), (0, 0), (0, 0), (0, 0), (0, bk - vt.shape[4])))
        vt = jnp.concatenate([past[1], vt], axis=2)
        mla = _attention(qcat, kcat, vt, kv_len=kv_len, chunk_causal=False)

    x0 = jnp.concatenate([s5_state[..., 0].reshape(B, S5_N), s5_state[..., 1].reshape(B, S5_N)], axis=1)
    s5_out, x_last = _s5(u_tm.reshape(T, B, S5_W), x0, w['s5'])
    new_s5 = jnp.stack([x_last[:, :S5_N].reshape(B, S5_GROUPS, S5_STATE),
                        x_last[:, S5_N:].reshape(B, S5_GROUPS, S5_STATE)], axis=-1)

    rwkv, new_rwkv, shift = _rwkv_mixer(p_tm, shift0, rwkv_state, B, w['rwkv'])
    x = _outproj(mla, s5_out.reshape(T, B * S5_W), rwkv, x, w['out'], w['rwkv'], alpha)
    x = _mlp(x, w['mlp'], alpha)
    return x, ckv, krope, new_s5, new_rwkv, shift


def kernel(x_prompt, x_sample, cache_mla_ckv, cache_mla_krope, state_s5, state_rwkv, state_rwkv_shift, w_in, q_norm_g, w_qb, kv_norm_g, w_kvb, lam_re, lam_im, log_dt, b_re, b_im, c_re, c_im, s5_d, w_glu, b_glu, mu_shift, w0, w_w2, a0, w_a2, w_g2, k_k, k_a, r_k, gn_g, gn_b, w_out, ln1_g, ln1_b, w_up, w_down, ln2_g, ln2_b):
    depth = w_in.shape[0]
    alpha = (2 * depth) ** 0.25
    bp, sp = x_prompt.shape[:2]
    ts = x_sample.shape[1]
    past = cache_mla_ckv.shape[2]
    tabs_p = _rope_tables(jnp.arange(sp))
    tabs_s = _rope_tables(past + jnp.arange(ts))
    names = dict(w_in=w_in, q_norm_g=q_norm_g, w_qb=w_qb, kv_norm_g=kv_norm_g, w_kvb=w_kvb, lam_re=lam_re,
                 lam_im=lam_im, log_dt=log_dt, b_re=b_re, b_im=b_im, c_re=c_re, c_im=c_im, s5_d=s5_d, w_glu=w_glu,
                 b_glu=b_glu, mu_shift=mu_shift, w0=w0, w_w2=w_w2, a0=a0, w_a2=w_a2, w_g2=w_g2, k_k=k_k, k_a=k_a,
                 r_k=r_k, gn_g=gn_g, gn_b=gn_b, w_out=w_out, ln1_g=ln1_g, ln1_b=ln1_b, w_up=w_up, w_down=w_down,
                 ln2_g=ln2_g, ln2_b=ln2_b)
    s5_zero = jnp.zeros((bp, S5_GROUPS, S5_STATE, 2), F32)
    rwkv_zero = jnp.zeros((bp, RWKV_HEADS, RWKV_HEAD, RWKV_HEAD), F32)
    shift_zero = jnp.zeros((bp, 1, RWKV_PROJ), F32)

    xp, xs = x_prompt, x_sample
    outs_p = [[] for _ in range(5)]
    outs_s = [[] for _ in range(5)]
    for l in range(depth):
        w = _prep_layer({k: v[l] for k, v in names.items()})
        xp, *rest = _trunk_layer(xp, tabs_p, None, s5_zero, rwkv_zero, shift_zero, w, alpha)
        for acc, val in zip(outs_p, rest):
            acc.append(val)
        past_kv = _kvup(cache_mla_ckv[l], cache_mla_krope[l], w['kvup'])
        xs, *rest = _trunk_layer(xs, tabs_s, past_kv, state_s5[l], state_rwkv[l], state_rwkv_shift[l], w, alpha)
        for acc, val in zip(outs_s, rest):
            acc.append(val)
    return (xp, xs, *[jnp.stack(a) for a in outs_p], *[jnp.stack(a) for a in outs_s])
```

```python
import functools
import math

import jax
import jax.numpy as jnp
from jax import lax
from jax.experimental import pallas as pl
from jax.experimental.pallas import tpu as pltpu

F32 = jnp.float32
BF16 = jnp.bfloat16

CHUNK = 64
MLA_HEADS = 6
NOPE_DIM = 64
ROPE_DIM = 32
V_DIM = 64
Q_RANK = 256
KV_RANK = 128
ROPE_BASE = 10000.0
MLA_SCALE = (NOPE_DIM + ROPE_DIM) ** -0.5
S5_GROUP_CH = 16
S5_W = 256
S5_GROUPS = S5_W // S5_GROUP_CH
S5_STATE = 64
S5_N = S5_GROUPS * S5_STATE
RWKV_HEADS = 6
RWKV_HEAD = 64
RWKV_W = RWKV_HEADS * RWKV_HEAD
DECAY_LORA = 32
AAA_LORA = 32
GATE_LORA = 64
RWKV_PROJ = 3 * RWKV_W + DECAY_LORA + AAA_LORA + GATE_LORA
LN_EPS = 1e-5
RMS_EPS = 1e-6
GN_EPS = 64e-5
NEG_INF = -1e30

V7X_LANES = 128
V7X_VMEM_BYTES = 64 * 1024 * 1024
VMEM_LIMIT_BYTES = V7X_VMEM_BYTES - 8 * 1024 * 1024
HEAD_PAD = V7X_LANES
ATTN_BLOCK = 512
ROW_TILE = 128
ATTN_HEAD_GROUP = 3
V_ROWS = V_DIM + 16
LOG2_E = 1.4426950408889634
RWKV_PACK = 8


def _cparams(*sem):
    return pltpu.CompilerParams(dimension_semantics=sem, vmem_limit_bytes=VMEM_LIMIT_BYTES)


def _const_spec(shape):
    zeros = (0,) * len(shape)
    return pl.BlockSpec(shape, lambda *_: zeros)


def _tile(n, target):
    if n <= target:
        return n
    t = target
    while n % t:
        t //= 2
    return t


def _layer_norm(y, g, b):
    mu = jnp.mean(y, -1, keepdims=True)
    yc = y - mu
    var = jnp.mean(yc * yc, -1, keepdims=True)
    return yc * lax.rsqrt(var + LN_EPS) * g + b


def _inproj_kernel(x_ref, wm_ref, wu_ref, wp_ref, wq1_ref, wq2_ref, qg_ref, kg_ref, csk_ref, cq_ref, sq_ref,
                   qcat_ref, ckv_ref, krope_ref, u_ref, p_ref):
    B, tt, D = x_ref.shape
    x = x_ref[...].reshape(B * tt, D).astype(BF16)
    per_b = lambda tab_ref: jnp.concatenate([tab_ref[...]] * B, axis=0)
    m = jnp.dot(x, wm_ref[...], preferred_element_type=F32)
    q_lat = m[:, :Q_RANK]
    qn = q_lat * lax.rsqrt(jnp.mean(q_lat * q_lat, -1, keepdims=True) + RMS_EPS) * qg_ref[...]
    kv_lat = m[:, Q_RANK:Q_RANK + KV_RANK]
    ckv = kv_lat * lax.rsqrt(jnp.mean(kv_lat * kv_lat, -1, keepdims=True) + RMS_EPS) * kg_ref[...]
    ckv_ref[...] = ckv.reshape(B, tt, KV_RANK)
    kr = m[:, Q_RANK + KV_RANK:] * per_b(csk_ref)
    kr = kr + pltpu.roll(kr, V7X_LANES - ROPE_DIM, 1)
    krope_ref[...] = kr[:, :ROPE_DIM].reshape(B, tt, ROPE_DIM)
    qb = qn.astype(BF16)
    q1 = jnp.dot(qb, wq1_ref[...], preferred_element_type=F32)
    q2 = jnp.dot(qb, wq2_ref[...], preferred_element_type=F32)
    cq = per_b(cq_ref)
    sq = per_b(sq_ref)
    for h in range(MLA_HEADS):
        sl = slice(h * HEAD_PAD, (h + 1) * HEAD_PAD)
        qcat_ref[:, :, sl] = (q1[:, sl] * cq + q2[:, sl] * sq).astype(BF16).reshape(B, tt, HEAD_PAD)
    u = jnp.dot(x, wu_ref[...], preferred_element_type=F32)
    p = jnp.dot(x, wp_ref[...], preferred_element_type=F32)
    for b in range(B):
        u_ref[:, b, :] = u[b * tt:(b + 1) * tt]
        p_ref[:, b, :] = p[b * tt:(b + 1) * tt]


def _inproj(x, w, tabs):
    B, T, D = x.shape
    tt = _tile(T, ROW_TILE)
    nq = MLA_HEADS * HEAD_PAD
    row = lambda i: (0, i, 0)
    tab = lambda i: (i, 0)
    tm = lambda i: (i, 0, 0)
    return pl.pallas_call(
        _inproj_kernel,
        grid=(T // tt,),
        in_specs=[
            pl.BlockSpec((B, tt, D), row),
            _const_spec(w['wm'].shape), _const_spec(w['wu'].shape), _const_spec(w['wp'].shape),
            _const_spec(w['wq1'].shape), _const_spec(w['wq2'].shape),
            _const_spec((1, Q_RANK)), _const_spec((1, KV_RANK)),
            pl.BlockSpec((tt, V7X_LANES), tab), pl.BlockSpec((tt, HEAD_PAD), tab), pl.BlockSpec((tt, HEAD_PAD), tab),
        ],
        out_specs=[
            pl.BlockSpec((B, tt, nq), row),
            pl.BlockSpec((B, tt, KV_RANK), row),
            pl.BlockSpec((B, tt, ROPE_DIM), row),
            pl.BlockSpec((tt, B, S5_W), tm),
            pl.BlockSpec((tt, B, RWKV_PROJ), tm),
        ],
        out_shape=[
            jax.ShapeDtypeStruct((B, T, nq), BF16),
            jax.ShapeDtypeStruct((B, T, KV_RANK), F32),
            jax.ShapeDtypeStruct((B, T, ROPE_DIM), F32),
            jax.ShapeDtypeStruct((T, B, S5_W), F32),
            jax.ShapeDtypeStruct((T, B, RWKV_PROJ), F32),
        ],
        compiler_params=_cparams("parallel"),
        name="inproj",
    )(x, w['wm'], w['wu'], w['wp'], w['wq1'], w['wq2'], w['qg'], w['kg'], tabs['csk'], tabs['cq'], tabs['sq'])


def _kvup_kernel(ckv_ref, kr_ref, wk_ref, pk_ref, wvt_ref, kcat_ref, vt_ref):
    c = ckv_ref[0].astype(BF16)
    kr = kr_ref[0].astype(BF16)
    kcat = jnp.dot(c, wk_ref[...], preferred_element_type=F32) + jnp.dot(kr, pk_ref[...], preferred_element_type=F32)
    kcat_ref[0] = kcat.astype(BF16)
    vt = lax.dot_general(wvt_ref[...], c, (((1,), (1,)), ((), ())), preferred_element_type=F32).astype(BF16)
    ones = jnp.ones((V_ROWS - V_DIM, vt.shape[1]), BF16)
    for h in range(MLA_HEADS):
        vt_ref[0, h, 0, :V_DIM, :] = vt[h * V_DIM:(h + 1) * V_DIM, :]
        vt_ref[0, h, 0, V_DIM:, :] = ones


def _kvup(ckv, krope, w):
    B, T, _ = ckv.shape
    tt = _tile(T, ATTN_BLOCK)
    nk = MLA_HEADS * HEAD_PAD
    row = lambda b, i: (b, i, 0)
    return pl.pallas_call(
        _kvup_kernel,
        grid=(B, T // tt),
        in_specs=[pl.BlockSpec((1, tt, KV_RANK), row), pl.BlockSpec((1, tt, ROPE_DIM), row),
                  _const_spec(w['wk'].shape), _const_spec(w['pk'].shape), _const_spec(w['wvt'].shape)],
        out_specs=[pl.BlockSpec((1, tt, nk), row),
                   pl.BlockSpec((1, MLA_HEADS, 1, V_ROWS, tt), lambda b, i: (b, 0, i, 0, 0))],
        out_shape=[jax.ShapeDtypeStruct((B, T, nk), BF16),
                   jax.ShapeDtypeStruct((B, MLA_HEADS, T // tt, V_ROWS, tt), BF16)],
        compiler_params=_cparams("parallel", "parallel"),
        name="kvup",
    )(ckv, krope, w['wk'], w['pk'], w['wvt'])


def _attn_kernel(q_ref, k_ref, v_ref, o_ref, sa_ref, sb_ref, m_ref, acc_ref, *, bq, bk, kv_len, chunk_causal):
    hg = ATTN_HEAD_GROUP
    qi = pl.program_id(2)
    last = qi if chunk_causal else jnp.int32(-(-kv_len // bk) - 1)
    nt = (((1,), (1,)), ((), ()))

    def scores(j, s_ref, h):
        r0 = pl.multiple_of(j * bk, bk)
        sl = slice(h * HEAD_PAD, (h + 1) * HEAD_PAD)
        s_ref[h] = lax.dot_general(k_ref[0, pl.ds(r0, bk), sl], q_ref[0, :, sl], nt,
                                   preferred_element_type=F32)

    def softmax_pv(j, s_ref, h, ok):
        s = s_ref[h]
        if ok is not None:
            s = jnp.where(ok, s, NEG_INF)
        m = m_ref[h]
        m_new = jnp.maximum(m, jnp.max(s, axis=0, keepdims=True))
        p = jnp.exp2(s - m_new)
        alpha = jnp.exp2(m - m_new)
        acc_ref[h] = alpha * acc_ref[h] + jnp.dot(v_ref[0, h, j], p.astype(BF16),
                                                  preferred_element_type=F32)
        m_ref[h] = m_new

    def mask(j):
        kidx = j * bk + lax.broadcasted_iota(jnp.int32, (bk, bq), 0)
        ok = kidx < kv_len
        if chunk_causal:
            qidx = qi * bq + lax.broadcasted_iota(jnp.int32, (bk, bq), 1)
            ok = jnp.logical_and(ok, (kidx // CHUNK) <= (qidx // CHUNK))
        return ok

    def advance(j_next, s_next, j, s_cur, ok=None):
        for h in range(hg):
            if j_next is not None:
                scores(j_next, s_next, h)
            softmax_pv(j, s_cur, h, ok)

    m_ref[...] = jnp.full(m_ref.shape, NEG_INF, F32)
    acc_ref[...] = jnp.zeros(acc_ref.shape, F32)
    for h in range(hg):
        scores(0, sa_ref, h)

    def pair(p, _):
        j = 2 * p
        advance(j + 1, sb_ref, j, sa_ref)
        advance(j + 2, sa_ref, j + 1, sb_ref)
        return 0

    lax.fori_loop(0, last // 2, pair, 0)

    @pl.when(last % 2 == 0)
    def _():
        advance(None, None, last, sa_ref, mask(last))

    @pl.when(last % 2 == 1)
    def _():
        advance(last, sb_ref, last - 1, sa_ref)
        advance(None, None, last, sb_ref, mask(last))

    for h in range(hg):
        o_ref[0, h] = (acc_ref[h, :V_DIM, :] / acc_ref[h, V_DIM:V_DIM + 1, :]).astype(BF16)


def _attention(qcat, kcat, vt, *, kv_len, chunk_causal):
    B, Tq, _ = qcat.shape
    Tk = kcat.shape[1]
    nblk, bk = vt.shape[2], vt.shape[4]
    assert nblk * bk == Tk
    bq = bk if chunk_causal else Tq
    hg = ATTN_HEAD_GROUP
    out_t = pl.pallas_call(
        functools.partial(_attn_kernel, bq=bq, bk=bk, kv_len=kv_len, chunk_causal=chunk_causal),
        grid=(B, MLA_HEADS // hg, Tq // bq),
        in_specs=[
            pl.BlockSpec((1, bq, hg * HEAD_PAD), lambda b, g, i: (b, i, g)),
            pl.BlockSpec((1, Tk, hg * HEAD_PAD), lambda b, g, i: (b, 0, g)),
            pl.BlockSpec((1, hg, nblk, V_ROWS, bk), lambda b, g, i: (b, g, 0, 0, 0)),
        ],
        out_specs=pl.BlockSpec((1, hg, V_DIM, bq), lambda b, g, i: (b, g, 0, i)),
        out_shape=jax.ShapeDtypeStruct((B, MLA_HEADS, V_DIM, Tq), BF16),
        scratch_shapes=[pltpu.VMEM((hg, bk, bq), F32), pltpu.VMEM((hg, bk, bq), F32),
                        pltpu.VMEM((hg, 1, bq), F32), pltpu.VMEM((hg, V_ROWS, bq), F32)],
        compiler_params=_cparams("parallel", "parallel", "arbitrary"),
        name="mla_attention",
    )(qcat, kcat, vt)
    return out_t


def _gelu_tanh(x):
    return 0.5 * x * (1.0 + jnp.tanh(math.sqrt(2.0 / math.pi) * (x + 0.044715 * (x * x * x))))


def _s5_kernel(u_ref, x0_ref, a_ref, wb_ref, wc_ref, d_ref, wg_ref, bg_ref, o_ref, xl_ref, xs_ref, st_ref, *, L, B):
    @pl.when(pl.program_id(0) == 0)
    def _():
        st_ref[...] = x0_ref[...]

    u = u_ref[...].reshape(L * B, S5_W)
    xs_ref[...] = jnp.dot(u.astype(BF16), wb_ref[...], preferred_element_type=F32)
    a_re = a_ref[0:1, :]
    a_im = a_ref[1:2, :]

    def body(t, carry):
        x_re, x_im = carry
        r0 = pl.multiple_of(t * B, B)
        n_re = a_re * x_re - a_im * x_im + xs_ref[pl.ds(r0, B), :S5_N]
        n_im = a_re * x_im + a_im * x_re + xs_ref[pl.ds(r0, B), S5_N:]
        xs_ref[pl.ds(r0, B), :S5_N] = n_re
        xs_ref[pl.ds(r0, B), S5_N:] = n_im
        return n_re, n_im

    x_re, x_im = lax.fori_loop(0, L, body, (st_ref[:, :S5_N], st_ref[:, S5_N:]), unroll=8)
    st_ref[:, :S5_N] = x_re
    st_ref[:, S5_N:] = x_im
    xl_ref[...] = st_ref[...]

    y = jnp.dot(xs_ref[...].astype(BF16), wc_ref[...], preferred_element_type=F32) + d_ref[...] * u
    z = _gelu_tanh(y)
    gate = jax.nn.sigmoid(jnp.dot(z.astype(BF16), wg_ref[...], preferred_element_type=F32) + bg_ref[...])
    o_ref[...] = (z * gate).reshape(L, B, S5_W)


def _s5(u_tm, x0, w):
    T, B, _ = u_tm.shape
    L = _tile(T, 1024 // B)
    return pl.pallas_call(
        functools.partial(_s5_kernel, L=L, B=B),
        grid=(T // L,),
        in_specs=[
            pl.BlockSpec((L, B, S5_W), lambda i: (i, 0, 0)),
            _const_spec((B, 2 * S5_N)), _const_spec((2, S5_N)),
            _const_spec((S5_W, 2 * S5_N)), _const_spec((2 * S5_N, S5_W)),
            _const_spec((1, S5_W)), _const_spec((S5_W, S5_W)), _const_spec((1, S5_W)),
        ],
        out_specs=[pl.BlockSpec((L, B, S5_W), lambda i: (i, 0, 0)), _const_spec((B, 2 * S5_N))],
        out_shape=[jax.ShapeDtypeStruct((T, B, S5_W), F32), jax.ShapeDtypeStruct((B, 2 * S5_N), F32)],
        scratch_shapes=[pltpu.VMEM((L * B, 2 * S5_N), F32), pltpu.VMEM((B, 2 * S5_N), F32)],
        compiler_params=_cparams("arbitrary"),
        name="s5_scan",
    )(u_tm, x0, w['a'], w['wb'], w['wc'], w['d'], w['wglu'], w['bglu'])


def _softplus(x):
    return jnp.maximum(x, 0.0) + jnp.log(1.0 + jnp.exp(-jnp.abs(x)))


def _head_sum(x, ones_ref):
    hi = x.astype(BF16)
    lo = (x - hi.astype(F32)).astype(BF16)
    ones = ones_ref[...]
    return jnp.dot(hi, ones, preferred_element_type=F32) + jnp.dot(lo, ones, preferred_element_type=F32)


def _rwkv_prep_kernel(p_ref, sh0_ref, mu_ref, ww_ref, wa_ref, wg_ref, w0_ref, a0_ref, kk_ref, ka_ref, rk_ref,
                      ones_ref, r_o, w_o, k_o, v_o, n_o, b_o, g_o, bon_o, sh_o, last_ref, *, L, B):
    @pl.when(pl.program_id(0) == 0)
    def _():
        last_ref[...] = sh0_ref[...]

    p = p_ref[...].reshape(L * B, RWKV_PROJ)
    if L > 1:
        prev = jnp.concatenate([last_ref[...], p[:(L - 1) * B]], axis=0)
    else:
        prev = last_ref[...]
    last_ref[...] = p[(L - 1) * B:]
    sh_o[...] = p[(L - 1) * B:]
    ps = p + (prev - p) * mu_ref[...]
    r = ps[:, :RWKV_W]
    k = ps[:, RWKV_W:2 * RWKV_W]
    v = ps[:, 2 * RWKV_W:3 * RWKV_W]
    tail = ps[:, 3 * RWKV_W:]
    lw = jnp.dot(jnp.tanh(tail).astype(BF16), ww_ref[...], preferred_element_type=F32)
    la = jnp.dot(tail.astype(BF16), wa_ref[...], preferred_element_type=F32)
    g = jnp.dot(jax.nn.sigmoid(tail).astype(BF16), wg_ref[...], preferred_element_type=F32)
    w_log = -_softplus(-(w0_ref[...] + lw)) - 0.5
    decay = jnp.exp(-jnp.exp(w_log))
    a = jax.nn.sigmoid(a0_ref[...] + la)
    kk = k * kk_ref[...]
    nrm = jnp.maximum(jnp.sqrt(_head_sum(kk * kk, ones_ref)), 1e-12)
    kk = kk / nrm
    k2 = k * (1.0 + (a - 1.0) * ka_ref[...])
    bonus = _head_sum(r * k2 * rk_ref[...], ones_ref) * v
    shp = (L, B, RWKV_W)
    r_o[...] = r.reshape(shp)
    w_o[...] = decay.reshape(shp)
    k_o[...] = k2.reshape(shp)
    v_o[...] = v.reshape(shp)
    n_o[...] = (-kk).reshape(shp)
    b_o[...] = (kk * a).reshape(shp)
    g_o[...] = g.reshape(shp)
    bon_o[...] = bonus.reshape(shp)


def _rwkv_prep(p_tm, shift0, w):
    T, B, _ = p_tm.shape
    L = _tile(T, 512 // B)
    blk = pl.BlockSpec((L, B, RWKV_W), lambda i: (i, 0, 0))
    vec = _const_spec((1, RWKV_W))
    outs = pl.pallas_call(
        functools.partial(_rwkv_prep_kernel, L=L, B=B),
        grid=(T // L,),
        in_specs=[pl.BlockSpec((L, B, RWKV_PROJ), lambda i: (i, 0, 0)), _const_spec((B, RWKV_PROJ)),
                  _const_spec((1, RWKV_PROJ)),
                  _const_spec((V7X_LANES, RWKV_W)), _const_spec((V7X_LANES, RWKV_W)), _const_spec((V7X_LANES, RWKV_W)),
                  vec, vec, vec, vec, vec, _const_spec((RWKV_W, RWKV_W))],
        out_specs=[blk] * 8 + [_const_spec((B, RWKV_PROJ))],
        out_shape=[jax.ShapeDtypeStruct((T, B, RWKV_W), F32)] * 8 + [jax.ShapeDtypeStruct((B, RWKV_PROJ), F32)],
        scratch_shapes=[pltpu.VMEM((B, RWKV_PROJ), F32)],
        compiler_params=_cparams("arbitrary"),
        name="rwkv_prep",
    )(p_tm, shift0, w['mu'], w['ww'], w['wa'], w['wg'], w['w0'], w['a0'], w['kk'], w['ka'], w['rk'], w['ones'])
    return outs


def _lane_window(tile, off, width, dst, lane):
    v0, lo = divmod(off, V7X_LANES)
    shift = (dst - lo) % V7X_LANES
    a = tile(v0)
    r = pltpu.roll(a, shift, 1) if shift else a
    if lo + width > V7X_LANES:
        b = tile(v0 + 1)
        r2 = pltpu.roll(b, shift, 1) if shift else b
        r = jnp.where(lane < dst + (V7X_LANES - lo), r, r2)
    return r


def _chain_lanes(pieces, bh, lane):
    rep = len(pieces)
    out = jnp.where(lane < rep * bh, pieces[rep - 1], 0.0)
    for r in range(rep - 2, -1, -1):
        out = jnp.where(lane < (r + 1) * bh, pieces[r], out)
    return out


def _rwkv_scan_kernel(w_ref, b_ref, k_ref, r_ref, nn_ref, nnx_ref, v_ref, s0_ref, y_ref, so_ref,
                      s_ref, sa_ref, e_ref, nna_ref, *, Tm, IL, bh, rep):
    lane = lax.broadcasted_iota(jnp.int32, (RWKV_HEAD, V7X_LANES), 1)
    lane_il = lax.broadcasted_iota(jnp.int32, (IL, V7X_LANES), 1)
    ncol = RWKV_PACK * bh // V7X_LANES

    def tiles(slab, rows=slice(None)):
        return lambda c: slab[rows, c * V7X_LANES:(c + 1) * V7X_LANES]

    def expand(slab, t8):
        return _chain_lanes([_lane_window(tiles(slab), t8 * bh, bh, r * bh, lane) for r in range(rep)], bh, lane)

    nna_ref[0:Tm] = nn_ref[...]
    nna_ref[Tm] = nnx_ref[0]

    @pl.when(pl.program_id(0) == 0)
    def _():
        s_ref[...] = s0_ref[...]
        e_ref[0, 4] = expand(nn_ref[0], 0)
        acc = jnp.zeros((IL, V7X_LANES), F32)
        for j in range(RWKV_HEAD):
            acc = acc + s0_ref[j] * e_ref[0, 4, j:j + 1, :]
        sa_ref[...] = acc

    def body(m, sa):
        slabs = [w_ref[m], b_ref[m], k_ref[m], r_ref[m]]
        nn0 = nna_ref[m]
        nn1 = nna_ref[m + 1]
        vs = v_ref[m]
        for t8 in range(RWKV_PACK):
            for o, slab in enumerate(slabs):
                e_ref[t8, o] = expand(slab, t8)
            e_ref[t8, 4] = expand(nn0, t8 + 1) if t8 + 1 < RWKV_PACK else expand(nn1, 0)
        ycols = [[jnp.zeros((IL, V7X_LANES), F32) for _ in range(rep)] for _ in range(ncol)]
        for t8 in range(RWKV_PACK):
            off = t8 * bh
            vt = _chain_lanes([_lane_window(tiles(vs, slice(r * IL, (r + 1) * IL)), off, bh, r * bh, lane_il)
                               for r in range(rep)], bh, lane_il)
            yacc = jnp.zeros((IL, V7X_LANES), F32)
            san = jnp.zeros((IL, V7X_LANES), F32)
            for j in range(RWKV_HEAD):
                row = lambda o: e_ref[t8, o, j:j + 1, :]
                sn = s_ref[j] * row(0) + sa * row(1) + vt * row(2)
                s_ref[j] = sn
                yacc = yacc + sn * row(3)
                san = san + sn * row(4)
            sa = san
            for r in range(rep):
                shift = (off - r * bh) % V7X_LANES
                rolled = pltpu.roll(yacc, shift, 1) if shift else yacc
                for c in range(ncol):
                    lo, hi = max(off, c * V7X_LANES), min(off + bh, (c + 1) * V7X_LANES)
                    if lo < hi:
                        inside = jnp.logical_and(lane_il >= lo - c * V7X_LANES, lane_il < hi - c * V7X_LANES)
                        ycols[c][r] = jnp.where(inside, rolled, ycols[c][r])
        for c in range(ncol):
            for r in range(rep):
                y_ref[m, r * IL:(r + 1) * IL, c * V7X_LANES:(c + 1) * V7X_LANES] = ycols[c][r]
        return sa

    sa_ref[...] = lax.fori_loop(0, Tm, body, sa_ref[...])
    so_ref[...] = s_ref[...]


def _rwkv_scan(ops, v_p, s0_l, bh, rep):
    TP, _, W = v_p.shape
    IL = RWKV_HEAD // rep
    Tm = _tile(TP, 8)
    blk = pl.BlockSpec((Tm, RWKV_HEAD, W), lambda i: (i, 0, 0))
    nxt = pl.BlockSpec((1, RWKV_HEAD, W), lambda i: (jnp.minimum((i + 1) * Tm, TP - 1), 0, 0))
    sblk = _const_spec((RWKV_HEAD, IL, V7X_LANES))
    return pl.pallas_call(
        functools.partial(_rwkv_scan_kernel, Tm=Tm, IL=IL, bh=bh, rep=rep),
        grid=(TP // Tm,),
        in_specs=[blk] * 5 + [nxt, blk, sblk],
        out_specs=[blk, sblk],
        out_shape=[jax.ShapeDtypeStruct((TP, RWKV_HEAD, W), F32),
                   jax.ShapeDtypeStruct((RWKV_HEAD, IL, V7X_LANES), F32)],
        scratch_shapes=[pltpu.VMEM((RWKV_HEAD, IL, V7X_LANES), F32), pltpu.VMEM((IL, V7X_LANES), F32),
                        pltpu.VMEM((RWKV_PACK, 5, RWKV_HEAD, V7X_LANES), F32),
                        pltpu.VMEM((Tm + 1, RWKV_HEAD, W), F32)],
        compiler_params=_cparams("arbitrary"),
        name="rwkv_scan",
    )(*ops, ops[4], v_p, s0_l)


def _rwkv_layout(B):
    bh = B * RWKV_HEADS
    rep = 1
    while 2 * rep * bh <= V7X_LANES and RWKV_HEAD % (2 * rep) == 0:
        rep *= 2
    return bh, rep, RWKV_HEAD // rep


def _rwkv_mixer(p_tm, shift0, s0, w):
    T, B, _ = p_tm.shape
    bh, rep, IL = _rwkv_layout(B)
    lanes = rep * bh
    r, dec, k2, v, nkk, beta, g, bonus, shift = _rwkv_prep(p_tm, shift0, w)
    TP = T // RWKV_PACK

    def pack(x):
        x = x.reshape(TP, RWKV_PACK, bh, RWKV_HEAD).transpose(0, 3, 1, 2)
        return x.reshape(TP, RWKV_HEAD, RWKV_PACK * bh)

    ops = [pack(dec), pack(beta), pack(k2), pack(r), pack(nkk)]
    s0_l = s0.reshape(bh, rep, IL, RWKV_HEAD).transpose(3, 2, 1, 0).reshape(RWKV_HEAD, IL, lanes)
    s0_l = jnp.pad(s0_l, ((0, 0), (0, 0), (0, V7X_LANES - lanes)))
    y_p, s_l = _rwkv_scan(ops, pack(v), s0_l, bh, rep)
    y_tm = y_p.reshape(TP, RWKV_HEAD, RWKV_PACK, bh).transpose(0, 2, 3, 1).reshape(T, B, RWKV_W)
    s_last = s_l[:, :, :lanes].reshape(RWKV_HEAD, IL, rep, bh).transpose(3, 2, 1, 0)
    s_last = s_last.reshape(B, RWKV_HEADS, RWKV_HEAD, RWKV_HEAD)
    return (y_tm, bonus, g), s_last, shift


def _outproj_kernel(m_ref, s_ref, y_ref, bon_ref, gate_ref, x_ref, wm_ref, ws_ref, wr_ref, gng_ref, gnb_ref,
                    ones_ref, g_ref, b_ref, o_ref, *, alpha):
    B, tt, D = x_ref.shape
    mla = jnp.concatenate([m_ref[b].reshape(MLA_HEADS * V_DIM, tt).T for b in range(B)], axis=0)
    rows = lambda ref: jnp.concatenate([ref[:, b, :] for b in range(B)], axis=0)
    acc = jnp.dot(mla, wm_ref[...], preferred_element_type=F32)
    acc = acc + jnp.dot(rows(s_ref).astype(BF16), ws_ref[...], preferred_element_type=F32)
    y = rows(y_ref)
    inv_n = 1.0 / RWKV_HEAD
    mu = _head_sum(y, ones_ref) * inv_n
    yc = y - mu
    var = _head_sum(yc * yc, ones_ref) * inv_n
    yn = yc * lax.rsqrt(var + GN_EPS) * gng_ref[...] + gnb_ref[...]
    rw = (yn + rows(bon_ref)) * rows(gate_ref)
    acc = acc + jnp.dot(rw.astype(BF16), wr_ref[...], preferred_element_type=F32)
    out = _layer_norm(alpha * x_ref[...].reshape(B * tt, D) + acc, g_ref[...], b_ref[...])
    o_ref[...] = out.reshape(B, tt, D)


def _outproj(mla_t, s5_tm, rwkv, x, w, wr, alpha):
    B, T, D = x.shape
    tt = _tile(T, ROW_TILE)
    row = lambda i: (0, i, 0)
    tm = lambda i: (i, 0, 0)
    rblk = pl.BlockSpec((tt, B, RWKV_W), tm)
    vec = _const_spec((1, RWKV_W))
    return pl.pallas_call(
        functools.partial(_outproj_kernel, alpha=alpha),
        grid=(T // tt,),
        in_specs=[pl.BlockSpec((B, MLA_HEADS, V_DIM, tt), lambda i: (0, 0, 0, i)), pl.BlockSpec((tt, B, S5_W), tm),
                  rblk, rblk, rblk, pl.BlockSpec((B, tt, D), row),
                  _const_spec(w['wo_m'].shape), _const_spec(w['wo_s'].shape), _const_spec(w['wo_r'].shape),
                  vec, vec, _const_spec((RWKV_W, RWKV_W)), _const_spec((1, D)), _const_spec((1, D))],
        out_specs=pl.BlockSpec((B, tt, D), row),
        out_shape=jax.ShapeDtypeStruct((B, T, D), F32),
        compiler_params=_cparams("parallel"),
        name="outproj_ln",
    )(mla_t, s5_tm, *rwkv, x, w['wo_m'], w['wo_s'], w['wo_r'], wr['gng'], wr['gnb'], wr['ones'], w['ln1g'],
      w['ln1b'])


def _mlp_kernel(x_ref, wu_ref, wd_ref, g_ref, b_ref, o_ref, *, alpha, fc):
    x = x_ref[0]
    xb = x.astype(BF16)
    acc = alpha * x
    for c in range(wu_ref.shape[1] // fc):
        h = jnp.maximum(jnp.dot(xb, wu_ref[:, c * fc:(c + 1) * fc], preferred_element_type=F32), 0.0)
        acc = acc + jnp.dot((h * h).astype(BF16), wd_ref[c * fc:(c + 1) * fc, :], preferred_element_type=F32)
    o_ref[0] = _layer_norm(acc, g_ref[...], b_ref[...])


def _mlp(x, w, alpha):
    B, T, D = x.shape
    tt = _tile(T, 512)
    row = lambda b, i: (b, i, 0)
    return pl.pallas_call(
        functools.partial(_mlp_kernel, alpha=alpha, fc=1024),
        grid=(B, T // tt),
        in_specs=[pl.BlockSpec((1, tt, D), row), _const_spec(w['wup'].shape), _const_spec(w['wdown'].shape),
                  _const_spec((1, D)), _const_spec((1, D))],
        out_specs=pl.BlockSpec((1, tt, D), row),
        out_shape=jax.ShapeDtypeStruct((B, T, D), F32),
        compiler_params=_cparams("parallel", "parallel"),
        name="mlp_ln",
    )(x, w['wup'], w['wdown'], w['ln2g'], w['ln2b'])


def _rot_cols(wr):
    half = ROPE_DIM // 2
    return jnp.concatenate([-wr[..., half:], wr[..., :half]], axis=-1)


def _prep_layer(p):
    D = p['w_in'].shape[0]
    w_in = p['w_in']
    o_kv = Q_RANK
    o_kr = Q_RANK + KV_RANK
    o_u = o_kr + ROPE_DIM
    o_p = o_u + S5_W
    w_kr = w_in[:, o_kr:o_u]
    wm = jnp.concatenate([w_in[:, :o_kr], w_kr, _rot_cols(w_kr),
                          jnp.zeros((D, V7X_LANES - 2 * ROPE_DIM), F32)], axis=1)
    wqb = p['w_qb'].reshape(Q_RANK, MLA_HEADS, NOPE_DIM + ROPE_DIM) * (MLA_SCALE * LOG2_E)
    zq = jnp.zeros((Q_RANK, MLA_HEADS, HEAD_PAD - NOPE_DIM - ROPE_DIM), F32)
    wq1 = jnp.concatenate([wqb, zq], axis=-1).reshape(Q_RANK, MLA_HEADS * HEAD_PAD)
    wq2 = jnp.concatenate([jnp.zeros((Q_RANK, MLA_HEADS, NOPE_DIM), F32), _rot_cols(wqb[..., NOPE_DIM:]), zq],
                          axis=-1).reshape(Q_RANK, MLA_HEADS * HEAD_PAD)
    wkvb = p['w_kvb'].reshape(KV_RANK, MLA_HEADS, NOPE_DIM + V_DIM)
    wk = jnp.concatenate([wkvb[..., :NOPE_DIM], jnp.zeros((KV_RANK, MLA_HEADS, HEAD_PAD - NOPE_DIM), F32)],
                         axis=-1).reshape(KV_RANK, MLA_HEADS * HEAD_PAD)
    pk = jnp.concatenate([jnp.zeros((ROPE_DIM, NOPE_DIM), F32), jnp.eye(ROPE_DIM, dtype=F32),
                          jnp.zeros((ROPE_DIM, HEAD_PAD - NOPE_DIM - ROPE_DIM), F32)], axis=1)
    pk = jnp.tile(pk, (1, MLA_HEADS))
    wvt = wkvb[..., NOPE_DIM:].reshape(KV_RANK, MLA_HEADS * V_DIM).T

    lr, li = p['lam_re'], p['lam_im']
    dt = jnp.exp(p['log_dt'])[:, None]
    mag = jnp.exp(lr * dt)
    ar, ai = mag * jnp.cos(li * dt), mag * jnp.sin(li * dt)
    den = lr * lr + li * li
    cr = ((ar - 1.0) * lr + ai * li) / den
    ci = (ai * lr - (ar - 1.0) * li) / den
    bbr = cr[..., None] * p['b_re'] - ci[..., None] * p['b_im']
    bbi = cr[..., None] * p['b_im'] + ci[..., None] * p['b_re']
    eye_g = jnp.eye(S5_GROUPS, dtype=F32)
    bd_in = lambda m: (m.transpose(0, 2, 1)[:, :, None, :] * eye_g[:, None, :, None]).reshape(S5_W, S5_N)
    bd_out = lambda m: (m.transpose(0, 2, 1)[:, :, None, :] * eye_g[:, None, :, None]).reshape(S5_N, S5_W)
    wb = jnp.concatenate([bd_in(bbr), bd_in(bbi)], axis=1)
    wc = jnp.concatenate([bd_out(p['c_re']), -bd_out(p['c_im'])], axis=0)
    a = jnp.stack([ar.reshape(S5_N), ai.reshape(S5_N)])

    def lora_pad(w, off):
        z = jnp.zeros((V7X_LANES, RWKV_W), F32)
        return z.at[off:off + w.shape[0]].set(w)

    seg = jnp.arange(RWKV_W) // RWKV_HEAD
    ones = (seg[:, None] == seg[None, :]).astype(F32)
    row = lambda v: v.reshape(1, -1).astype(F32)
    wo = p['w_out']
    return dict(
        inproj=dict(wm=wm.astype(BF16), wu=w_in[:, o_u:o_p].astype(BF16), wp=w_in[:, o_p:].astype(BF16),
                    wq1=wq1.astype(BF16), wq2=wq2.astype(BF16), qg=row(p['q_norm_g']), kg=row(p['kv_norm_g'])),
        kvup=dict(wk=wk.astype(BF16), pk=pk.astype(BF16), wvt=wvt.astype(BF16)),
        s5=dict(a=a, wb=wb.astype(BF16), wc=wc.astype(BF16), d=row(p['s5_d']), wglu=p['w_glu'].astype(BF16),
                bglu=row(p['b_glu'])),
        rwkv=dict(mu=row(p['mu_shift']), ww=lora_pad(p['w_w2'], 0).astype(BF16),
                  wa=lora_pad(p['w_a2'], DECAY_LORA).astype(BF16),
                  wg=lora_pad(p['w_g2'], DECAY_LORA + AAA_LORA).astype(BF16),
                  w0=row(p['w0']), a0=row(p['a0']), kk=row(p['k_k']), ka=row(p['k_a']), rk=row(p['r_k']),
                  gng=row(p['gn_g']), gnb=row(p['gn_b']), ones=ones.astype(BF16)),
        out=dict(wo_m=wo[:MLA_HEADS * V_DIM].astype(BF16),
                 wo_s=wo[MLA_HEADS * V_DIM:MLA_HEADS * V_DIM + S5_W].astype(BF16),
                 wo_r=wo[MLA_HEADS * V_DIM + S5_W:].astype(BF16), ln1g=row(p['ln1_g']), ln1b=row(p['ln1_b'])),
        mlp=dict(wup=p['w_up'].astype(BF16), wdown=p['w_down'].astype(BF16), ln2g=row(p['ln2_g']),
                 ln2b=row(p['ln2_b'])),
    )


def _rope_tables(pos):
    inv_freq = ROPE_BASE ** (-jnp.arange(0, ROPE_DIM, 2, dtype=F32) / ROPE_DIM)
    ang = pos.astype(F32)[:, None] * inv_freq[None, :]
    ang = jnp.concatenate([ang, ang], -1)
    cos, sin = jnp.cos(ang), jnp.sin(ang)
    T = pos.shape[0]
    z = lambda n: jnp.zeros((T, n), F32)
    return dict(
        csk=jnp.concatenate([cos, sin, z(V7X_LANES - 2 * ROPE_DIM)], axis=1),
        cq=jnp.concatenate([jnp.ones((T, NOPE_DIM), F32), cos, z(HEAD_PAD - NOPE_DIM - ROPE_DIM)], axis=1),
        sq=jnp.concatenate([z(NOPE_DIM), sin, z(HEAD_PAD - NOPE_DIM - ROPE_DIM)], axis=1),
    )


def _trunk_layer(x, tabs, past, s5_state, rwkv_state, shift0, w, alpha):
    B, T, D = x.shape
    qcat, ckv, krope, u_tm, p_tm = _inproj(x, w['inproj'], tabs)
    kcat, vt = _kvup(ckv, krope, w['kvup'])
    if past is None:
        mla = _attention(qcat, kcat, vt, kv_len=T, chunk_causal=True)
    else:
        kv_len = past[0].shape[1] + T
        bk = past[1].shape[4]
        kcat = jnp.concatenate([past[0], kcat], axis=1)
        kcat = jnp.pad(kcat, ((0, 0), (0, -kv_len % bk), (0, 0)))
        vt = jnp.pad(vt, ((0, 0), (0, 0), (0, 0), (0, 0), (0, bk - vt.shape[4])))
        vt = jnp.concatenate([past[1], vt], axis=2)
        mla = _attention(qcat, kcat, vt, kv_len=kv_len, chunk_causal=False)

    x0 = jnp.concatenate([s5_state[..., 0].reshape(B, S5_N), s5_state[..., 1].reshape(B, S5_N)], axis=1)
    s5_out, x_last = _s5(u_tm, x0, w['s5'])
    new_s5 = jnp.stack([x_last[:, :S5_N].reshape(B, S5_GROUPS, S5_STATE),
                        x_last[:, S5_N:].reshape(B, S5_GROUPS, S5_STATE)], axis=-1)

    rwkv, new_rwkv, shift = _rwkv_mixer(p_tm, shift0.reshape(B, RWKV_PROJ), rwkv_state, w['rwkv'])
    x = _outproj(mla, s5_out, rwkv, x, w['out'], w['rwkv'], alpha)
    x = _mlp(x, w['mlp'], alpha)
    return x, ckv, krope, new_s5, new_rwkv, shift.reshape(B, 1, RWKV_PROJ)


def kernel(x_prompt, x_sample, cache_mla_ckv, cache_mla_krope, state_s5, state_rwkv, state_rwkv_shift, w_in, q_norm_g, w_qb, kv_norm_g, w_kvb, lam_re, lam_im, log_dt, b_re, b_im, c_re, c_im, s5_d, w_glu, b_glu, mu_shift, w0, w_w2, a0, w_a2, w_g2, k_k, k_a, r_k, gn_g, gn_b, w_out, ln1_g, ln1_b, w_up, w_down, ln2_g, ln2_b):
    depth = w_in.shape[0]
    alpha = (2 * depth) ** 0.25
    bp, sp = x_prompt.shape[:2]
    ts = x_sample.shape[1]
    past = cache_mla_ckv.shape[2]
    tabs_p = _rope_tables(jnp.arange(sp))
    tabs_s = _rope_tables(past + jnp.arange(ts))
    names = dict(w_in=w_in, q_norm_g=q_norm_g, w_qb=w_qb, kv_norm_g=kv_norm_g, w_kvb=w_kvb, lam_re=lam_re,
                 lam_im=lam_im, log_dt=log_dt, b_re=b_re, b_im=b_im, c_re=c_re, c_im=c_im, s5_d=s5_d, w_glu=w_glu,
                 b_glu=b_glu, mu_shift=mu_shift, w0=w0, w_w2=w_w2, a0=a0, w_a2=w_a2, w_g2=w_g2, k_k=k_k, k_a=k_a,
                 r_k=r_k, gn_g=gn_g, gn_b=gn_b, w_out=w_out, ln1_g=ln1_g, ln1_b=ln1_b, w_up=w_up, w_down=w_down,
                 ln2_g=ln2_g, ln2_b=ln2_b)
    s5_zero = jnp.zeros((bp, S5_GROUPS, S5_STATE, 2), F32)
    rwkv_zero = jnp.zeros((bp, RWKV_HEADS, RWKV_HEAD, RWKV_HEAD), F32)
    shift_zero = jnp.zeros((bp, 1, RWKV_PROJ), F32)

    xp, xs = x_prompt, x_sample
    outs_p = [[] for _ in range(5)]
    outs_s = [[] for _ in range(5)]
    for l in range(depth):
        w = _prep_layer({k: v[l] for k, v in names.items()})
        xp, *rest = _trunk_layer(xp, tabs_p, None, s5_zero, rwkv_zero, shift_zero, w, alpha)
        for acc, val in zip(outs_p, rest):
            acc.append(val)
        past_kv = _kvup(cache_mla_ckv[l], cache_mla_krope[l], w['kvup'])
        xs, *rest = _trunk_layer(xs, tabs_s, past_kv, state_s5[l], state_rwkv[l], state_rwkv_shift[l], w, alpha)
        for acc, val in zip(outs_s, rest):
            acc.append(val)
    return (xp, xs, *[jnp.stack(a) for a in outs_p], *[jnp.stack(a) for a in outs_s])
```

```python
import functools
import math

import jax
import jax.numpy as jnp
from jax import lax
from jax.experimental import pallas as pl
from jax.experimental.pallas import tpu as pltpu

F32 = jnp.float32
BF16 = jnp.bfloat16

CHUNK = 64
MLA_HEADS = 6
NOPE_DIM = 64
ROPE_DIM = 32
V_DIM = 64
Q_RANK = 256
KV_RANK = 128
ROPE_BASE = 10000.0
MLA_SCALE = (NOPE_DIM + ROPE_DIM) ** -0.5
S5_GROUP_CH = 16
S5_W = 256
S5_GROUPS = S5_W // S5_GROUP_CH
S5_STATE = 64
S5_N = S5_GROUPS * S5_STATE
RWKV_HEADS = 6
RWKV_HEAD = 64
RWKV_W = RWKV_HEADS * RWKV_HEAD
DECAY_LORA = 32
AAA_LORA = 32
GATE_LORA = 64
RWKV_PROJ = 3 * RWKV_W + DECAY_LORA + AAA_LORA + GATE_LORA
LN_EPS = 1e-5
RMS_EPS = 1e-6
GN_EPS = 64e-5
NEG_INF = -1e30

V7X_LANES = 128
V7X_VMEM_BYTES = 64 * 1024 * 1024
VMEM_LIMIT_BYTES = V7X_VMEM_BYTES - 8 * 1024 * 1024
HEAD_PAD = V7X_LANES
ATTN_BLOCK = 512
ROW_TILE = 128
ATTN_HEAD_GROUP = 3
V_ROWS = V_DIM + 16
LOG2_E = 1.4426950408889634
RWKV_PACK = 8


def _cparams(*sem):
    return pltpu.CompilerParams(dimension_semantics=sem, vmem_limit_bytes=VMEM_LIMIT_BYTES)


def _const_spec(shape):
    zeros = (0,) * len(shape)
    return pl.BlockSpec(shape, lambda *_: zeros)


def _tile(n, target):
    if n <= target:
        return n
    t = target
    while n % t:
        t //= 2
    return t


def _layer_norm(y, g, b):
    mu = jnp.mean(y, -1, keepdims=True)
    yc = y - mu
    var = jnp.mean(yc * yc, -1, keepdims=True)
    return yc * lax.rsqrt(var + LN_EPS) * g + b


def _inproj_kernel(x_ref, wm_ref, wu_ref, wp_ref, wq1_ref, wq2_ref, qg_ref, kg_ref, csk_ref, cq_ref, sq_ref,
                   qcat_ref, ckv_ref, krope_ref, u_ref, p_ref):
    B, tt, D = x_ref.shape
    x = x_ref[...].reshape(B * tt, D).astype(BF16)
    per_b = lambda tab_ref: jnp.concatenate([tab_ref[...]] * B, axis=0)
    m = jnp.dot(x, wm_ref[...], preferred_element_type=F32)
    q_lat = m[:, :Q_RANK]
    qn = q_lat * lax.rsqrt(jnp.mean(q_lat * q_lat, -1, keepdims=True) + RMS_EPS) * qg_ref[...]
    kv_lat = m[:, Q_RANK:Q_RANK + KV_RANK]
    ckv = kv_lat * lax.rsqrt(jnp.mean(kv_lat * kv_lat, -1, keepdims=True) + RMS_EPS) * kg_ref[...]
    ckv_ref[...] = ckv.reshape(B, tt, KV_RANK)
    kr = m[:, Q_RANK + KV_RANK:] * per_b(csk_ref)
    kr = kr + pltpu.roll(kr, V7X_LANES - ROPE_DIM, 1)
    krope_ref[...] = kr[:, :ROPE_DIM].reshape(B, tt, ROPE_DIM)
    qb = qn.astype(BF16)
    q1 = jnp.dot(qb, wq1_ref[...], preferred_element_type=F32)
    q2 = jnp.dot(qb, wq2_ref[...], preferred_element_type=F32)
    cq = per_b(cq_ref)
    sq = per_b(sq_ref)
    for h in range(MLA_HEADS):
        sl = slice(h * HEAD_PAD, (h + 1) * HEAD_PAD)
        qcat_ref[:, :, sl] = (q1[:, sl] * cq + q2[:, sl] * sq).astype(BF16).reshape(B, tt, HEAD_PAD)
    u = jnp.dot(x, wu_ref[...], preferred_element_type=F32)
    p = jnp.dot(x, wp_ref[...], preferred_element_type=F32)
    for b in range(B):
        u_ref[:, b, :] = u[b * tt:(b + 1) * tt]
        p_ref[:, b, :] = p[b * tt:(b + 1) * tt]


def _inproj(x, w, tabs):
    B, T, D = x.shape
    tt = _tile(T, ROW_TILE)
    nq = MLA_HEADS * HEAD_PAD
    row = lambda i: (0, i, 0)
    tab = lambda i: (i, 0)
    tm = lambda i: (i, 0, 0)
    return pl.pallas_call(
        _inproj_kernel,
        grid=(T // tt,),
        in_specs=[
            pl.BlockSpec((B, tt, D), row),
            _const_spec(w['wm'].shape), _const_spec(w['wu'].shape), _const_spec(w['wp'].shape),
            _const_spec(w['wq1'].shape), _const_spec(w['wq2'].shape),
            _const_spec((1, Q_RANK)), _const_spec((1, KV_RANK)),
            pl.BlockSpec((tt, V7X_LANES), tab), pl.BlockSpec((tt, HEAD_PAD), tab), pl.BlockSpec((tt, HEAD_PAD), tab),
        ],
        out_specs=[
            pl.BlockSpec((B, tt, nq), row),
            pl.BlockSpec((B, tt, KV_RANK), row),
            pl.BlockSpec((B, tt, ROPE_DIM), row),
            pl.BlockSpec((tt, B, S5_W), tm),
            pl.BlockSpec((tt, B, RWKV_PROJ), tm),
        ],
        out_shape=[
            jax.ShapeDtypeStruct((B, T, nq), BF16),
            jax.ShapeDtypeStruct((B, T, KV_RANK), F32),
            jax.ShapeDtypeStruct((B, T, ROPE_DIM), F32),
            jax.ShapeDtypeStruct((T, B, S5_W), F32),
            jax.ShapeDtypeStruct((T, B, RWKV_PROJ), F32),
        ],
        compiler_params=_cparams("parallel"),
        name="inproj",
    )(x, w['wm'], w['wu'], w['wp'], w['wq1'], w['wq2'], w['qg'], w['kg'], tabs['csk'], tabs['cq'], tabs['sq'])


def _kvup_kernel(ckv_ref, kr_ref, wk_ref, pk_ref, wvt_ref, kcat_ref, vt_ref):
    c = ckv_ref[0].astype(BF16)
    kr = kr_ref[0].astype(BF16)
    kcat = jnp.dot(c, wk_ref[...], preferred_element_type=F32) + jnp.dot(kr, pk_ref[...], preferred_element_type=F32)
    kcat_ref[0] = kcat.astype(BF16)
    vt = lax.dot_general(wvt_ref[...], c, (((1,), (1,)), ((), ())), preferred_element_type=F32).astype(BF16)
    ones = jnp.ones((V_ROWS - V_DIM, vt.shape[1]), BF16)
    for h in range(MLA_HEADS):
        vt_ref[0, h, 0, :V_DIM, :] = vt[h * V_DIM:(h + 1) * V_DIM, :]
        vt_ref[0, h, 0, V_DIM:, :] = ones


def _kvup(ckv, krope, w):
    B, T, _ = ckv.shape
    tt = _tile(T, ATTN_BLOCK)
    nk = MLA_HEADS * HEAD_PAD
    row = lambda b, i: (b, i, 0)
    return pl.pallas_call(
        _kvup_kernel,
        grid=(B, T // tt),
        in_specs=[pl.BlockSpec((1, tt, KV_RANK), row), pl.BlockSpec((1, tt, ROPE_DIM), row),
                  _const_spec(w['wk'].shape), _const_spec(w['pk'].shape), _const_spec(w['wvt'].shape)],
        out_specs=[pl.BlockSpec((1, tt, nk), row),
                   pl.BlockSpec((1, MLA_HEADS, 1, V_ROWS, tt), lambda b, i: (b, 0, i, 0, 0))],
        out_shape=[jax.ShapeDtypeStruct((B, T, nk), BF16),
                   jax.ShapeDtypeStruct((B, MLA_HEADS, T // tt, V_ROWS, tt), BF16)],
        compiler_params=_cparams("parallel", "parallel"),
        name="kvup",
    )(ckv, krope, w['wk'], w['pk'], w['wvt'])


def _attn_kernel(q_ref, k_ref, v_ref, o_ref, sa_ref, sb_ref, bias_ref, m_ref, acc_ref, *, bq, bk, kv_len,
                 chunk_causal):
    hg = ATTN_HEAD_GROUP
    qi = pl.program_id(2)
    last = qi if chunk_causal else jnp.int32(-(-kv_len // bk) - 1)
    nt = (((1,), (1,)), ((), ()))

    def scores(j, s_ref, h):
        r0 = pl.multiple_of(j * bk, bk)
        sl = slice(h * HEAD_PAD, (h + 1) * HEAD_PAD)
        s_ref[h] = lax.dot_general(k_ref[0, pl.ds(r0, bk), sl], q_ref[0, :, sl], nt,
                                   preferred_element_type=F32)

    def softmax_pv(j, s_ref, h, masked):
        s = s_ref[h]
        if masked:
            s = s + bias_ref[...]
        m = m_ref[h]
        m_new = jnp.maximum(m, jnp.max(s, axis=0, keepdims=True))
        p = jnp.exp2(s - m_new)
        alpha = jnp.exp2(m - m_new)
        acc_ref[h] = alpha * acc_ref[h] + jnp.dot(v_ref[0, h, j], p.astype(BF16),
                                                  preferred_element_type=F32)
        m_ref[h] = m_new

    @pl.when(qi == 0)
    def _():
        krel = lax.broadcasted_iota(jnp.int32, (bk, bq), 0)
        if chunk_causal:
            qrel = lax.broadcasted_iota(jnp.int32, (bk, bq), 1)
            ok = (krel // CHUNK) <= (qrel // CHUNK)
        else:
            ok = krel < kv_len - (-(-kv_len // bk) - 1) * bk
        bias_ref[...] = jnp.where(ok, 0.0, NEG_INF)

    def advance(j_next, s_next, j, s_cur, masked=False):
        for h in range(hg):
            if j_next is not None:
                scores(j_next, s_next, h)
            softmax_pv(j, s_cur, h, masked)

    m_ref[...] = jnp.full(m_ref.shape, NEG_INF, F32)
    acc_ref[...] = jnp.zeros(acc_ref.shape, F32)
    for h in range(hg):
        scores(0, sa_ref, h)

    def pair(p, _):
        j = 2 * p
        advance(j + 1, sb_ref, j, sa_ref)
        advance(j + 2, sa_ref, j + 1, sb_ref)
        return 0

    lax.fori_loop(0, last // 2, pair, 0)

    @pl.when(last % 2 == 0)
    def _():
        advance(None, None, last, sa_ref, True)

    @pl.when(last % 2 == 1)
    def _():
        advance(last, sb_ref, last - 1, sa_ref)
        advance(None, None, last, sb_ref, True)

    for h in range(hg):
        o_ref[0, h] = (acc_ref[h, :V_DIM, :] / acc_ref[h, V_DIM:V_DIM + 1, :]).astype(BF16)


def _attention(qcat, kcat, vt, *, kv_len, chunk_causal):
    B, Tq, _ = qcat.shape
    Tk = kcat.shape[1]
    nblk, bk = vt.shape[2], vt.shape[4]
    assert nblk * bk == Tk
    bq = bk if chunk_causal else Tq
    hg = ATTN_HEAD_GROUP
    out_t = pl.pallas_call(
        functools.partial(_attn_kernel, bq=bq, bk=bk, kv_len=kv_len, chunk_causal=chunk_causal),
        grid=(B, MLA_HEADS // hg, Tq // bq),
        in_specs=[
            pl.BlockSpec((1, bq, hg * HEAD_PAD), lambda b, g, i: (b, i, g)),
            pl.BlockSpec((1, Tk, hg * HEAD_PAD), lambda b, g, i: (b, 0, g)),
            pl.BlockSpec((1, hg, nblk, V_ROWS, bk), lambda b, g, i: (b, g, 0, 0, 0)),
        ],
        out_specs=pl.BlockSpec((1, hg, V_DIM, bq), lambda b, g, i: (b, g, 0, i)),
        out_shape=jax.ShapeDtypeStruct((B, MLA_HEADS, V_DIM, Tq), BF16),
        scratch_shapes=[pltpu.VMEM((hg, bk, bq), F32), pltpu.VMEM((hg, bk, bq), F32), pltpu.VMEM((bk, bq), F32),
                        pltpu.VMEM((hg, 1, bq), F32), pltpu.VMEM((hg, V_ROWS, bq), F32)],
        compiler_params=_cparams("parallel", "parallel", "arbitrary"),
        name="mla_attention",
    )(qcat, kcat, vt)
    return out_t


def _gelu_tanh(x):
    return 0.5 * x * (1.0 + jnp.tanh(math.sqrt(2.0 / math.pi) * (x + 0.044715 * (x * x * x))))


def _s5_kernel(u_ref, x0_ref, a_ref, wb_ref, wc_ref, d_ref, wg_ref, bg_ref, o_ref, xl_ref, xs_ref, st_ref, *, L, B):
    @pl.when(pl.program_id(0) == 0)
    def _():
        st_ref[...] = x0_ref[...]

    u = u_ref[...].reshape(L * B, S5_W)
    xs_ref[...] = jnp.dot(u.astype(BF16), wb_ref[...], preferred_element_type=F32)
    a_re = a_ref[0:1, :]
    a_im = a_ref[1:2, :]

    def body(t, carry):
        x_re, x_im = carry
        r0 = pl.multiple_of(t * B, B)
        n_re = a_re * x_re - a_im * x_im + xs_ref[pl.ds(r0, B), :S5_N]
        n_im = a_re * x_im + a_im * x_re + xs_ref[pl.ds(r0, B), S5_N:]
        xs_ref[pl.ds(r0, B), :S5_N] = n_re
        xs_ref[pl.ds(r0, B), S5_N:] = n_im
        return n_re, n_im

    x_re, x_im = lax.fori_loop(0, L, body, (st_ref[:, :S5_N], st_ref[:, S5_N:]), unroll=8)
    st_ref[:, :S5_N] = x_re
    st_ref[:, S5_N:] = x_im
    xl_ref[...] = st_ref[...]

    y = jnp.dot(xs_ref[...].astype(BF16), wc_ref[...], preferred_element_type=F32) + d_ref[...] * u
    z = _gelu_tanh(y)
    gate = jax.nn.sigmoid(jnp.dot(z.astype(BF16), wg_ref[...], preferred_element_type=F32) + bg_ref[...])
    o_ref[...] = (z * gate).reshape(L, B, S5_W)


def _s5(u_tm, x0, w):
    T, B, _ = u_tm.shape
    L = _tile(T, 1024 // B)
    return pl.pallas_call(
        functools.partial(_s5_kernel, L=L, B=B),
        grid=(T // L,),
        in_specs=[
            pl.BlockSpec((L, B, S5_W), lambda i: (i, 0, 0)),
            _const_spec((B, 2 * S5_N)), _const_spec((2, S5_N)),
            _const_spec((S5_W, 2 * S5_N)), _const_spec((2 * S5_N, S5_W)),
            _const_spec((1, S5_W)), _const_spec((S5_W, S5_W)), _const_spec((1, S5_W)),
        ],
        out_specs=[pl.BlockSpec((L, B, S5_W), lambda i: (i, 0, 0)), _const_spec((B, 2 * S5_N))],
        out_shape=[jax.ShapeDtypeStruct((T, B, S5_W), F32), jax.ShapeDtypeStruct((B, 2 * S5_N), F32)],
        scratch_shapes=[pltpu.VMEM((L * B, 2 * S5_N), F32), pltpu.VMEM((B, 2 * S5_N), F32)],
        compiler_params=_cparams("arbitrary"),
        name="s5_scan",
    )(u_tm, x0, w['a'], w['wb'], w['wc'], w['d'], w['wglu'], w['bglu'])


def _softplus(x):
    return jnp.maximum(x, 0.0) + jnp.log(1.0 + jnp.exp(-jnp.abs(x)))


def _head_sum(x, ones_ref):
    hi = x.astype(BF16)
    lo = (x - hi.astype(F32)).astype(BF16)
    ones = ones_ref[...]
    return jnp.dot(hi, ones, preferred_element_type=F32) + jnp.dot(lo, ones, preferred_element_type=F32)


def _rwkv_prep_kernel(p_ref, sh0_ref, mu_ref, ww_ref, wa_ref, wg_ref, w0_ref, a0_ref, ka_ref, rk_ref,
                      ones_ref, r_o, w_o, k_o, v_o, a_o, g_o, bon_o, sh_o, last_ref, *, L, B):
    @pl.when(pl.program_id(0) == 0)
    def _():
        last_ref[...] = sh0_ref[...]

    p = p_ref[...].reshape(L * B, RWKV_PROJ)
    if L > 1:
        prev = jnp.concatenate([last_ref[...], p[:(L - 1) * B]], axis=0)
    else:
        prev = last_ref[...]
    last_ref[...] = p[(L - 1) * B:]
    sh_o[...] = p[(L - 1) * B:]
    ps = p + (prev - p) * mu_ref[...]
    r = ps[:, :RWKV_W]
    k = ps[:, RWKV_W:2 * RWKV_W]
    v = ps[:, 2 * RWKV_W:3 * RWKV_W]
    tail = ps[:, 3 * RWKV_W:]
    lw = jnp.dot(jnp.tanh(tail).astype(BF16), ww_ref[...], preferred_element_type=F32)
    la = jnp.dot(tail.astype(BF16), wa_ref[...], preferred_element_type=F32)
    g = jnp.dot(jax.nn.sigmoid(tail).astype(BF16), wg_ref[...], preferred_element_type=F32)
    w_log = -_softplus(-(w0_ref[...] + lw)) - 0.5
    decay = jnp.exp(-jnp.exp(w_log))
    a = jax.nn.sigmoid(a0_ref[...] + la)
    k2 = k * (1.0 + (a - 1.0) * ka_ref[...])
    bonus = _head_sum(r * k2 * rk_ref[...], ones_ref) * v
    shp = (L, B, RWKV_W)
    r_o[...] = r.reshape(shp)
    w_o[...] = decay.reshape(shp)
    k_o[...] = k.reshape(shp)
    v_o[...] = v.reshape(shp)
    a_o[...] = a.reshape(shp)
    g_o[...] = g.reshape(shp)
    bon_o[...] = bonus.reshape(shp)


def _rwkv_prep(p_tm, shift0, w):
    T, B, _ = p_tm.shape
    L = _tile(T, 512 // B)
    blk = pl.BlockSpec((L, B, RWKV_W), lambda i: (i, 0, 0))
    vec = _const_spec((1, RWKV_W))
    outs = pl.pallas_call(
        functools.partial(_rwkv_prep_kernel, L=L, B=B),
        grid=(T // L,),
        in_specs=[pl.BlockSpec((L, B, RWKV_PROJ), lambda i: (i, 0, 0)), _const_spec((B, RWKV_PROJ)),
                  _const_spec((1, RWKV_PROJ)),
                  _const_spec((V7X_LANES, RWKV_W)), _const_spec((V7X_LANES, RWKV_W)), _const_spec((V7X_LANES, RWKV_W)),
                  vec, vec, vec, vec, _const_spec((RWKV_W, RWKV_W))],
        out_specs=[blk] * 7 + [_const_spec((B, RWKV_PROJ))],
        out_shape=[jax.ShapeDtypeStruct((T, B, RWKV_W), F32)] * 7 + [jax.ShapeDtypeStruct((B, RWKV_PROJ), F32)],
        scratch_shapes=[pltpu.VMEM((B, RWKV_PROJ), F32)],
        compiler_params=_cparams("arbitrary"),
        name="rwkv_prep",
    )(p_tm, shift0, w['mu'], w['ww'], w['wa'], w['wg'], w['w0'], w['a0'], w['ka'], w['rk'], w['ones'])
    return outs


def _lane_window(tile, off, width, dst, lane):
    v0, lo = divmod(off, V7X_LANES)
    shift = (dst - lo) % V7X_LANES
    a = tile(v0)
    r = pltpu.roll(a, shift, 1) if shift else a
    if lo + width > V7X_LANES:
        b = tile(v0 + 1)
        r2 = pltpu.roll(b, shift, 1) if shift else b
        r = jnp.where(lane < dst + (V7X_LANES - lo), r, r2)
    return r


def _chain_lanes(pieces, bh, lane):
    rep = len(pieces)
    out = jnp.where(lane < rep * bh, pieces[rep - 1], 0.0)
    for r in range(rep - 2, -1, -1):
        out = jnp.where(lane < (r + 1) * bh, pieces[r], out)
    return out


def _rwkv_scan_kernel(w_ref, a_ref, k_ref, kx_ref, r_ref, v_ref, kkt_ref, kat_ref, s0_ref, y_ref, so_ref,
                      s_ref, sa_ref, e_ref, ev_ref, ys_ref, kall_ref, *, Tm, IL, bh, rep):
    lane = lax.broadcasted_iota(jnp.int32, (RWKV_HEAD, V7X_LANES), 1)
    lane_il = lax.broadcasted_iota(jnp.int32, (IL, V7X_LANES), 1)
    ncol = RWKV_PACK * bh // V7X_LANES

    def tiles(slab, rows=slice(None)):
        return lambda c: slab[rows, c * V7X_LANES:(c + 1) * V7X_LANES]

    def expand(slab, t8):
        return _chain_lanes([_lane_window(tiles(slab), t8 * bh, bh, r * bh, lane) for r in range(rep)], bh, lane)

    def unit_kk(k_raw):
        kk = k_raw * kkt_ref[...]
        nrm = jnp.sqrt(jnp.sum(kk * kk, axis=0, keepdims=True))
        return kk / jnp.maximum(nrm, 1e-12)

    kall_ref[0:Tm] = k_ref[...]
    kall_ref[Tm] = kx_ref[0]

    @pl.when(pl.program_id(0) == 0)
    def _():
        s_ref[...] = s0_ref[...]
        e_ref[0, 4] = expand(-unit_kk(k_ref[0]), 0)
        acc = jnp.zeros((IL, V7X_LANES), F32)
        for j in range(RWKV_HEAD):
            acc = acc + s0_ref[j] * e_ref[0, 4, j:j + 1, :]
        sa_ref[...] = acc

    def body(m, sa):
        k_raw = kall_ref[m]
        a_s = a_ref[m]
        kkn = unit_kk(k_raw)
        slabs = [w_ref[m], kkn * a_s, k_raw * (1.0 + (a_s - 1.0) * kat_ref[...]), r_ref[m]]
        nn0 = -kkn
        nn1 = -unit_kk(kall_ref[m + 1])
        vs = v_ref[m]
        for t8 in range(RWKV_PACK):
            for o, slab in enumerate(slabs):
                e_ref[t8, o] = expand(slab, t8)
            e_ref[t8, 4] = expand(nn0, t8 + 1) if t8 + 1 < RWKV_PACK else expand(nn1, 0)
            ev_ref[t8] = _chain_lanes([_lane_window(tiles(vs, slice(r * IL, (r + 1) * IL)), t8 * bh, bh, r * bh,
                                                    lane_il) for r in range(rep)], bh, lane_il)
        for t8 in range(RWKV_PACK):
            vt = ev_ref[t8]
            yacc = jnp.zeros((IL, V7X_LANES), F32)
            san = jnp.zeros((IL, V7X_LANES), F32)
            for j in range(RWKV_HEAD):
                row = lambda o: e_ref[t8, o, j:j + 1, :]
                sn = s_ref[j] * row(0) + sa * row(1) + vt * row(2)
                s_ref[j] = sn
                yacc = yacc + sn * row(3)
                san = san + sn * row(4)
            sa = san
            ys_ref[t8] = yacc
        for c in range(ncol):
            for r in range(rep):
                col = jnp.zeros((IL, V7X_LANES), F32)
                for t8 in range(RWKV_PACK):
                    off = t8 * bh
                    lo, hi = max(off, c * V7X_LANES), min(off + bh, (c + 1) * V7X_LANES)
                    if lo < hi:
                        shift = (off - r * bh) % V7X_LANES
                        rolled = pltpu.roll(ys_ref[t8], shift, 1) if shift else ys_ref[t8]
                        inside = jnp.logical_and(lane_il >= lo - c * V7X_LANES, lane_il < hi - c * V7X_LANES)
                        col = jnp.where(inside, rolled, col)
                y_ref[m, r * IL:(r + 1) * IL, c * V7X_LANES:(c + 1) * V7X_LANES] = col
        return sa

    sa_ref[...] = lax.fori_loop(0, Tm, body, sa_ref[...])
    so_ref[...] = s_ref[...]


def _rwkv_scan(w_p, a_p, k_p, r_p, v_p, kk_t, ka_t, s0_l, bh, rep):
    TP, _, W = v_p.shape
    IL = RWKV_HEAD // rep
    Tm = _tile(TP, 8)
    blk = pl.BlockSpec((Tm, RWKV_HEAD, W), lambda i: (i, 0, 0))
    nxt = pl.BlockSpec((1, RWKV_HEAD, W), lambda i: (jnp.minimum((i + 1) * Tm, TP - 1), 0, 0))
    sblk = _const_spec((RWKV_HEAD, IL, V7X_LANES))
    par = _const_spec((RWKV_HEAD, W))
    return pl.pallas_call(
        functools.partial(_rwkv_scan_kernel, Tm=Tm, IL=IL, bh=bh, rep=rep),
        grid=(TP // Tm,),
        in_specs=[blk, blk, blk, nxt, blk, blk, par, par, sblk],
        out_specs=[blk, sblk],
        out_shape=[jax.ShapeDtypeStruct((TP, RWKV_HEAD, W), F32),
                   jax.ShapeDtypeStruct((RWKV_HEAD, IL, V7X_LANES), F32)],
        scratch_shapes=[pltpu.VMEM((RWKV_HEAD, IL, V7X_LANES), F32), pltpu.VMEM((IL, V7X_LANES), F32),
                        pltpu.VMEM((RWKV_PACK, 5, RWKV_HEAD, V7X_LANES), F32),
                        pltpu.VMEM((RWKV_PACK, IL, V7X_LANES), F32), pltpu.VMEM((RWKV_PACK, IL, V7X_LANES), F32),
                        pltpu.VMEM((Tm + 1, RWKV_HEAD, W), F32)],
        compiler_params=_cparams("arbitrary"),
        name="rwkv_scan",
    )(w_p, a_p, k_p, k_p, r_p, v_p, kk_t, ka_t, s0_l)


def _rwkv_layout(B):
    bh = B * RWKV_HEADS
    rep = 1
    while 2 * rep * bh <= V7X_LANES and RWKV_HEAD % (2 * rep) == 0:
        rep *= 2
    return bh, rep, RWKV_HEAD // rep


def _rwkv_mixer(p_tm, shift0, s0, w):
    T, B, _ = p_tm.shape
    bh, rep, IL = _rwkv_layout(B)
    lanes = rep * bh
    r, dec, k, v, a, g, bonus, shift = _rwkv_prep(p_tm, shift0, w)
    TP = T // RWKV_PACK

    def pack(x):
        x = x.reshape(TP, RWKV_PACK, bh, RWKV_HEAD).transpose(0, 3, 1, 2)
        return x.reshape(TP, RWKV_HEAD, RWKV_PACK * bh)

    def par_tile(vec):
        t = vec.reshape(RWKV_HEADS, RWKV_HEAD).T[:, None, None, :]
        return jnp.broadcast_to(t, (RWKV_HEAD, RWKV_PACK, B, RWKV_HEADS)).reshape(RWKV_HEAD, RWKV_PACK * bh)

    s0_l = s0.reshape(bh, rep, IL, RWKV_HEAD).transpose(3, 2, 1, 0).reshape(RWKV_HEAD, IL, lanes)
    s0_l = jnp.pad(s0_l, ((0, 0), (0, 0), (0, V7X_LANES - lanes)))
    y_p, s_l = _rwkv_scan(pack(dec), pack(a), pack(k), pack(r), pack(v), par_tile(w['kk']), par_tile(w['ka']),
                          s0_l, bh, rep)
    y_tm = y_p.reshape(TP, RWKV_HEAD, RWKV_PACK, bh).transpose(0, 2, 3, 1).reshape(T, B, RWKV_W)
    s_last = s_l[:, :, :lanes].reshape(RWKV_HEAD, IL, rep, bh).transpose(3, 2, 1, 0)
    s_last = s_last.reshape(B, RWKV_HEADS, RWKV_HEAD, RWKV_HEAD)
    return (y_tm, bonus, g), s_last, shift


def _outproj_kernel(m_ref, s_ref, y_ref, bon_ref, gate_ref, x_ref, wm_ref, ws_ref, wr_ref, gng_ref, gnb_ref,
                    ones_ref, g_ref, b_ref, o_ref, *, alpha):
    B, tt, D = x_ref.shape
    mla = jnp.concatenate([m_ref[b].reshape(MLA_HEADS * V_DIM, tt).T for b in range(B)], axis=0)
    rows = lambda ref: jnp.concatenate([ref[:, b, :] for b in range(B)], axis=0)
    acc = jnp.dot(mla, wm_ref[...], preferred_element_type=F32)
    acc = acc + jnp.dot(rows(s_ref).astype(BF16), ws_ref[...], preferred_element_type=F32)
    y = rows(y_ref)
    inv_n = 1.0 / RWKV_HEAD
    mu = _head_sum(y, ones_ref) * inv_n
    yc = y - mu
    var = _head_sum(yc * yc, ones_ref) * inv_n
    yn = yc * lax.rsqrt(var + GN_EPS) * gng_ref[...] + gnb_ref[...]
    rw = (yn + rows(bon_ref)) * rows(gate_ref)
    acc = acc + jnp.dot(rw.astype(BF16), wr_ref[...], preferred_element_type=F32)
    out = _layer_norm(alpha * x_ref[...].reshape(B * tt, D) + acc, g_ref[...], b_ref[...])
    o_ref[...] = out.reshape(B, tt, D)


def _outproj(mla_t, s5_tm, rwkv, x, w, wr, alpha):
    B, T, D = x.shape
    tt = _tile(T, ROW_TILE)
    row = lambda i: (0, i, 0)
    tm = lambda i: (i, 0, 0)
    rblk = pl.BlockSpec((tt, B, RWKV_W), tm)
    vec = _const_spec((1, RWKV_W))
    return pl.pallas_call(
        functools.partial(_outproj_kernel, alpha=alpha),
        grid=(T // tt,),
        in_specs=[pl.BlockSpec((B, MLA_HEADS, V_DIM, tt), lambda i: (0, 0, 0, i)), pl.BlockSpec((tt, B, S5_W), tm),
                  rblk, rblk, rblk, pl.BlockSpec((B, tt, D), row),
                  _const_spec(w['wo_m'].shape), _const_spec(w['wo_s'].shape), _const_spec(w['wo_r'].shape),
                  vec, vec, _const_spec((RWKV_W, RWKV_W)), _const_spec((1, D)), _const_spec((1, D))],
        out_specs=pl.BlockSpec((B, tt, D), row),
        out_shape=jax.ShapeDtypeStruct((B, T, D), F32),
        compiler_params=_cparams("parallel"),
        name="outproj_ln",
    )(mla_t, s5_tm, *rwkv, x, w['wo_m'], w['wo_s'], w['wo_r'], wr['gng'], wr['gnb'], wr['ones'], w['ln1g'],
      w['ln1b'])


def _mlp_kernel(x_ref, wu_ref, wd_ref, g_ref, b_ref, o_ref, *, alpha, fc):
    x = x_ref[0]
    xb = x.astype(BF16)
    acc = alpha * x
    for c in range(wu_ref.shape[1] // fc):
        h = jnp.maximum(jnp.dot(xb, wu_ref[:, c * fc:(c + 1) * fc], preferred_element_type=F32), 0.0)
        acc = acc + jnp.dot((h * h).astype(BF16), wd_ref[c * fc:(c + 1) * fc, :], preferred_element_type=F32)
    o_ref[0] = _layer_norm(acc, g_ref[...], b_ref[...])


def _mlp(x, w, alpha):
    B, T, D = x.shape
    tt = _tile(T, 512)
    row = lambda b, i: (b, i, 0)
    return pl.pallas_call(
        functools.partial(_mlp_kernel, alpha=alpha, fc=1024),
        grid=(B, T // tt),
        in_specs=[pl.BlockSpec((1, tt, D), row), _const_spec(w['wup'].shape), _const_spec(w['wdown'].shape),
                  _const_spec((1, D)), _const_spec((1, D))],
        out_specs=pl.BlockSpec((1, tt, D), row),
        out_shape=jax.ShapeDtypeStruct((B, T, D), F32),
        compiler_params=_cparams("parallel", "parallel"),
        name="mlp_ln",
    )(x, w['wup'], w['wdown'], w['ln2g'], w['ln2b'])


def _rot_cols(wr):
    half = ROPE_DIM // 2
    return jnp.concatenate([-wr[..., half:], wr[..., :half]], axis=-1)


def _prep_layer(p):
    D = p['w_in'].shape[0]
    w_in = p['w_in']
    o_kv = Q_RANK
    o_kr = Q_RANK + KV_RANK
    o_u = o_kr + ROPE_DIM
    o_p = o_u + S5_W
    w_kr = w_in[:, o_kr:o_u]
    wm = jnp.concatenate([w_in[:, :o_kr], w_kr, _rot_cols(w_kr),
                          jnp.zeros((D, V7X_LANES - 2 * ROPE_DIM), F32)], axis=1)
    wqb = p['w_qb'].reshape(Q_RANK, MLA_HEADS, NOPE_DIM + ROPE_DIM) * (MLA_SCALE * LOG2_E)
    zq = jnp.zeros((Q_RANK, MLA_HEADS, HEAD_PAD - NOPE_DIM - ROPE_DIM), F32)
    wq1 = jnp.concatenate([wqb, zq], axis=-1).reshape(Q_RANK, MLA_HEADS * HEAD_PAD)
    wq2 = jnp.concatenate([jnp.zeros((Q_RANK, MLA_HEADS, NOPE_DIM), F32), _rot_cols(wqb[..., NOPE_DIM:]), zq],
                          axis=-1).reshape(Q_RANK, MLA_HEADS * HEAD_PAD)
    wkvb = p['w_kvb'].reshape(KV_RANK, MLA_HEADS, NOPE_DIM + V_DIM)
    wk = jnp.concatenate([wkvb[..., :NOPE_DIM], jnp.zeros((KV_RANK, MLA_HEADS, HEAD_PAD - NOPE_DIM), F32)],
                         axis=-1).reshape(KV_RANK, MLA_HEADS * HEAD_PAD)
    pk = jnp.concatenate([jnp.zeros((ROPE_DIM, NOPE_DIM), F32), jnp.eye(ROPE_DIM, dtype=F32),
                          jnp.zeros((ROPE_DIM, HEAD_PAD - NOPE_DIM - ROPE_DIM), F32)], axis=1)
    pk = jnp.tile(pk, (1, MLA_HEADS))
    wvt = wkvb[..., NOPE_DIM:].reshape(KV_RANK, MLA_HEADS * V_DIM).T

    lr, li = p['lam_re'], p['lam_im']
    dt = jnp.exp(p['log_dt'])[:, None]
    mag = jnp.exp(lr * dt)
    ar, ai = mag * jnp.cos(li * dt), mag * jnp.sin(li * dt)
    den = lr * lr + li * li
    cr = ((ar - 1.0) * lr + ai * li) / den
    ci = (ai * lr - (ar - 1.0) * li) / den
    bbr = cr[..., None] * p['b_re'] - ci[..., None] * p['b_im']
    bbi = cr[..., None] * p['b_im'] + ci[..., None] * p['b_re']
    eye_g = jnp.eye(S5_GROUPS, dtype=F32)
    bd_in = lambda m: (m.transpose(0, 2, 1)[:, :, None, :] * eye_g[:, None, :, None]).reshape(S5_W, S5_N)
    bd_out = lambda m: (m.transpose(0, 2, 1)[:, :, None, :] * eye_g[:, None, :, None]).reshape(S5_N, S5_W)
    wb = jnp.concatenate([bd_in(bbr), bd_in(bbi)], axis=1)
    wc = jnp.concatenate([bd_out(p['c_re']), -bd_out(p['c_im'])], axis=0)
    a = jnp.stack([ar.reshape(S5_N), ai.reshape(S5_N)])

    def lora_pad(w, off):
        z = jnp.zeros((V7X_LANES, RWKV_W), F32)
        return z.at[off:off + w.shape[0]].set(w)

    seg = jnp.arange(RWKV_W) // RWKV_HEAD
    ones = (seg[:, None] == seg[None, :]).astype(F32)
    row = lambda v: v.reshape(1, -1).astype(F32)
    wo = p['w_out']
    return dict(
        inproj=dict(wm=wm.astype(BF16), wu=w_in[:, o_u:o_p].astype(BF16), wp=w_in[:, o_p:].astype(BF16),
                    wq1=wq1.astype(BF16), wq2=wq2.astype(BF16), qg=row(p['q_norm_g']), kg=row(p['kv_norm_g'])),
        kvup=dict(wk=wk.astype(BF16), pk=pk.astype(BF16), wvt=wvt.astype(BF16)),
        s5=dict(a=a, wb=wb.astype(BF16), wc=wc.astype(BF16), d=row(p['s5_d']), wglu=p['w_glu'].astype(BF16),
                bglu=row(p['b_glu'])),
        rwkv=dict(mu=row(p['mu_shift']), ww=lora_pad(p['w_w2'], 0).astype(BF16),
                  wa=lora_pad(p['w_a2'], DECAY_LORA).astype(BF16),
                  wg=lora_pad(p['w_g2'], DECAY_LORA + AAA_LORA).astype(BF16),
                  w0=row(p['w0']), a0=row(p['a0']), kk=row(p['k_k']), ka=row(p['k_a']), rk=row(p['r_k']),
                  gng=row(p['gn_g']), gnb=row(p['gn_b']), ones=ones.astype(BF16)),
        out=dict(wo_m=wo[:MLA_HEADS * V_DIM].astype(BF16),
                 wo_s=wo[MLA_HEADS * V_DIM:MLA_HEADS * V_DIM + S5_W].astype(BF16),
                 wo_r=wo[MLA_HEADS * V_DIM + S5_W:].astype(BF16), ln1g=row(p['ln1_g']), ln1b=row(p['ln1_b'])),
        mlp=dict(wup=p['w_up'].astype(BF16), wdown=p['w_down'].astype(BF16), ln2g=row(p['ln2_g']),
                 ln2b=row(p['ln2_b'])),
    )


def _rope_tables(pos):
    inv_freq = ROPE_BASE ** (-jnp.arange(0, ROPE_DIM, 2, dtype=F32) / ROPE_DIM)
    ang = pos.astype(F32)[:, None] * inv_freq[None, :]
    ang = jnp.concatenate([ang, ang], -1)
    cos, sin = jnp.cos(ang), jnp.sin(ang)
    T = pos.shape[0]
    z = lambda n: jnp.zeros((T, n), F32)
    return dict(
        csk=jnp.concatenate([cos, sin, z(V7X_LANES - 2 * ROPE_DIM)], axis=1),
        cq=jnp.concatenate([jnp.ones((T, NOPE_DIM), F32), cos, z(HEAD_PAD - NOPE_DIM - ROPE_DIM)], axis=1),
        sq=jnp.concatenate([z(NOPE_DIM), sin, z(HEAD_PAD - NOPE_DIM - ROPE_DIM)], axis=1),
    )


def _trunk_layer(x, tabs, past, s5_state, rwkv_state, shift0, w, alpha):
    B, T, D = x.shape
    qcat, ckv, krope, u_tm, p_tm = _inproj(x, w['inproj'], tabs)
    kcat, vt = _kvup(ckv, krope, w['kvup'])
    if past is None:
        mla = _attention(qcat, kcat, vt, kv_len=T, chunk_causal=True)
    else:
        kv_len = past[0].shape[1] + T
        bk = past[1].shape[4]
        kcat = jnp.concatenate([past[0], kcat], axis=1)
        kcat = jnp.pad(kcat, ((0, 0), (0, -kv_len % bk), (0, 0)))
        vt = jnp.pad(vt, ((0, 0), (0, 0), (0, 0), (0, 0), (0, bk - vt.shape[4])))
        vt = jnp.concatenate([past[1], vt], axis=2)
        mla = _attention(qcat, kcat, vt, kv_len=kv_len, chunk_causal=False)

    x0 = jnp.concatenate([s5_state[..., 0].reshape(B, S5_N), s5_state[..., 1].reshape(B, S5_N)], axis=1)
    s5_out, x_last = _s5(u_tm, x0, w['s5'])
    new_s5 = jnp.stack([x_last[:, :S5_N].reshape(B, S5_GROUPS, S5_STATE),
                        x_last[:, S5_N:].reshape(B, S5_GROUPS, S5_STATE)], axis=-1)

    rwkv, new_rwkv, shift = _rwkv_mixer(p_tm, shift0.reshape(B, RWKV_PROJ), rwkv_state, w['rwkv'])
    x = _outproj(mla, s5_out, rwkv, x, w['out'], w['rwkv'], alpha)
    x = _mlp(x, w['mlp'], alpha)
    return x, ckv, krope, new_s5, new_rwkv, shift.reshape(B, 1, RWKV_PROJ)


def kernel(x_prompt, x_sample, cache_mla_ckv, cache_mla_krope, state_s5, state_rwkv, state_rwkv_shift, w_in, q_norm_g, w_qb, kv_norm_g, w_kvb, lam_re, lam_im, log_dt, b_re, b_im, c_re, c_im, s5_d, w_glu, b_glu, mu_shift, w0, w_w2, a0, w_a2, w_g2, k_k, k_a, r_k, gn_g, gn_b, w_out, ln1_g, ln1_b, w_up, w_down, ln2_g, ln2_b):
    depth = w_in.shape[0]
    alpha = (2 * depth) ** 0.25
    bp, sp = x_prompt.shape[:2]
    ts = x_sample.shape[1]
    past = cache_mla_ckv.shape[2]
    tabs_p = _rope_tables(jnp.arange(sp))
    tabs_s = _rope_tables(past + jnp.arange(ts))
    names = dict(w_in=w_in, q_norm_g=q_norm_g, w_qb=w_qb, kv_norm_g=kv_norm_g, w_kvb=w_kvb, lam_re=lam_re,
                 lam_im=lam_im, log_dt=log_dt, b_re=b_re, b_im=b_im, c_re=c_re, c_im=c_im, s5_d=s5_d, w_glu=w_glu,
                 b_glu=b_glu, mu_shift=mu_shift, w0=w0, w_w2=w_w2, a0=a0, w_a2=w_a2, w_g2=w_g2, k_k=k_k, k_a=k_a,
                 r_k=r_k, gn_g=gn_g, gn_b=gn_b, w_out=w_out, ln1_g=ln1_g, ln1_b=ln1_b, w_up=w_up, w_down=w_down,
                 ln2_g=ln2_g, ln2_b=ln2_b)
    s5_zero = jnp.zeros((bp, S5_GROUPS, S5_STATE, 2), F32)
    rwkv_zero = jnp.zeros((bp, RWKV_HEADS, RWKV_HEAD, RWKV_HEAD), F32)
    shift_zero = jnp.zeros((bp, 1, RWKV_PROJ), F32)

    xp, xs = x_prompt, x_sample
    outs_p = [[] for _ in range(5)]
    outs_s = [[] for _ in range(5)]
    for l in range(depth):
        w = _prep_layer({k: v[l] for k, v in names.items()})
        xp, *rest = _trunk_layer(xp, tabs_p, None, s5_zero, rwkv_zero, shift_zero, w, alpha)
        for acc, val in zip(outs_p, rest):
            acc.append(val)
        past_kv = _kvup(cache_mla_ckv[l], cache_mla_krope[l], w['kvup'])
        xs, *rest = _trunk_layer(xs, tabs_s, past_kv, state_s5[l], state_rwkv[l], state_rwkv_shift[l], w, alpha)
        for acc, val in zip(outs_s, rest):
            acc.append(val)
    return (xp, xs, *[jnp.stack(a) for a in outs_p], *[jnp.stack(a) for a in outs_s])
```

```python
import functools
import math

import jax
import jax.numpy as jnp
from jax import lax
from jax.experimental import pallas as pl
from jax.experimental.pallas import tpu as pltpu

F32 = jnp.float32
BF16 = jnp.bfloat16

CHUNK = 64
MLA_HEADS = 6
NOPE_DIM = 64
ROPE_DIM = 32
V_DIM = 64
Q_RANK = 256
KV_RANK = 128
ROPE_BASE = 10000.0
MLA_SCALE = (NOPE_DIM + ROPE_DIM) ** -0.5
S5_GROUP_CH = 16
S5_W = 256
S5_GROUPS = S5_W // S5_GROUP_CH
S5_STATE = 64
S5_N = S5_GROUPS * S5_STATE
RWKV_HEADS = 6
RWKV_HEAD = 64
RWKV_W = RWKV_HEADS * RWKV_HEAD
DECAY_LORA = 32
AAA_LORA = 32
GATE_LORA = 64
RWKV_PROJ = 3 * RWKV_W + DECAY_LORA + AAA_LORA + GATE_LORA
LN_EPS = 1e-5
RMS_EPS = 1e-6
GN_EPS = 64e-5
NEG_INF = -1e30

V7X_LANES = 128
V7X_VMEM_BYTES = 64 * 1024 * 1024
VMEM_LIMIT_BYTES = V7X_VMEM_BYTES - 8 * 1024 * 1024
HEAD_PAD = V7X_LANES
ATTN_BLOCK = 512
ROW_TILE = 128
ATTN_HEAD_GROUP = 3
V_ROWS = V_DIM + 16
LOG2_E = 1.4426950408889634
RWKV_PACK = 8


def _cparams(*sem):
    return pltpu.CompilerParams(dimension_semantics=sem, vmem_limit_bytes=VMEM_LIMIT_BYTES)


def _const_spec(shape):
    zeros = (0,) * len(shape)
    return pl.BlockSpec(shape, lambda *_: zeros)


def _tile(n, target):
    if n <= target:
        return n
    t = target
    while n % t:
        t //= 2
    return t


def _layer_norm(y, g, b):
    mu = jnp.mean(y, -1, keepdims=True)
    yc = y - mu
    var = jnp.mean(yc * yc, -1, keepdims=True)
    return yc * lax.rsqrt(var + LN_EPS) * g + b


def _inproj_kernel(x_ref, wm_ref, wu_ref, wp_ref, wq1_ref, wq2_ref, qg_ref, kg_ref, csk_ref, cq_ref, sq_ref,
                   qcat_ref, ckv_ref, krope_ref, u_ref, p_ref):
    B, tt, D = x_ref.shape
    x = x_ref[...].reshape(B * tt, D).astype(BF16)
    per_b = lambda tab_ref: jnp.concatenate([tab_ref[...]] * B, axis=0)
    m = jnp.dot(x, wm_ref[...], preferred_element_type=F32)
    q_lat = m[:, :Q_RANK]
    qn = q_lat * lax.rsqrt(jnp.mean(q_lat * q_lat, -1, keepdims=True) + RMS_EPS) * qg_ref[...]
    kv_lat = m[:, Q_RANK:Q_RANK + KV_RANK]
    ckv = kv_lat * lax.rsqrt(jnp.mean(kv_lat * kv_lat, -1, keepdims=True) + RMS_EPS) * kg_ref[...]
    ckv_ref[...] = ckv.reshape(B, tt, KV_RANK)
    kr = m[:, Q_RANK + KV_RANK:] * per_b(csk_ref)
    kr = kr + pltpu.roll(kr, V7X_LANES - ROPE_DIM, 1)
    krope_ref[...] = kr[:, :ROPE_DIM].reshape(B, tt, ROPE_DIM)
    qb = qn.astype(BF16)
    q1 = jnp.dot(qb, wq1_ref[...], preferred_element_type=F32)
    q2 = jnp.dot(qb, wq2_ref[...], preferred_element_type=F32)
    cq = per_b(cq_ref)
    sq = per_b(sq_ref)
    for h in range(MLA_HEADS):
        sl = slice(h * HEAD_PAD, (h + 1) * HEAD_PAD)
        qcat_ref[:, :, sl] = (q1[:, sl] * cq + q2[:, sl] * sq).astype(BF16).reshape(B, tt, HEAD_PAD)
    u = jnp.dot(x, wu_ref[...], preferred_element_type=F32)
    p = jnp.dot(x, wp_ref[...], preferred_element_type=F32)
    for b in range(B):
        u_ref[:, b, :] = u[b * tt:(b + 1) * tt]
        p_ref[:, b, :] = p[b * tt:(b + 1) * tt]


def _inproj(x, w, tabs):
    B, T, D = x.shape
    tt = _tile(T, ROW_TILE)
    nq = MLA_HEADS * HEAD_PAD
    row = lambda i: (0, i, 0)
    tab = lambda i: (i, 0)
    tm = lambda i: (i, 0, 0)
    return pl.pallas_call(
        _inproj_kernel,
        grid=(T // tt,),
        in_specs=[
            pl.BlockSpec((B, tt, D), row),
            _const_spec(w['wm'].shape), _const_spec(w['wu'].shape), _const_spec(w['wp'].shape),
            _const_spec(w['wq1'].shape), _const_spec(w['wq2'].shape),
            _const_spec((1, Q_RANK)), _const_spec((1, KV_RANK)),
            pl.BlockSpec((tt, V7X_LANES), tab), pl.BlockSpec((tt, HEAD_PAD), tab), pl.BlockSpec((tt, HEAD_PAD), tab),
        ],
        out_specs=[
            pl.BlockSpec((B, tt, nq), row),
            pl.BlockSpec((B, tt, KV_RANK), row),
            pl.BlockSpec((B, tt, ROPE_DIM), row),
            pl.BlockSpec((tt, B, S5_W), tm),
            pl.BlockSpec((tt, B, RWKV_PROJ), tm),
        ],
        out_shape=[
            jax.ShapeDtypeStruct((B, T, nq), BF16),
            jax.ShapeDtypeStruct((B, T, KV_RANK), F32),
            jax.ShapeDtypeStruct((B, T, ROPE_DIM), F32),
            jax.ShapeDtypeStruct((T, B, S5_W), F32),
            jax.ShapeDtypeStruct((T, B, RWKV_PROJ), F32),
        ],
        compiler_params=_cparams("parallel"),
        name="inproj",
    )(x, w['wm'], w['wu'], w['wp'], w['wq1'], w['wq2'], w['qg'], w['kg'], tabs['csk'], tabs['cq'], tabs['sq'])


def _kvup_kernel(ckv_ref, kr_ref, wk_ref, pk_ref, wvt_ref, kcat_ref, vt_ref):
    c = ckv_ref[0].astype(BF16)
    kr = kr_ref[0].astype(BF16)
    kcat = jnp.dot(c, wk_ref[...], preferred_element_type=F32) + jnp.dot(kr, pk_ref[...], preferred_element_type=F32)
    kcat_ref[0] = kcat.astype(BF16)
    vt = lax.dot_general(wvt_ref[...], c, (((1,), (1,)), ((), ())), preferred_element_type=F32).astype(BF16)
    ones = jnp.ones((V_ROWS - V_DIM, vt.shape[1]), BF16)
    for h in range(MLA_HEADS):
        vt_ref[0, h, 0, :V_DIM, :] = vt[h * V_DIM:(h + 1) * V_DIM, :]
        vt_ref[0, h, 0, V_DIM:, :] = ones


def _kvup(ckv, krope, w):
    B, T, _ = ckv.shape
    tt = _tile(T, ATTN_BLOCK)
    nk = MLA_HEADS * HEAD_PAD
    row = lambda b, i: (b, i, 0)
    return pl.pallas_call(
        _kvup_kernel,
        grid=(B, T // tt),
        in_specs=[pl.BlockSpec((1, tt, KV_RANK), row), pl.BlockSpec((1, tt, ROPE_DIM), row),
                  _const_spec(w['wk'].shape), _const_spec(w['pk'].shape), _const_spec(w['wvt'].shape)],
        out_specs=[pl.BlockSpec((1, tt, nk), row),
                   pl.BlockSpec((1, MLA_HEADS, 1, V_ROWS, tt), lambda b, i: (b, 0, i, 0, 0))],
        out_shape=[jax.ShapeDtypeStruct((B, T, nk), BF16),
                   jax.ShapeDtypeStruct((B, MLA_HEADS, T // tt, V_ROWS, tt), BF16)],
        compiler_params=_cparams("parallel", "parallel"),
        name="kvup",
    )(ckv, krope, w['wk'], w['pk'], w['wvt'])


def _attn_kernel(q_ref, k_ref, v_ref, o_ref, sa_ref, sb_ref, bias_ref, m_ref, acc_ref, *, bq, bk, kv_len,
                 chunk_causal):
    hg = ATTN_HEAD_GROUP
    qi = pl.program_id(2)
    last = qi if chunk_causal else jnp.int32(-(-kv_len // bk) - 1)
    nt = (((1,), (1,)), ((), ()))

    def scores(j, s_ref, h):
        r0 = pl.multiple_of(j * bk, bk)
        sl = slice(h * HEAD_PAD, (h + 1) * HEAD_PAD)
        s_ref[h] = lax.dot_general(k_ref[0, pl.ds(r0, bk), sl], q_ref[0, :, sl], nt,
                                   preferred_element_type=F32)

    def softmax_pv(j, s_ref, h, masked):
        s = s_ref[h]
        if masked:
            s = s + bias_ref[...]
        m = m_ref[h]
        m_new = jnp.maximum(m, jnp.max(s, axis=0, keepdims=True))
        p = jnp.exp2(s - m_new)
        alpha = jnp.exp2(m - m_new)
        acc_ref[h] = alpha * acc_ref[h] + jnp.dot(v_ref[0, h, j], p.astype(BF16),
                                                  preferred_element_type=F32)
        m_ref[h] = m_new

    @pl.when(qi == 0)
    def _():
        krel = lax.broadcasted_iota(jnp.int32, (bk, bq), 0)
        if chunk_causal:
            qrel = lax.broadcasted_iota(jnp.int32, (bk, bq), 1)
            ok = (krel // CHUNK) <= (qrel // CHUNK)
        else:
            ok = krel < kv_len - (-(-kv_len // bk) - 1) * bk
        bias_ref[...] = jnp.where(ok, 0.0, NEG_INF)

    def advance(j_next, s_next, j, s_cur, masked=False):
        for h in range(hg):
            if j_next is not None:
                scores(j_next, s_next, h)
            softmax_pv(j, s_cur, h, masked)

    m_ref[...] = jnp.full(m_ref.shape, NEG_INF, F32)
    acc_ref[...] = jnp.zeros(acc_ref.shape, F32)
    for h in range(hg):
        scores(0, sa_ref, h)

    def pair(p, _):
        j = 2 * p
        advance(j + 1, sb_ref, j, sa_ref)
        advance(j + 2, sa_ref, j + 1, sb_ref)
        return 0

    lax.fori_loop(0, last // 2, pair, 0)

    @pl.when(last % 2 == 0)
    def _():
        advance(None, None, last, sa_ref, True)

    @pl.when(last % 2 == 1)
    def _():
        advance(last, sb_ref, last - 1, sa_ref)
        advance(None, None, last, sb_ref, True)

    for h in range(hg):
        o_ref[0, h] = (acc_ref[h, :V_DIM, :] / acc_ref[h, V_DIM:V_DIM + 1, :]).astype(BF16)


def _attention(qcat, kcat, vt, *, kv_len, chunk_causal):
    B, Tq, _ = qcat.shape
    Tk = kcat.shape[1]
    nblk, bk = vt.shape[2], vt.shape[4]
    assert nblk * bk == Tk
    bq = bk if chunk_causal else Tq
    hg = ATTN_HEAD_GROUP
    out_t = pl.pallas_call(
        functools.partial(_attn_kernel, bq=bq, bk=bk, kv_len=kv_len, chunk_causal=chunk_causal),
        grid=(B, MLA_HEADS // hg, Tq // bq),
        in_specs=[
            pl.BlockSpec((1, bq, hg * HEAD_PAD), lambda b, g, i: (b, i, g)),
            pl.BlockSpec((1, Tk, hg * HEAD_PAD), lambda b, g, i: (b, 0, g)),
            pl.BlockSpec((1, hg, nblk, V_ROWS, bk), lambda b, g, i: (b, g, 0, 0, 0)),
        ],
        out_specs=pl.BlockSpec((1, hg, V_DIM, bq), lambda b, g, i: (b, g, 0, i)),
        out_shape=jax.ShapeDtypeStruct((B, MLA_HEADS, V_DIM, Tq), BF16),
        scratch_shapes=[pltpu.VMEM((hg, bk, bq), F32), pltpu.VMEM((hg, bk, bq), F32), pltpu.VMEM((bk, bq), F32),
                        pltpu.VMEM((hg, 1, bq), F32), pltpu.VMEM((hg, V_ROWS, bq), F32)],
        compiler_params=_cparams("parallel", "parallel", "arbitrary"),
        name="mla_attention",
    )(qcat, kcat, vt)
    return out_t


def _gelu_tanh(x):
    return 0.5 * x * (1.0 + jnp.tanh(math.sqrt(2.0 / math.pi) * (x + 0.044715 * (x * x * x))))


def _s5_kernel(u_ref, x0_ref, a_ref, wb_ref, wc_ref, d_ref, wg_ref, bg_ref, o_ref, xl_ref, xs_ref, st_ref, *, L, B):
    @pl.when(pl.program_id(0) == 0)
    def _():
        st_ref[...] = x0_ref[...]

    u = u_ref[...].reshape(L * B, S5_W)
    xs_ref[...] = jnp.dot(u.astype(BF16), wb_ref[...], preferred_element_type=F32)
    a_re = a_ref[0:1, :]
    a_im = a_ref[1:2, :]

    def body(t, carry):
        x_re, x_im = carry
        r0 = pl.multiple_of(t * B, B)
        n_re = a_re * x_re - a_im * x_im + xs_ref[pl.ds(r0, B), :S5_N]
        n_im = a_re * x_im + a_im * x_re + xs_ref[pl.ds(r0, B), S5_N:]
        xs_ref[pl.ds(r0, B), :S5_N] = n_re
        xs_ref[pl.ds(r0, B), S5_N:] = n_im
        return n_re, n_im

    x_re, x_im = lax.fori_loop(0, L, body, (st_ref[:, :S5_N], st_ref[:, S5_N:]), unroll=8)
    st_ref[:, :S5_N] = x_re
    st_ref[:, S5_N:] = x_im
    xl_ref[...] = st_ref[...]

    y = jnp.dot(xs_ref[...].astype(BF16), wc_ref[...], preferred_element_type=F32) + d_ref[...] * u
    z = _gelu_tanh(y)
    gate = jax.nn.sigmoid(jnp.dot(z.astype(BF16), wg_ref[...], preferred_element_type=F32) + bg_ref[...])
    o_ref[...] = (z * gate).reshape(L, B, S5_W)


def _s5(u_tm, x0, w):
    T, B, _ = u_tm.shape
    L = _tile(T, 1024 // B)
    return pl.pallas_call(
        functools.partial(_s5_kernel, L=L, B=B),
        grid=(T // L,),
        in_specs=[
            pl.BlockSpec((L, B, S5_W), lambda i: (i, 0, 0)),
            _const_spec((B, 2 * S5_N)), _const_spec((2, S5_N)),
            _const_spec((S5_W, 2 * S5_N)), _const_spec((2 * S5_N, S5_W)),
            _const_spec((1, S5_W)), _const_spec((S5_W, S5_W)), _const_spec((1, S5_W)),
        ],
        out_specs=[pl.BlockSpec((L, B, S5_W), lambda i: (i, 0, 0)), _const_spec((B, 2 * S5_N))],
        out_shape=[jax.ShapeDtypeStruct((T, B, S5_W), F32), jax.ShapeDtypeStruct((B, 2 * S5_N), F32)],
        scratch_shapes=[pltpu.VMEM((L * B, 2 * S5_N), F32), pltpu.VMEM((B, 2 * S5_N), F32)],
        compiler_params=_cparams("arbitrary"),
        name="s5_scan",
    )(u_tm, x0, w['a'], w['wb'], w['wc'], w['d'], w['wglu'], w['bglu'])


def _softplus(x):
    return jnp.maximum(x, 0.0) + jnp.log(1.0 + jnp.exp(-jnp.abs(x)))


def _head_sum(x, ones_ref):
    hi = x.astype(BF16)
    lo = (x - hi.astype(F32)).astype(BF16)
    ones = ones_ref[...]
    return jnp.dot(hi, ones, preferred_element_type=F32) + jnp.dot(lo, ones, preferred_element_type=F32)


def _pack_rows(x, o_ref, zs_ref, B):
    rows = RWKV_PACK * B
    for m in range(x.shape[0] // rows):
        xm = x[m * rows:(m + 1) * rows]
        for h in range(RWKV_HEADS):
            c = h * RWKV_HEAD // V7X_LANES
            piece = xm[:, c * V7X_LANES:(c + 1) * V7X_LANES]
            if (h * RWKV_HEAD) % V7X_LANES:
                piece = pltpu.roll(piece, V7X_LANES - (h * RWKV_HEAD) % V7X_LANES, 1)
            zs_ref[pl.ds(h, rows, stride=RWKV_HEADS), :] = piece
        o_ref[m] = zs_ref[...].T[:RWKV_HEAD, :]


def _rwkv_prep_kernel(p_ref, sh0_ref, mu_ref, ww_ref, wa_ref, wg_ref, w0_ref, a0_ref, ka_ref, rk_ref,
                      ones_ref, r_o, w_o, k_o, v_o, a_o, g_o, bon_o, sh_o, last_ref, zs_ref, *, L, B):
    @pl.when(pl.program_id(0) == 0)
    def _():
        last_ref[...] = sh0_ref[...]

    p = p_ref[...].reshape(L * B, RWKV_PROJ)
    if L > 1:
        prev = jnp.concatenate([last_ref[...], p[:(L - 1) * B]], axis=0)
    else:
        prev = last_ref[...]
    last_ref[...] = p[(L - 1) * B:]
    sh_o[...] = p[(L - 1) * B:]
    ps = p + (prev - p) * mu_ref[...]
    r = ps[:, :RWKV_W]
    k = ps[:, RWKV_W:2 * RWKV_W]
    v = ps[:, 2 * RWKV_W:3 * RWKV_W]
    tail = ps[:, 3 * RWKV_W:]
    lw = jnp.dot(jnp.tanh(tail).astype(BF16), ww_ref[...], preferred_element_type=F32)
    la = jnp.dot(tail.astype(BF16), wa_ref[...], preferred_element_type=F32)
    g = jnp.dot(jax.nn.sigmoid(tail).astype(BF16), wg_ref[...], preferred_element_type=F32)
    w_log = -_softplus(-(w0_ref[...] + lw)) - 0.5
    decay = jnp.exp(-jnp.exp(w_log))
    a = jax.nn.sigmoid(a0_ref[...] + la)
    k2 = k * (1.0 + (a - 1.0) * ka_ref[...])
    bonus = _head_sum(r * k2 * rk_ref[...], ones_ref) * v
    for val, o_ref in ((r, r_o), (decay, w_o), (k, k_o), (v, v_o), (a, a_o)):
        _pack_rows(val, o_ref, zs_ref, B)
    shp = (L, B, RWKV_W)
    g_o[...] = g.reshape(shp)
    bon_o[...] = bonus.reshape(shp)


def _rwkv_prep(p_tm, shift0, w):
    T, B, _ = p_tm.shape
    L = _tile(T, 512 // B)
    assert L % RWKV_PACK == 0
    blk = pl.BlockSpec((L, B, RWKV_W), lambda i: (i, 0, 0))
    W = RWKV_PACK * B * RWKV_HEADS
    pblk = pl.BlockSpec((L // RWKV_PACK, RWKV_HEAD, W), lambda i: (i, 0, 0))
    packed = jax.ShapeDtypeStruct((T // RWKV_PACK, RWKV_HEAD, W), F32)
    vec = _const_spec((1, RWKV_W))
    outs = pl.pallas_call(
        functools.partial(_rwkv_prep_kernel, L=L, B=B),
        grid=(T // L,),
        in_specs=[pl.BlockSpec((L, B, RWKV_PROJ), lambda i: (i, 0, 0)), _const_spec((B, RWKV_PROJ)),
                  _const_spec((1, RWKV_PROJ)),
                  _const_spec((V7X_LANES, RWKV_W)), _const_spec((V7X_LANES, RWKV_W)), _const_spec((V7X_LANES, RWKV_W)),
                  vec, vec, vec, vec, _const_spec((RWKV_W, RWKV_W))],
        out_specs=[pblk] * 5 + [blk] * 2 + [_const_spec((B, RWKV_PROJ))],
        out_shape=[packed] * 5 + [jax.ShapeDtypeStruct((T, B, RWKV_W), F32)] * 2
        + [jax.ShapeDtypeStruct((B, RWKV_PROJ), F32)],
        scratch_shapes=[pltpu.VMEM((B, RWKV_PROJ), F32), pltpu.VMEM((RWKV_PACK * B * RWKV_HEADS, V7X_LANES), F32)],
        compiler_params=_cparams("arbitrary"),
        name="rwkv_prep",
    )(p_tm, shift0, w['mu'], w['ww'], w['wa'], w['wg'], w['w0'], w['a0'], w['ka'], w['rk'], w['ones'])
    return outs


def _lane_window(tile, off, width, dst, lane):
    v0, lo = divmod(off, V7X_LANES)
    shift = (dst - lo) % V7X_LANES
    a = tile(v0)
    r = pltpu.roll(a, shift, 1) if shift else a
    if lo + width > V7X_LANES:
        b = tile(v0 + 1)
        r2 = pltpu.roll(b, shift, 1) if shift else b
        r = jnp.where(lane < dst + (V7X_LANES - lo), r, r2)
    return r


def _chain_lanes(pieces, bh, lane):
    rep = len(pieces)
    out = jnp.where(lane < rep * bh, pieces[rep - 1], 0.0)
    for r in range(rep - 2, -1, -1):
        out = jnp.where(lane < (r + 1) * bh, pieces[r], out)
    return out


def _rwkv_scan_kernel(w_ref, a_ref, k_ref, kx_ref, r_ref, v_ref, kkt_ref, kat_ref, s0_ref, y_ref, so_ref,
                      s_ref, sa_ref, e_ref, ev_ref, ys_ref, kall_ref, *, Tm, IL, bh, rep):
    lane = lax.broadcasted_iota(jnp.int32, (RWKV_HEAD, V7X_LANES), 1)
    lane_il = lax.broadcasted_iota(jnp.int32, (IL, V7X_LANES), 1)
    ncol = RWKV_PACK * bh // V7X_LANES

    def tiles(slab, rows=slice(None)):
        return lambda c: slab[rows, c * V7X_LANES:(c + 1) * V7X_LANES]

    def expand(slab, t8):
        return _chain_lanes([_lane_window(tiles(slab), t8 * bh, bh, r * bh, lane) for r in range(rep)], bh, lane)

    def unit_kk(k_raw):
        kk = k_raw * kkt_ref[...]
        nrm = jnp.sqrt(jnp.sum(kk * kk, axis=0, keepdims=True))
        return kk / jnp.maximum(nrm, 1e-12)

    kall_ref[0:Tm] = k_ref[...]
    kall_ref[Tm] = kx_ref[0]

    @pl.when(pl.program_id(0) == 0)
    def _():
        s_ref[...] = s0_ref[...]
        e_ref[0, 4] = expand(-unit_kk(k_ref[0]), 0)
        acc = jnp.zeros((IL, V7X_LANES), F32)
        for j in range(RWKV_HEAD):
            acc = acc + s0_ref[j] * e_ref[0, 4, j:j + 1, :]
        sa_ref[...] = acc

    def body(m, sa):
        k_raw = kall_ref[m]
        a_s = a_ref[m]
        kkn = unit_kk(k_raw)
        slabs = [w_ref[m], kkn * a_s, k_raw * (1.0 + (a_s - 1.0) * kat_ref[...]), r_ref[m]]
        nn0 = -kkn
        nn1 = -unit_kk(kall_ref[m + 1])
        vs = v_ref[m]
        for t8 in range(RWKV_PACK):
            for o, slab in enumerate(slabs):
                e_ref[t8, o] = expand(slab, t8)
            e_ref[t8, 4] = expand(nn0, t8 + 1) if t8 + 1 < RWKV_PACK else expand(nn1, 0)
            ev_ref[t8] = _chain_lanes([_lane_window(tiles(vs, slice(r * IL, (r + 1) * IL)), t8 * bh, bh, r * bh,
                                                    lane_il) for r in range(rep)], bh, lane_il)
        for t8 in range(RWKV_PACK):
            vt = ev_ref[t8]
            yacc = jnp.zeros((IL, V7X_LANES), F32)
            san = jnp.zeros((IL, V7X_LANES), F32)
            for j in range(RWKV_HEAD):
                row = lambda o: e_ref[t8, o, j:j + 1, :]
                sn = s_ref[j] * row(0) + sa * row(1) + vt * row(2)
                s_ref[j] = sn
                yacc = yacc + sn * row(3)
                san = san + sn * row(4)
            sa = san
            ys_ref[t8] = yacc
        for c in range(ncol):
            for r in range(rep):
                col = jnp.zeros((IL, V7X_LANES), F32)
                for t8 in range(RWKV_PACK):
                    off = t8 * bh
                    lo, hi = max(off, c * V7X_LANES), min(off + bh, (c + 1) * V7X_LANES)
                    if lo < hi:
                        shift = (off - r * bh) % V7X_LANES
                        rolled = pltpu.roll(ys_ref[t8], shift, 1) if shift else ys_ref[t8]
                        inside = jnp.logical_and(lane_il >= lo - c * V7X_LANES, lane_il < hi - c * V7X_LANES)
                        col = jnp.where(inside, rolled, col)
                y_ref[m, r * IL:(r + 1) * IL, c * V7X_LANES:(c + 1) * V7X_LANES] = col
        return sa

    sa_ref[...] = lax.fori_loop(0, Tm, body, sa_ref[...])
    so_ref[...] = s_ref[...]


def _rwkv_scan(w_p, a_p, k_p, r_p, v_p, kk_t, ka_t, s0_l, bh, rep):
    TP, _, W = v_p.shape
    IL = RWKV_HEAD // rep
    Tm = _tile(TP, 8)
    blk = pl.BlockSpec((Tm, RWKV_HEAD, W), lambda i: (i, 0, 0))
    nxt = pl.BlockSpec((1, RWKV_HEAD, W), lambda i: (jnp.minimum((i + 1) * Tm, TP - 1), 0, 0))
    sblk = _const_spec((RWKV_HEAD, IL, V7X_LANES))
    par = _const_spec((RWKV_HEAD, W))
    return pl.pallas_call(
        functools.partial(_rwkv_scan_kernel, Tm=Tm, IL=IL, bh=bh, rep=rep),
        grid=(TP // Tm,),
        in_specs=[blk, blk, blk, nxt, blk, blk, par, par, sblk],
        out_specs=[blk, sblk],
        out_shape=[jax.ShapeDtypeStruct((TP, RWKV_HEAD, W), F32),
                   jax.ShapeDtypeStruct((RWKV_HEAD, IL, V7X_LANES), F32)],
        scratch_shapes=[pltpu.VMEM((RWKV_HEAD, IL, V7X_LANES), F32), pltpu.VMEM((IL, V7X_LANES), F32),
                        pltpu.VMEM((RWKV_PACK, 5, RWKV_HEAD, V7X_LANES), F32),
                        pltpu.VMEM((RWKV_PACK, IL, V7X_LANES), F32), pltpu.VMEM((RWKV_PACK, IL, V7X_LANES), F32),
                        pltpu.VMEM((Tm + 1, RWKV_HEAD, W), F32)],
        compiler_params=_cparams("arbitrary"),
        name="rwkv_scan",
    )(w_p, a_p, k_p, k_p, r_p, v_p, kk_t, ka_t, s0_l)


def _rwkv_layout(B):
    bh = B * RWKV_HEADS
    rep = 1
    while 2 * rep * bh <= V7X_LANES and RWKV_HEAD % (2 * rep) == 0:
        rep *= 2
    return bh, rep, RWKV_HEAD // rep


def _rwkv_mixer(p_tm, shift0, s0, w):
    T, B, _ = p_tm.shape
    bh, rep, IL = _rwkv_layout(B)
    lanes = rep * bh
    r_p, dec_p, k_p, v_p, a_p, g, bonus, shift = _rwkv_prep(p_tm, shift0, w)
    TP = T // RWKV_PACK

    def par_tile(vec):
        t = vec.reshape(RWKV_HEADS, RWKV_HEAD).T[:, None, None, :]
        return jnp.broadcast_to(t, (RWKV_HEAD, RWKV_PACK, B, RWKV_HEADS)).reshape(RWKV_HEAD, RWKV_PACK * bh)

    s0_l = s0.reshape(bh, rep, IL, RWKV_HEAD).transpose(3, 2, 1, 0).reshape(RWKV_HEAD, IL, lanes)
    s0_l = jnp.pad(s0_l, ((0, 0), (0, 0), (0, V7X_LANES - lanes)))
    y_p, s_l = _rwkv_scan(dec_p, a_p, k_p, r_p, v_p, par_tile(w['kk']), par_tile(w['ka']), s0_l, bh, rep)
    y_tm = y_p.reshape(TP, RWKV_HEAD, RWKV_PACK, bh).transpose(0, 2, 3, 1).reshape(T, B, RWKV_W)
    s_last = s_l[:, :, :lanes].reshape(RWKV_HEAD, IL, rep, bh).transpose(3, 2, 1, 0)
    s_last = s_last.reshape(B, RWKV_HEADS, RWKV_HEAD, RWKV_HEAD)
    return (y_tm, bonus, g), s_last, shift


def _outproj_kernel(m_ref, s_ref, y_ref, bon_ref, gate_ref, x_ref, wm_ref, ws_ref, wr_ref, gng_ref, gnb_ref,
                    ones_ref, g_ref, b_ref, o_ref, *, alpha):
    B, tt, D = x_ref.shape
    mla = jnp.concatenate([m_ref[b].reshape(MLA_HEADS * V_DIM, tt).T for b in range(B)], axis=0)
    rows = lambda ref: jnp.concatenate([ref[:, b, :] for b in range(B)], axis=0)
    acc = jnp.dot(mla, wm_ref[...], preferred_element_type=F32)
    acc = acc + jnp.dot(rows(s_ref).astype(BF16), ws_ref[...], preferred_element_type=F32)
    y = rows(y_ref)
    inv_n = 1.0 / RWKV_HEAD
    mu = _head_sum(y, ones_ref) * inv_n
    yc = y - mu
    var = _head_sum(yc * yc, ones_ref) * inv_n
    yn = yc * lax.rsqrt(var + GN_EPS) * gng_ref[...] + gnb_ref[...]
    rw = (yn + rows(bon_ref)) * rows(gate_ref)
    acc = acc + jnp.dot(rw.astype(BF16), wr_ref[...], preferred_element_type=F32)
    out = _layer_norm(alpha * x_ref[...].reshape(B * tt, D) + acc, g_ref[...], b_ref[...])
    o_ref[...] = out.reshape(B, tt, D)


def _outproj(mla_t, s5_tm, rwkv, x, w, wr, alpha):
    B, T, D = x.shape
    tt = _tile(T, ROW_TILE)
    row = lambda i: (0, i, 0)
    tm = lambda i: (i, 0, 0)
    rblk = pl.BlockSpec((tt, B, RWKV_W), tm)
    vec = _const_spec((1, RWKV_W))
    return pl.pallas_call(
        functools.partial(_outproj_kernel, alpha=alpha),
        grid=(T // tt,),
        in_specs=[pl.BlockSpec((B, MLA_HEADS, V_DIM, tt), lambda i: (0, 0, 0, i)), pl.BlockSpec((tt, B, S5_W), tm),
                  rblk, rblk, rblk, pl.BlockSpec((B, tt, D), row),
                  _const_spec(w['wo_m'].shape), _const_spec(w['wo_s'].shape), _const_spec(w['wo_r'].shape),
                  vec, vec, _const_spec((RWKV_W, RWKV_W)), _const_spec((1, D)), _const_spec((1, D))],
        out_specs=pl.BlockSpec((B, tt, D), row),
        out_shape=jax.ShapeDtypeStruct((B, T, D), F32),
        compiler_params=_cparams("parallel"),
        name="outproj_ln",
    )(mla_t, s5_tm, *rwkv, x, w['wo_m'], w['wo_s'], w['wo_r'], wr['gng'], wr['gnb'], wr['ones'], w['ln1g'],
      w['ln1b'])


def _mlp_kernel(x_ref, wu_ref, wd_ref, g_ref, b_ref, o_ref, *, alpha, fc):
    x = x_ref[0]
    xb = x.astype(BF16)
    acc = alpha * x
    for c in range(wu_ref.shape[1] // fc):
        h = jnp.maximum(jnp.dot(xb, wu_ref[:, c * fc:(c + 1) * fc], preferred_element_type=F32), 0.0)
        acc = acc + jnp.dot((h * h).astype(BF16), wd_ref[c * fc:(c + 1) * fc, :], preferred_element_type=F32)
    o_ref[0] = _layer_norm(acc, g_ref[...], b_ref[...])


def _mlp(x, w, alpha):
    B, T, D = x.shape
    tt = _tile(T, 512)
    row = lambda b, i: (b, i, 0)
    return pl.pallas_call(
        functools.partial(_mlp_kernel, alpha=alpha, fc=1024),
        grid=(B, T // tt),
        in_specs=[pl.BlockSpec((1, tt, D), row), _const_spec(w['wup'].shape), _const_spec(w['wdown'].shape),
                  _const_spec((1, D)), _const_spec((1, D))],
        out_specs=pl.BlockSpec((1, tt, D), row),
        out_shape=jax.ShapeDtypeStruct((B, T, D), F32),
        compiler_params=_cparams("parallel", "parallel"),
        name="mlp_ln",
    )(x, w['wup'], w['wdown'], w['ln2g'], w['ln2b'])


def _rot_cols(wr):
    half = ROPE_DIM // 2
    return jnp.concatenate([-wr[..., half:], wr[..., :half]], axis=-1)


def _prep_layer(p):
    D = p['w_in'].shape[0]
    w_in = p['w_in']
    o_kv = Q_RANK
    o_kr = Q_RANK + KV_RANK
    o_u = o_kr + ROPE_DIM
    o_p = o_u + S5_W
    w_kr = w_in[:, o_kr:o_u]
    wm = jnp.concatenate([w_in[:, :o_kr], w_kr, _rot_cols(w_kr),
                          jnp.zeros((D, V7X_LANES - 2 * ROPE_DIM), F32)], axis=1)
    wqb = p['w_qb'].reshape(Q_RANK, MLA_HEADS, NOPE_DIM + ROPE_DIM) * (MLA_SCALE * LOG2_E)
    zq = jnp.zeros((Q_RANK, MLA_HEADS, HEAD_PAD - NOPE_DIM - ROPE_DIM), F32)
    wq1 = jnp.concatenate([wqb, zq], axis=-1).reshape(Q_RANK, MLA_HEADS * HEAD_PAD)
    wq2 = jnp.concatenate([jnp.zeros((Q_RANK, MLA_HEADS, NOPE_DIM), F32), _rot_cols(wqb[..., NOPE_DIM:]), zq],
                          axis=-1).reshape(Q_RANK, MLA_HEADS * HEAD_PAD)
    wkvb = p['w_kvb'].reshape(KV_RANK, MLA_HEADS, NOPE_DIM + V_DIM)
    wk = jnp.concatenate([wkvb[..., :NOPE_DIM], jnp.zeros((KV_RANK, MLA_HEADS, HEAD_PAD - NOPE_DIM), F32)],
                         axis=-1).reshape(KV_RANK, MLA_HEADS * HEAD_PAD)
    pk = jnp.concatenate([jnp.zeros((ROPE_DIM, NOPE_DIM), F32), jnp.eye(ROPE_DIM, dtype=F32),
                          jnp.zeros((ROPE_DIM, HEAD_PAD - NOPE_DIM - ROPE_DIM), F32)], axis=1)
    pk = jnp.tile(pk, (1, MLA_HEADS))
    wvt = wkvb[..., NOPE_DIM:].reshape(KV_RANK, MLA_HEADS * V_DIM).T

    lr, li = p['lam_re'], p['lam_im']
    dt = jnp.exp(p['log_dt'])[:, None]
    mag = jnp.exp(lr * dt)
    ar, ai = mag * jnp.cos(li * dt), mag * jnp.sin(li * dt)
    den = lr * lr + li * li
    cr = ((ar - 1.0) * lr + ai * li) / den
    ci = (ai * lr - (ar - 1.0) * li) / den
    bbr = cr[..., None] * p['b_re'] - ci[..., None] * p['b_im']
    bbi = cr[..., None] * p['b_im'] + ci[..., None] * p['b_re']
    eye_g = jnp.eye(S5_GROUPS, dtype=F32)
    bd_in = lambda m: (m.transpose(0, 2, 1)[:, :, None, :] * eye_g[:, None, :, None]).reshape(S5_W, S5_N)
    bd_out = lambda m: (m.transpose(0, 2, 1)[:, :, None, :] * eye_g[:, None, :, None]).reshape(S5_N, S5_W)
    wb = jnp.concatenate([bd_in(bbr), bd_in(bbi)], axis=1)
    wc = jnp.concatenate([bd_out(p['c_re']), -bd_out(p['c_im'])], axis=0)
    a = jnp.stack([ar.reshape(S5_N), ai.reshape(S5_N)])

    def lora_pad(w, off):
        z = jnp.zeros((V7X_LANES, RWKV_W), F32)
        return z.at[off:off + w.shape[0]].set(w)

    seg = jnp.arange(RWKV_W) // RWKV_HEAD
    ones = (seg[:, None] == seg[None, :]).astype(F32)
    row = lambda v: v.reshape(1, -1).astype(F32)
    wo = p['w_out']
    return dict(
        inproj=dict(wm=wm.astype(BF16), wu=w_in[:, o_u:o_p].astype(BF16), wp=w_in[:, o_p:].astype(BF16),
                    wq1=wq1.astype(BF16), wq2=wq2.astype(BF16), qg=row(p['q_norm_g']), kg=row(p['kv_norm_g'])),
        kvup=dict(wk=wk.astype(BF16), pk=pk.astype(BF16), wvt=wvt.astype(BF16)),
        s5=dict(a=a, wb=wb.astype(BF16), wc=wc.astype(BF16), d=row(p['s5_d']), wglu=p['w_glu'].astype(BF16),
                bglu=row(p['b_glu'])),
        rwkv=dict(mu=row(p['mu_shift']), ww=lora_pad(p['w_w2'], 0).astype(BF16),
                  wa=lora_pad(p['w_a2'], DECAY_LORA).astype(BF16),
                  wg=lora_pad(p['w_g2'], DECAY_LORA + AAA_LORA).astype(BF16),
                  w0=row(p['w0']), a0=row(p['a0']), kk=row(p['k_k']), ka=row(p['k_a']), rk=row(p['r_k']),
                  gng=row(p['gn_g']), gnb=row(p['gn_b']), ones=ones.astype(BF16)),
        out=dict(wo_m=wo[:MLA_HEADS * V_DIM].astype(BF16),
                 wo_s=wo[MLA_HEADS * V_DIM:MLA_HEADS * V_DIM + S5_W].astype(BF16),
                 wo_r=wo[MLA_HEADS * V_DIM + S5_W:].astype(BF16), ln1g=row(p['ln1_g']), ln1b=row(p['ln1_b'])),
        mlp=dict(wup=p['w_up'].astype(BF16), wdown=p['w_down'].astype(BF16), ln2g=row(p['ln2_g']),
                 ln2b=row(p['ln2_b'])),
    )


def _rope_tables(pos):
    inv_freq = ROPE_BASE ** (-jnp.arange(0, ROPE_DIM, 2, dtype=F32) / ROPE_DIM)
    ang = pos.astype(F32)[:, None] * inv_freq[None, :]
    ang = jnp.concatenate([ang, ang], -1)
    cos, sin = jnp.cos(ang), jnp.sin(ang)
    T = pos.shape[0]
    z = lambda n: jnp.zeros((T, n), F32)
    return dict(
        csk=jnp.concatenate([cos, sin, z(V7X_LANES - 2 * ROPE_DIM)], axis=1),
        cq=jnp.concatenate([jnp.ones((T, NOPE_DIM), F32), cos, z(HEAD_PAD - NOPE_DIM - ROPE_DIM)], axis=1),
        sq=jnp.concatenate([z(NOPE_DIM), sin, z(HEAD_PAD - NOPE_DIM - ROPE_DIM)], axis=1),
    )


def _trunk_layer(x, tabs, past, s5_state, rwkv_state, shift0, w, alpha):
    B, T, D = x.shape
    qcat, ckv, krope, u_tm, p_tm = _inproj(x, w['inproj'], tabs)
    kcat, vt = _kvup(ckv, krope, w['kvup'])
    if past is None:
        mla = _attention(qcat, kcat, vt, kv_len=T, chunk_causal=True)
    else:
        kv_len = past[0].shape[1] + T
        bk = past[1].shape[4]
        kcat = jnp.concatenate([past[0], kcat], axis=1)
        kcat = jnp.pad(kcat, ((0, 0), (0, -kv_len % bk), (0, 0)))
        vt = jnp.pad(vt, ((0, 0), (0, 0), (0, 0), (0, 0), (0, bk - vt.shape[4])))
        vt = jnp.concatenate([past[1], vt], axis=2)
        mla = _attention(qcat, kcat, vt, kv_len=kv_len, chunk_causal=False)

    x0 = jnp.concatenate([s5_state[..., 0].reshape(B, S5_N), s5_state[..., 1].reshape(B, S5_N)], axis=1)
    s5_out, x_last = _s5(u_tm, x0, w['s5'])
    new_s5 = jnp.stack([x_last[:, :S5_N].reshape(B, S5_GROUPS, S5_STATE),
                        x_last[:, S5_N:].reshape(B, S5_GROUPS, S5_STATE)], axis=-1)

    rwkv, new_rwkv, shift = _rwkv_mixer(p_tm, shift0.reshape(B, RWKV_PROJ), rwkv_state, w['rwkv'])
    x = _outproj(mla, s5_out, rwkv, x, w['out'], w['rwkv'], alpha)
    x = _mlp(x, w['mlp'], alpha)
    return x, ckv, krope, new_s5, new_rwkv, shift.reshape(B, 1, RWKV_PROJ)


def kernel(x_prompt, x_sample, cache_mla_ckv, cache_mla_krope, state_s5, state_rwkv, state_rwkv_shift, w_in, q_norm_g, w_qb, kv_norm_g, w_kvb, lam_re, lam_im, log_dt, b_re, b_im, c_re, c_im, s5_d, w_glu, b_glu, mu_shift, w0, w_w2, a0, w_a2, w_g2, k_k, k_a, r_k, gn_g, gn_b, w_out, ln1_g, ln1_b, w_up, w_down, ln2_g, ln2_b):
    depth = w_in.shape[0]
    alpha = (2 * depth) ** 0.25
    bp, sp = x_prompt.shape[:2]
    ts = x_sample.shape[1]
    past = cache_mla_ckv.shape[2]
    tabs_p = _rope_tables(jnp.arange(sp))
    tabs_s = _rope_tables(past + jnp.arange(ts))
    names = dict(w_in=w_in, q_norm_g=q_norm_g, w_qb=w_qb, kv_norm_g=kv_norm_g, w_kvb=w_kvb, lam_re=lam_re,
                 lam_im=lam_im, log_dt=log_dt, b_re=b_re, b_im=b_im, c_re=c_re, c_im=c_im, s5_d=s5_d, w_glu=w_glu,
                 b_glu=b_glu, mu_shift=mu_shift, w0=w0, w_w2=w_w2, a0=a0, w_a2=w_a2, w_g2=w_g2, k_k=k_k, k_a=k_a,
                 r_k=r_k, gn_g=gn_g, gn_b=gn_b, w_out=w_out, ln1_g=ln1_g, ln1_b=ln1_b, w_up=w_up, w_down=w_down,
                 ln2_g=ln2_g, ln2_b=ln2_b)
    s5_zero = jnp.zeros((bp, S5_GROUPS, S5_STATE, 2), F32)
    rwkv_zero = jnp.zeros((bp, RWKV_HEADS, RWKV_HEAD, RWKV_HEAD), F32)
    shift_zero = jnp.zeros((bp, 1, RWKV_PROJ), F32)

    xp, xs = x_prompt, x_sample
    outs_p = [[] for _ in range(5)]
    outs_s = [[] for _ in range(5)]
    for l in range(depth):
        w = _prep_layer({k: v[l] for k, v in names.items()})
        xp, *rest = _trunk_layer(xp, tabs_p, None, s5_zero, rwkv_zero, shift_zero, w, alpha)
        for acc, val in zip(outs_p, rest):
            acc.append(val)
        past_kv = _kvup(cache_mla_ckv[l], cache_mla_krope[l], w['kvup'])
        xs, *rest = _trunk_layer(xs, tabs_s, past_kv, state_s5[l], state_rwkv[l], state_rwkv_shift[l], w, alpha)
        for acc, val in zip(outs_s, rest):
            acc.append(val)
    return (xp, xs, *[jnp.stack(a) for a in outs_p], *[jnp.stack(a) for a in outs_s])
```

```python
import functools
import math

import jax
import jax.numpy as jnp
from jax import lax
from jax.experimental import pallas as pl
from jax.experimental.pallas import tpu as pltpu

F32 = jnp.float32
BF16 = jnp.bfloat16

CHUNK = 64
MLA_HEADS = 6
NOPE_DIM = 64
ROPE_DIM = 32
V_DIM = 64
Q_RANK = 256
KV_RANK = 128
ROPE_BASE = 10000.0
MLA_SCALE = (NOPE_DIM + ROPE_DIM) ** -0.5
S5_GROUP_CH = 16
S5_W = 256
S5_GROUPS = S5_W // S5_GROUP_CH
S5_STATE = 64
S5_N = S5_GROUPS * S5_STATE
RWKV_HEADS = 6
RWKV_HEAD = 64
RWKV_W = RWKV_HEADS * RWKV_HEAD
DECAY_LORA = 32
AAA_LORA = 32
GATE_LORA = 64
RWKV_PROJ = 3 * RWKV_W + DECAY_LORA + AAA_LORA + GATE_LORA
LN_EPS = 1e-5
RMS_EPS = 1e-6
GN_EPS = 64e-5
NEG_INF = -1e30

V7X_LANES = 128
V7X_VMEM_BYTES = 64 * 1024 * 1024
VMEM_LIMIT_BYTES = V7X_VMEM_BYTES - 8 * 1024 * 1024
HEAD_PAD = V7X_LANES
ATTN_BLOCK = 512
ROW_TILE = 128
ATTN_HEAD_GROUP = 6
V_ROWS = V_DIM + 16
LOG2_E = 1.4426950408889634
RWKV_PACK = 8


def _cparams(*sem):
    return pltpu.CompilerParams(dimension_semantics=sem, vmem_limit_bytes=VMEM_LIMIT_BYTES)


def _const_spec(shape):
    zeros = (0,) * len(shape)
    return pl.BlockSpec(shape, lambda *_: zeros)


def _tile(n, target):
    if n <= target:
        return n
    t = target
    while n % t:
        t //= 2
    return t


def _layer_norm(y, g, b):
    mu = jnp.mean(y, -1, keepdims=True)
    yc = y - mu
    var = jnp.mean(yc * yc, -1, keepdims=True)
    return yc * lax.rsqrt(var + LN_EPS) * g + b


def _inproj_kernel(x_ref, wm_ref, wu_ref, wp_ref, wq1_ref, wq2_ref, qg_ref, kg_ref, csk_ref, cq_ref, sq_ref,
                   qcat_ref, ckv_ref, krope_ref, u_ref, p_ref):
    B, tt, D = x_ref.shape
    x = x_ref[...].reshape(B * tt, D).astype(BF16)
    per_b = lambda tab_ref: jnp.concatenate([tab_ref[...]] * B, axis=0)
    m = jnp.dot(x, wm_ref[...], preferred_element_type=F32)
    q_lat = m[:, :Q_RANK]
    qn = q_lat * lax.rsqrt(jnp.mean(q_lat * q_lat, -1, keepdims=True) + RMS_EPS) * qg_ref[...]
    kv_lat = m[:, Q_RANK:Q_RANK + KV_RANK]
    ckv = kv_lat * lax.rsqrt(jnp.mean(kv_lat * kv_lat, -1, keepdims=True) + RMS_EPS) * kg_ref[...]
    ckv_ref[...] = ckv.reshape(B, tt, KV_RANK)
    kr = m[:, Q_RANK + KV_RANK:] * per_b(csk_ref)
    kr = kr + pltpu.roll(kr, V7X_LANES - ROPE_DIM, 1)
    krope_ref[...] = kr[:, :ROPE_DIM].reshape(B, tt, ROPE_DIM)
    qb = qn.astype(BF16)
    q1 = jnp.dot(qb, wq1_ref[...], preferred_element_type=F32)
    q2 = jnp.dot(qb, wq2_ref[...], preferred_element_type=F32)
    cq = per_b(cq_ref)
    sq = per_b(sq_ref)
    for h in range(MLA_HEADS):
        sl = slice(h * HEAD_PAD, (h + 1) * HEAD_PAD)
        qcat_ref[:, :, sl] = (q1[:, sl] * cq + q2[:, sl] * sq).astype(BF16).reshape(B, tt, HEAD_PAD)
    u = jnp.dot(x, wu_ref[...], preferred_element_type=F32)
    p = jnp.dot(x, wp_ref[...], preferred_element_type=F32)
    for b in range(B):
        u_ref[:, b, :] = u[b * tt:(b + 1) * tt]
        p_ref[:, b, :] = p[b * tt:(b + 1) * tt]


def _inproj(x, w, tabs):
    B, T, D = x.shape
    tt = _tile(T, ROW_TILE)
    nq = MLA_HEADS * HEAD_PAD
    row = lambda i: (0, i, 0)
    tab = lambda i: (i, 0)
    tm = lambda i: (i, 0, 0)
    return pl.pallas_call(
        _inproj_kernel,
        grid=(T // tt,),
        in_specs=[
            pl.BlockSpec((B, tt, D), row),
            _const_spec(w['wm'].shape), _const_spec(w['wu'].shape), _const_spec(w['wp'].shape),
            _const_spec(w['wq1'].shape), _const_spec(w['wq2'].shape),
            _const_spec((1, Q_RANK)), _const_spec((1, KV_RANK)),
            pl.BlockSpec((tt, V7X_LANES), tab), pl.BlockSpec((tt, HEAD_PAD), tab), pl.BlockSpec((tt, HEAD_PAD), tab),
        ],
        out_specs=[
            pl.BlockSpec((B, tt, nq), row),
            pl.BlockSpec((B, tt, KV_RANK), row),
            pl.BlockSpec((B, tt, ROPE_DIM), row),
            pl.BlockSpec((tt, B, S5_W), tm),
            pl.BlockSpec((tt, B, RWKV_PROJ), tm),
        ],
        out_shape=[
            jax.ShapeDtypeStruct((B, T, nq), BF16),
            jax.ShapeDtypeStruct((B, T, KV_RANK), F32),
            jax.ShapeDtypeStruct((B, T, ROPE_DIM), F32),
            jax.ShapeDtypeStruct((T, B, S5_W), F32),
            jax.ShapeDtypeStruct((T, B, RWKV_PROJ), F32),
        ],
        compiler_params=_cparams("parallel"),
        name="inproj",
    )(x, w['wm'], w['wu'], w['wp'], w['wq1'], w['wq2'], w['qg'], w['kg'], tabs['csk'], tabs['cq'], tabs['sq'])


def _kvup_kernel(ckv_ref, kr_ref, wk_ref, pk_ref, wvt_ref, kcat_ref, vt_ref):
    c = ckv_ref[0].astype(BF16)
    kr = kr_ref[0].astype(BF16)
    kcat = jnp.dot(c, wk_ref[...], preferred_element_type=F32) + jnp.dot(kr, pk_ref[...], preferred_element_type=F32)
    kcat_ref[0] = kcat.astype(BF16)
    vt = lax.dot_general(wvt_ref[...], c, (((1,), (1,)), ((), ())), preferred_element_type=F32).astype(BF16)
    ones = jnp.ones((V_ROWS - V_DIM, vt.shape[1]), BF16)
    for h in range(MLA_HEADS):
        vt_ref[0, h, 0, :V_DIM, :] = vt[h * V_DIM:(h + 1) * V_DIM, :]
        vt_ref[0, h, 0, V_DIM:, :] = ones


def _kvup(ckv, krope, w):
    B, T, _ = ckv.shape
    tt = _tile(T, ATTN_BLOCK)
    nk = MLA_HEADS * HEAD_PAD
    row = lambda b, i: (b, i, 0)
    return pl.pallas_call(
        _kvup_kernel,
        grid=(B, T // tt),
        in_specs=[pl.BlockSpec((1, tt, KV_RANK), row), pl.BlockSpec((1, tt, ROPE_DIM), row),
                  _const_spec(w['wk'].shape), _const_spec(w['pk'].shape), _const_spec(w['wvt'].shape)],
        out_specs=[pl.BlockSpec((1, tt, nk), row),
                   pl.BlockSpec((1, MLA_HEADS, 1, V_ROWS, tt), lambda b, i: (b, 0, i, 0, 0))],
        out_shape=[jax.ShapeDtypeStruct((B, T, nk), BF16),
                   jax.ShapeDtypeStruct((B, MLA_HEADS, T // tt, V_ROWS, tt), BF16)],
        compiler_params=_cparams("parallel", "parallel"),
        name="kvup",
    )(ckv, krope, w['wk'], w['pk'], w['wvt'])


def _attn_kernel(q_ref, k_ref, v_ref, o_ref, sa_ref, sb_ref, bias_ref, m_ref, acc_ref, *, bq, bk, kv_len,
                 chunk_causal):
    hg = ATTN_HEAD_GROUP
    qi = pl.program_id(2)
    last = qi if chunk_causal else jnp.int32(-(-kv_len // bk) - 1)
    nt = (((1,), (1,)), ((), ()))

    def scores(j, s_ref, h):
        r0 = pl.multiple_of(j * bk, bk)
        sl = slice(h * HEAD_PAD, (h + 1) * HEAD_PAD)
        s_ref[h] = lax.dot_general(k_ref[0, pl.ds(r0, bk), sl], q_ref[0, :, sl], nt,
                                   preferred_element_type=F32)

    def softmax_pv(j, s_ref, h, masked):
        s = s_ref[h]
        if masked:
            s = s + bias_ref[...]
        m = m_ref[h]
        m_new = jnp.maximum(m, jnp.max(s, axis=0, keepdims=True))
        p = jnp.exp2(s - m_new)
        alpha = jnp.exp2(m - m_new)
        acc_ref[h] = alpha * acc_ref[h] + jnp.dot(v_ref[0, h, j], p.astype(BF16),
                                                  preferred_element_type=F32)
        m_ref[h] = m_new

    @pl.when(qi == 0)
    def _():
        krel = lax.broadcasted_iota(jnp.int32, (bk, bq), 0)
        if chunk_causal:
            qrel = lax.broadcasted_iota(jnp.int32, (bk, bq), 1)
            ok = (krel // CHUNK) <= (qrel // CHUNK)
        else:
            ok = krel < kv_len - (-(-kv_len // bk) - 1) * bk
        bias_ref[...] = jnp.where(ok, 0.0, NEG_INF)

    def advance(j_next, s_next, j, s_cur, masked=False):
        for h in range(hg):
            if j_next is not None:
                scores(j_next, s_next, h)
            softmax_pv(j, s_cur, h, masked)

    m_ref[...] = jnp.full(m_ref.shape, NEG_INF, F32)
    acc_ref[...] = jnp.zeros(acc_ref.shape, F32)
    for h in range(hg):
        scores(0, sa_ref, h)

    def pair(p, _):
        j = 2 * p
        advance(j + 1, sb_ref, j, sa_ref)
        advance(j + 2, sa_ref, j + 1, sb_ref)
        return 0

    lax.fori_loop(0, last // 2, pair, 0)

    @pl.when(last % 2 == 0)
    def _():
        advance(None, None, last, sa_ref, True)

    @pl.when(last % 2 == 1)
    def _():
        advance(last, sb_ref, last - 1, sa_ref)
        advance(None, None, last, sb_ref, True)

    for h in range(hg):
        o_ref[0, h] = (acc_ref[h, :V_DIM, :] / acc_ref[h, V_DIM:V_DIM + 1, :]).astype(BF16)


def _attention(qcat, kcat, vt, *, kv_len, chunk_causal):
    B, Tq, _ = qcat.shape
    Tk = kcat.shape[1]
    nblk, bk = vt.shape[2], vt.shape[4]
    assert nblk * bk == Tk
    bq = bk if chunk_causal else Tq
    hg = ATTN_HEAD_GROUP
    out_t = pl.pallas_call(
        functools.partial(_attn_kernel, bq=bq, bk=bk, kv_len=kv_len, chunk_causal=chunk_causal),
        grid=(B, MLA_HEADS // hg, Tq // bq),
        in_specs=[
            pl.BlockSpec((1, bq, hg * HEAD_PAD), lambda b, g, i: (b, i, g)),
            pl.BlockSpec((1, Tk, hg * HEAD_PAD), lambda b, g, i: (b, 0, g), pipeline_mode=pl.Buffered(1)),
            pl.BlockSpec((1, hg, nblk, V_ROWS, bk), lambda b, g, i: (b, g, 0, 0, 0), pipeline_mode=pl.Buffered(1)),
        ],
        out_specs=pl.BlockSpec((1, hg, V_DIM, bq), lambda b, g, i: (b, g, 0, i)),
        out_shape=jax.ShapeDtypeStruct((B, MLA_HEADS, V_DIM, Tq), BF16),
        scratch_shapes=[pltpu.VMEM((hg, bk, bq), F32), pltpu.VMEM((hg, bk, bq), F32), pltpu.VMEM((bk, bq), F32),
                        pltpu.VMEM((hg, 1, bq), F32), pltpu.VMEM((hg, V_ROWS, bq), F32)],
        compiler_params=_cparams("parallel", "parallel", "arbitrary"),
        name="mla_attention",
    )(qcat, kcat, vt)
    return out_t


def _gelu_tanh(x):
    return 0.5 * x * (1.0 + jnp.tanh(math.sqrt(2.0 / math.pi) * (x + 0.044715 * (x * x * x))))


def _s5_kernel(u_ref, x0_ref, a_ref, wb_ref, wc_ref, d_ref, wg_ref, bg_ref, o_ref, xl_ref, xs_ref, st_ref, *, L, B):
    @pl.when(pl.program_id(0) == 0)
    def _():
        st_ref[...] = x0_ref[...]

    u = u_ref[...].reshape(L * B, S5_W)
    xs_ref[...] = jnp.dot(u.astype(BF16), wb_ref[...], preferred_element_type=F32)
    a_re = a_ref[0:1, :]
    a_im = a_ref[1:2, :]

    def body(t, carry):
        x_re, x_im = carry
        r0 = pl.multiple_of(t * B, B)
        n_re = a_re * x_re - a_im * x_im + xs_ref[pl.ds(r0, B), :S5_N]
        n_im = a_re * x_im + a_im * x_re + xs_ref[pl.ds(r0, B), S5_N:]
        xs_ref[pl.ds(r0, B), :S5_N] = n_re
        xs_ref[pl.ds(r0, B), S5_N:] = n_im
        return n_re, n_im

    x_re, x_im = lax.fori_loop(0, L, body, (st_ref[:, :S5_N], st_ref[:, S5_N:]), unroll=8)
    st_ref[:, :S5_N] = x_re
    st_ref[:, S5_N:] = x_im
    xl_ref[...] = st_ref[...]

    y = jnp.dot(xs_ref[...].astype(BF16), wc_ref[...], preferred_element_type=F32) + d_ref[...] * u
    z = _gelu_tanh(y)
    gate = jax.nn.sigmoid(jnp.dot(z.astype(BF16), wg_ref[...], preferred_element_type=F32) + bg_ref[...])
    o_ref[...] = (z * gate).reshape(L, B, S5_W)


def _s5(u_tm, x0, w):
    T, B, _ = u_tm.shape
    L = _tile(T, 1024 // B)
    return pl.pallas_call(
        functools.partial(_s5_kernel, L=L, B=B),
        grid=(T // L,),
        in_specs=[
            pl.BlockSpec((L, B, S5_W), lambda i: (i, 0, 0)),
            _const_spec((B, 2 * S5_N)), _const_spec((2, S5_N)),
            _const_spec((S5_W, 2 * S5_N)), _const_spec((2 * S5_N, S5_W)),
            _const_spec((1, S5_W)), _const_spec((S5_W, S5_W)), _const_spec((1, S5_W)),
        ],
        out_specs=[pl.BlockSpec((L, B, S5_W), lambda i: (i, 0, 0)), _const_spec((B, 2 * S5_N))],
        out_shape=[jax.ShapeDtypeStruct((T, B, S5_W), F32), jax.ShapeDtypeStruct((B, 2 * S5_N), F32)],
        scratch_shapes=[pltpu.VMEM((L * B, 2 * S5_N), F32), pltpu.VMEM((B, 2 * S5_N), F32)],
        compiler_params=_cparams("arbitrary"),
        name="s5_scan",
    )(u_tm, x0, w['a'], w['wb'], w['wc'], w['d'], w['wglu'], w['bglu'])


def _softplus(x):
    return jnp.maximum(x, 0.0) + jnp.log(1.0 + jnp.exp(-jnp.abs(x)))


def _head_sum(x, ones_ref):
    hi = x.astype(BF16)
    lo = (x - hi.astype(F32)).astype(BF16)
    ones = ones_ref[...]
    return jnp.dot(hi, ones, preferred_element_type=F32) + jnp.dot(lo, ones, preferred_element_type=F32)


def _pack_rows(x, o_ref, zs_ref, B):
    rows = RWKV_PACK * B
    for m in range(x.shape[0] // rows):
        xm = x[m * rows:(m + 1) * rows]
        for h in range(RWKV_HEADS):
            c = h * RWKV_HEAD // V7X_LANES
            piece = xm[:, c * V7X_LANES:(c + 1) * V7X_LANES]
            if (h * RWKV_HEAD) % V7X_LANES:
                piece = pltpu.roll(piece, V7X_LANES - (h * RWKV_HEAD) % V7X_LANES, 1)
            zs_ref[pl.ds(h, rows, stride=RWKV_HEADS), :] = piece
        o_ref[m] = zs_ref[...].T[:RWKV_HEAD, :]


def _rwkv_prep_kernel(p_ref, sh0_ref, mu_ref, ww_ref, wa_ref, wg_ref, w0_ref, a0_ref, ka_ref, rk_ref,
                      ones_ref, r_o, w_o, k_o, v_o, a_o, g_o, bon_o, sh_o, last_ref, zs_ref, *, L, B):
    @pl.when(pl.program_id(0) == 0)
    def _():
        last_ref[...] = sh0_ref[...]

    p = p_ref[...].reshape(L * B, RWKV_PROJ)
    if L > 1:
        prev = jnp.concatenate([last_ref[...], p[:(L - 1) * B]], axis=0)
    else:
        prev = last_ref[...]
    last_ref[...] = p[(L - 1) * B:]
    sh_o[...] = p[(L - 1) * B:]
    ps = p + (prev - p) * mu_ref[...]
    r = ps[:, :RWKV_W]
    k = ps[:, RWKV_W:2 * RWKV_W]
    v = ps[:, 2 * RWKV_W:3 * RWKV_W]
    tail = ps[:, 3 * RWKV_W:]
    lw = jnp.dot(jnp.tanh(tail).astype(BF16), ww_ref[...], preferred_element_type=F32)
    la = jnp.dot(tail.astype(BF16), wa_ref[...], preferred_element_type=F32)
    g = jnp.dot(jax.nn.sigmoid(tail).astype(BF16), wg_ref[...], preferred_element_type=F32)
    w_log = -_softplus(-(w0_ref[...] + lw)) - 0.5
    decay = jnp.exp(-jnp.exp(w_log))
    a = jax.nn.sigmoid(a0_ref[...] + la)
    k2 = k * (1.0 + (a - 1.0) * ka_ref[...])
    bonus = _head_sum(r * k2 * rk_ref[...], ones_ref) * v
    for val, o_ref in ((r, r_o), (decay, w_o), (k, k_o), (v, v_o), (a, a_o)):
        _pack_rows(val, o_ref, zs_ref, B)
    shp = (L, B, RWKV_W)
    g_o[...] = g.reshape(shp)
    bon_o[...] = bonus.reshape(shp)


def _rwkv_prep(p_tm, shift0, w):
    T, B, _ = p_tm.shape
    L = _tile(T, 512 // B)
    assert L % RWKV_PACK == 0
    blk = pl.BlockSpec((L, B, RWKV_W), lambda i: (i, 0, 0))
    W = RWKV_PACK * B * RWKV_HEADS
    pblk = pl.BlockSpec((L // RWKV_PACK, RWKV_HEAD, W), lambda i: (i, 0, 0))
    packed = jax.ShapeDtypeStruct((T // RWKV_PACK, RWKV_HEAD, W), F32)
    vec = _const_spec((1, RWKV_W))
    outs = pl.pallas_call(
        functools.partial(_rwkv_prep_kernel, L=L, B=B),
        grid=(T // L,),
        in_specs=[pl.BlockSpec((L, B, RWKV_PROJ), lambda i: (i, 0, 0)), _const_spec((B, RWKV_PROJ)),
                  _const_spec((1, RWKV_PROJ)),
                  _const_spec((V7X_LANES, RWKV_W)), _const_spec((V7X_LANES, RWKV_W)), _const_spec((V7X_LANES, RWKV_W)),
                  vec, vec, vec, vec, _const_spec((RWKV_W, RWKV_W))],
        out_specs=[pblk] * 5 + [blk] * 2 + [_const_spec((B, RWKV_PROJ))],
        out_shape=[packed] * 5 + [jax.ShapeDtypeStruct((T, B, RWKV_W), F32)] * 2
        + [jax.ShapeDtypeStruct((B, RWKV_PROJ), F32)],
        scratch_shapes=[pltpu.VMEM((B, RWKV_PROJ), F32), pltpu.VMEM((RWKV_PACK * B * RWKV_HEADS, V7X_LANES), F32)],
        compiler_params=_cparams("arbitrary"),
        name="rwkv_prep",
    )(p_tm, shift0, w['mu'], w['ww'], w['wa'], w['wg'], w['w0'], w['a0'], w['ka'], w['rk'], w['ones'])
    return outs


def _lane_window(tile, off, width, dst, lane):
    v0, lo = divmod(off, V7X_LANES)
    shift = (dst - lo) % V7X_LANES
    a = tile(v0)
    r = pltpu.roll(a, shift, 1) if shift else a
    if lo + width > V7X_LANES:
        b = tile(v0 + 1)
        r2 = pltpu.roll(b, shift, 1) if shift else b
        r = jnp.where(lane < dst + (V7X_LANES - lo), r, r2)
    return r


def _chain_lanes(pieces, bh, lane):
    rep = len(pieces)
    out = jnp.where(lane < rep * bh, pieces[rep - 1], 0.0)
    for r in range(rep - 2, -1, -1):
        out = jnp.where(lane < (r + 1) * bh, pieces[r], out)
    return out


def _rwkv_scan_kernel(w_ref, a_ref, k_ref, kx_ref, r_ref, v_ref, kkt_ref, kat_ref, s0_ref, y_ref, so_ref,
                      s_ref, sa_ref, e_ref, ev_ref, ys_ref, kall_ref, *, Tm, IL, bh, rep):
    lane = lax.broadcasted_iota(jnp.int32, (RWKV_HEAD, V7X_LANES), 1)
    lane_il = lax.broadcasted_iota(jnp.int32, (IL, V7X_LANES), 1)
    ncol = RWKV_PACK * bh // V7X_LANES

    def tiles(slab, rows=slice(None)):
        return lambda c: slab[rows, c * V7X_LANES:(c + 1) * V7X_LANES]

    def expand(slab, t8):
        return _chain_lanes([_lane_window(tiles(slab), t8 * bh, bh, r * bh, lane) for r in range(rep)], bh, lane)

    def unit_kk(k_raw):
        kk = k_raw * kkt_ref[...]
        nrm = jnp.sqrt(jnp.sum(kk * kk, axis=0, keepdims=True))
        return kk / jnp.maximum(nrm, 1e-12)

    kall_ref[0:Tm] = k_ref[...]
    kall_ref[Tm] = kx_ref[0]

    @pl.when(pl.program_id(0) == 0)
    def _():
        s_ref[...] = s0_ref[...]
        e_ref[0, 4] = expand(-unit_kk(k_ref[0]), 0)
        acc = jnp.zeros((IL, V7X_LANES), F32)
        for j in range(RWKV_HEAD):
            acc = acc + s0_ref[j] * e_ref[0, 4, j:j + 1, :]
        sa_ref[...] = acc

    def body(m, sa):
        k_raw = kall_ref[m]
        a_s = a_ref[m]
        kkn = unit_kk(k_raw)
        slabs = [w_ref[m], kkn * a_s, k_raw * (1.0 + (a_s - 1.0) * kat_ref[...]), r_ref[m]]
        nn0 = -kkn
        nn1 = -unit_kk(kall_ref[m + 1])
        vs = v_ref[m]
        for t8 in range(RWKV_PACK):
            for o, slab in enumerate(slabs):
                e_ref[t8, o] = expand(slab, t8)
            e_ref[t8, 4] = expand(nn0, t8 + 1) if t8 + 1 < RWKV_PACK else expand(nn1, 0)
            ev_ref[t8] = _chain_lanes([_lane_window(tiles(vs, slice(r * IL, (r + 1) * IL)), t8 * bh, bh, r * bh,
                                                    lane_il) for r in range(rep)], bh, lane_il)
        for t8 in range(RWKV_PACK):
            vt = ev_ref[t8]
            yacc = jnp.zeros((IL, V7X_LANES), F32)
            san = jnp.zeros((IL, V7X_LANES), F32)
            for j in range(RWKV_HEAD):
                row = lambda o: e_ref[t8, o, j:j + 1, :]
                sn = s_ref[j] * row(0) + sa * row(1) + vt * row(2)
                s_ref[j] = sn
                yacc = yacc + sn * row(3)
                san = san + sn * row(4)
            sa = san
            ys_ref[t8] = yacc
        for c in range(ncol):
            for r in range(rep):
                col = jnp.zeros((IL, V7X_LANES), F32)
                for t8 in range(RWKV_PACK):
                    off = t8 * bh
                    lo, hi = max(off, c * V7X_LANES), min(off + bh, (c + 1) * V7X_LANES)
                    if lo < hi:
                        shift = (off - r * bh) % V7X_LANES
                        rolled = pltpu.roll(ys_ref[t8], shift, 1) if shift else ys_ref[t8]
                        inside = jnp.logical_and(lane_il >= lo - c * V7X_LANES, lane_il < hi - c * V7X_LANES)
                        col = jnp.where(inside, rolled, col)
                y_ref[m, r * IL:(r + 1) * IL, c * V7X_LANES:(c + 1) * V7X_LANES] = col
        return sa

    sa_ref[...] = lax.fori_loop(0, Tm, body, sa_ref[...])
    so_ref[...] = s_ref[...]


def _rwkv_scan(w_p, a_p, k_p, r_p, v_p, kk_t, ka_t, s0_l, bh, rep):
    TP, _, W = v_p.shape
    IL = RWKV_HEAD // rep
    Tm = _tile(TP, 8)
    blk = pl.BlockSpec((Tm, RWKV_HEAD, W), lambda i: (i, 0, 0))
    nxt = pl.BlockSpec((1, RWKV_HEAD, W), lambda i: (jnp.minimum((i + 1) * Tm, TP - 1), 0, 0))
    sblk = _const_spec((RWKV_HEAD, IL, V7X_LANES))
    par = _const_spec((RWKV_HEAD, W))
    return pl.pallas_call(
        functools.partial(_rwkv_scan_kernel, Tm=Tm, IL=IL, bh=bh, rep=rep),
        grid=(TP // Tm,),
        in_specs=[blk, blk, blk, nxt, blk, blk, par, par, sblk],
        out_specs=[blk, sblk],
        out_shape=[jax.ShapeDtypeStruct((TP, RWKV_HEAD, W), F32),
                   jax.ShapeDtypeStruct((RWKV_HEAD, IL, V7X_LANES), F32)],
        scratch_shapes=[pltpu.VMEM((RWKV_HEAD, IL, V7X_LANES), F32), pltpu.VMEM((IL, V7X_LANES), F32),
                        pltpu.VMEM((RWKV_PACK, 5, RWKV_HEAD, V7X_LANES), F32),
                        pltpu.VMEM((RWKV_PACK, IL, V7X_LANES), F32), pltpu.VMEM((RWKV_PACK, IL, V7X_LANES), F32),
                        pltpu.VMEM((Tm + 1, RWKV_HEAD, W), F32)],
        compiler_params=_cparams("arbitrary"),
        name="rwkv_scan",
    )(w_p, a_p, k_p, k_p, r_p, v_p, kk_t, ka_t, s0_l)


def _rwkv_layout(B):
    bh = B * RWKV_HEADS
    rep = 1
    while 2 * rep * bh <= V7X_LANES and RWKV_HEAD % (2 * rep) == 0:
        rep *= 2
    return bh, rep, RWKV_HEAD // rep


def _rwkv_mixer(p_tm, shift0, s0, w):
    T, B, _ = p_tm.shape
    bh, rep, IL = _rwkv_layout(B)
    lanes = rep * bh
    r_p, dec_p, k_p, v_p, a_p, g, bonus, shift = _rwkv_prep(p_tm, shift0, w)
    TP = T // RWKV_PACK

    def par_tile(vec):
        t = vec.reshape(RWKV_HEADS, RWKV_HEAD).T[:, None, None, :]
        return jnp.broadcast_to(t, (RWKV_HEAD, RWKV_PACK, B, RWKV_HEADS)).reshape(RWKV_HEAD, RWKV_PACK * bh)

    s0_l = s0.reshape(bh, rep, IL, RWKV_HEAD).transpose(3, 2, 1, 0).reshape(RWKV_HEAD, IL, lanes)
    s0_l = jnp.pad(s0_l, ((0, 0), (0, 0), (0, V7X_LANES - lanes)))
    y_p, s_l = _rwkv_scan(dec_p, a_p, k_p, r_p, v_p, par_tile(w['kk']), par_tile(w['ka']), s0_l, bh, rep)
    y_tm = y_p.reshape(TP, RWKV_HEAD, RWKV_PACK, bh).transpose(0, 2, 3, 1).reshape(T, B, RWKV_W)
    s_last = s_l[:, :, :lanes].reshape(RWKV_HEAD, IL, rep, bh).transpose(3, 2, 1, 0)
    s_last = s_last.reshape(B, RWKV_HEADS, RWKV_HEAD, RWKV_HEAD)
    return (y_tm, bonus, g), s_last, shift


def _outproj_kernel(m_ref, s_ref, y_ref, bon_ref, gate_ref, x_ref, wm_ref, ws_ref, wr_ref, gng_ref, gnb_ref,
                    ones_ref, g_ref, b_ref, o_ref, *, alpha):
    B, tt, D = x_ref.shape
    mla = jnp.concatenate([m_ref[b].reshape(MLA_HEADS * V_DIM, tt).T for b in range(B)], axis=0)
    rows = lambda ref: jnp.concatenate([ref[:, b, :] for b in range(B)], axis=0)
    acc = jnp.dot(mla, wm_ref[...], preferred_element_type=F32)
    acc = acc + jnp.dot(rows(s_ref).astype(BF16), ws_ref[...], preferred_element_type=F32)
    y = rows(y_ref)
    inv_n = 1.0 / RWKV_HEAD
    mu = _head_sum(y, ones_ref) * inv_n
    yc = y - mu
    var = _head_sum(yc * yc, ones_ref) * inv_n
    yn = yc * lax.rsqrt(var + GN_EPS) * gng_ref[...] + gnb_ref[...]
    rw = (yn + rows(bon_ref)) * rows(gate_ref)
    acc = acc + jnp.dot(rw.astype(BF16), wr_ref[...], preferred_element_type=F32)
    out = _layer_norm(alpha * x_ref[...].reshape(B * tt, D) + acc, g_ref[...], b_ref[...])
    o_ref[...] = out.reshape(B, tt, D)


def _outproj(mla_t, s5_tm, rwkv, x, w, wr, alpha):
    B, T, D = x.shape
    tt = _tile(T, ROW_TILE)
    row = lambda i: (0, i, 0)
    tm = lambda i: (i, 0, 0)
    rblk = pl.BlockSpec((tt, B, RWKV_W), tm)
    vec = _const_spec((1, RWKV_W))
    return pl.pallas_call(
        functools.partial(_outproj_kernel, alpha=alpha),
        grid=(T // tt,),
        in_specs=[pl.BlockSpec((B, MLA_HEADS, V_DIM, tt), lambda i: (0, 0, 0, i)), pl.BlockSpec((tt, B, S5_W), tm),
                  rblk, rblk, rblk, pl.BlockSpec((B, tt, D), row),
                  _const_spec(w['wo_m'].shape), _const_spec(w['wo_s'].shape), _const_spec(w['wo_r'].shape),
                  vec, vec, _const_spec((RWKV_W, RWKV_W)), _const_spec((1, D)), _const_spec((1, D))],
        out_specs=pl.BlockSpec((B, tt, D), row),
        out_shape=jax.ShapeDtypeStruct((B, T, D), F32),
        compiler_params=_cparams("parallel"),
        name="outproj_ln",
    )(mla_t, s5_tm, *rwkv, x, w['wo_m'], w['wo_s'], w['wo_r'], wr['gng'], wr['gnb'], wr['ones'], w['ln1g'],
      w['ln1b'])


def _mlp_kernel(x_ref, wu_ref, wd_ref, g_ref, b_ref, o_ref, *, alpha, fc):
    x = x_ref[0]
    xb = x.astype(BF16)
    acc = alpha * x
    for c in range(wu_ref.shape[1] // fc):
        h = jnp.maximum(jnp.dot(xb, wu_ref[:, c * fc:(c + 1) * fc], preferred_element_type=F32), 0.0)
        acc = acc + jnp.dot((h * h).astype(BF16), wd_ref[c * fc:(c + 1) * fc, :], preferred_element_type=F32)
    o_ref[0] = _layer_norm(acc, g_ref[...], b_ref[...])


def _mlp(x, w, alpha):
    B, T, D = x.shape
    tt = _tile(T, 512)
    row = lambda b, i: (b, i, 0)
    return pl.pallas_call(
        functools.partial(_mlp_kernel, alpha=alpha, fc=1024),
        grid=(B, T // tt),
        in_specs=[pl.BlockSpec((1, tt, D), row), _const_spec(w['wup'].shape), _const_spec(w['wdown'].shape),
                  _const_spec((1, D)), _const_spec((1, D))],
        out_specs=pl.BlockSpec((1, tt, D), row),
        out_shape=jax.ShapeDtypeStruct((B, T, D), F32),
        compiler_params=_cparams("parallel", "parallel"),
        name="mlp_ln",
    )(x, w['wup'], w['wdown'], w['ln2g'], w['ln2b'])


def _rot_cols(wr):
    half = ROPE_DIM // 2
    return jnp.concatenate([-wr[..., half:], wr[..., :half]], axis=-1)


def _prep_layer(p):
    D = p['w_in'].shape[0]
    w_in = p['w_in']
    o_kv = Q_RANK
    o_kr = Q_RANK + KV_RANK
    o_u = o_kr + ROPE_DIM
    o_p = o_u + S5_W
    w_kr = w_in[:, o_kr:o_u]
    wm = jnp.concatenate([w_in[:, :o_kr], w_kr, _rot_cols(w_kr),
                          jnp.zeros((D, V7X_LANES - 2 * ROPE_DIM), F32)], axis=1)
    wqb = p['w_qb'].reshape(Q_RANK, MLA_HEADS, NOPE_DIM + ROPE_DIM) * (MLA_SCALE * LOG2_E)
    zq = jnp.zeros((Q_RANK, MLA_HEADS, HEAD_PAD - NOPE_DIM - ROPE_DIM), F32)
    wq1 = jnp.concatenate([wqb, zq], axis=-1).reshape(Q_RANK, MLA_HEADS * HEAD_PAD)
    wq2 = jnp.concatenate([jnp.zeros((Q_RANK, MLA_HEADS, NOPE_DIM), F32), _rot_cols(wqb[..., NOPE_DIM:]), zq],
                          axis=-1).reshape(Q_RANK, MLA_HEADS * HEAD_PAD)
    wkvb = p['w_kvb'].reshape(KV_RANK, MLA_HEADS, NOPE_DIM + V_DIM)
    wk = jnp.concatenate([wkvb[..., :NOPE_DIM], jnp.zeros((KV_RANK, MLA_HEADS, HEAD_PAD - NOPE_DIM), F32)],
                         axis=-1).reshape(KV_RANK, MLA_HEADS * HEAD_PAD)
    pk = jnp.concatenate([jnp.zeros((ROPE_DIM, NOPE_DIM), F32), jnp.eye(ROPE_DIM, dtype=F32),
                          jnp.zeros((ROPE_DIM, HEAD_PAD - NOPE_DIM - ROPE_DIM), F32)], axis=1)
    pk = jnp.tile(pk, (1, MLA_HEADS))
    wvt = wkvb[..., NOPE_DIM:].reshape(KV_RANK, MLA_HEADS * V_DIM).T

    lr, li = p['lam_re'], p['lam_im']
    dt = jnp.exp(p['log_dt'])[:, None]
    mag = jnp.exp(lr * dt)
    ar, ai = mag * jnp.cos(li * dt), mag * jnp.sin(li * dt)
    den = lr * lr + li * li
    cr = ((ar - 1.0) * lr + ai * li) / den
    ci = (ai * lr - (ar - 1.0) * li) / den
    bbr = cr[..., None] * p['b_re'] - ci[..., None] * p['b_im']
    bbi = cr[..., None] * p['b_im'] + ci[..., None] * p['b_re']
    eye_g = jnp.eye(S5_GROUPS, dtype=F32)
    bd_in = lambda m: (m.transpose(0, 2, 1)[:, :, None, :] * eye_g[:, None, :, None]).reshape(S5_W, S5_N)
    bd_out = lambda m: (m.transpose(0, 2, 1)[:, :, None, :] * eye_g[:, None, :, None]).reshape(S5_N, S5_W)
    wb = jnp.concatenate([bd_in(bbr), bd_in(bbi)], axis=1)
    wc = jnp.concatenate([bd_out(p['c_re']), -bd_out(p['c_im'])], axis=0)
    a = jnp.stack([ar.reshape(S5_N), ai.reshape(S5_N)])

    def lora_pad(w, off):
        z = jnp.zeros((V7X_LANES, RWKV_W), F32)
        return z.at[off:off + w.shape[0]].set(w)

    seg = jnp.arange(RWKV_W) // RWKV_HEAD
    ones = (seg[:, None] == seg[None, :]).astype(F32)
    row = lambda v: v.reshape(1, -1).astype(F32)
    wo = p['w_out']
    return dict(
        inproj=dict(wm=wm.astype(BF16), wu=w_in[:, o_u:o_p].astype(BF16), wp=w_in[:, o_p:].astype(BF16),
                    wq1=wq1.astype(BF16), wq2=wq2.astype(BF16), qg=row(p['q_norm_g']), kg=row(p['kv_norm_g'])),
        kvup=dict(wk=wk.astype(BF16), pk=pk.astype(BF16), wvt=wvt.astype(BF16)),
        s5=dict(a=a, wb=wb.astype(BF16), wc=wc.astype(BF16), d=row(p['s5_d']), wglu=p['w_glu'].astype(BF16),
                bglu=row(p['b_glu'])),
        rwkv=dict(mu=row(p['mu_shift']), ww=lora_pad(p['w_w2'], 0).astype(BF16),
                  wa=lora_pad(p['w_a2'], DECAY_LORA).astype(BF16),
                  wg=lora_pad(p['w_g2'], DECAY_LORA + AAA_LORA).astype(BF16),
                  w0=row(p['w0']), a0=row(p['a0']), kk=row(p['k_k']), ka=row(p['k_a']), rk=row(p['r_k']),
                  gng=row(p['gn_g']), gnb=row(p['gn_b']), ones=ones.astype(BF16)),
        out=dict(wo_m=wo[:MLA_HEADS * V_DIM].astype(BF16),
                 wo_s=wo[MLA_HEADS * V_DIM:MLA_HEADS * V_DIM + S5_W].astype(BF16),
                 wo_r=wo[MLA_HEADS * V_DIM + S5_W:].astype(BF16), ln1g=row(p['ln1_g']), ln1b=row(p['ln1_b'])),
        mlp=dict(wup=p['w_up'].astype(BF16), wdown=p['w_down'].astype(BF16), ln2g=row(p['ln2_g']),
                 ln2b=row(p['ln2_b'])),
    )


def _rope_tables(pos):
    inv_freq = ROPE_BASE ** (-jnp.arange(0, ROPE_DIM, 2, dtype=F32) / ROPE_DIM)
    ang = pos.astype(F32)[:, None] * inv_freq[None, :]
    ang = jnp.concatenate([ang, ang], -1)
    cos, sin = jnp.cos(ang), jnp.sin(ang)
    T = pos.shape[0]
    z = lambda n: jnp.zeros((T, n), F32)
    return dict(
        csk=jnp.concatenate([cos, sin, z(V7X_LANES - 2 * ROPE_DIM)], axis=1),
        cq=jnp.concatenate([jnp.ones((T, NOPE_DIM), F32), cos, z(HEAD_PAD - NOPE_DIM - ROPE_DIM)], axis=1),
        sq=jnp.concatenate([z(NOPE_DIM), sin, z(HEAD_PAD - NOPE_DIM - ROPE_DIM)], axis=1),
    )


def _trunk_layer(x, tabs, past, s5_state, rwkv_state, shift0, w, alpha):
    B, T, D = x.shape
    qcat, ckv, krope, u_tm, p_tm = _inproj(x, w['inproj'], tabs)
    kcat, vt = _kvup(ckv, krope, w['kvup'])
    if past is None:
        mla = _attention(qcat, kcat, vt, kv_len=T, chunk_causal=True)
    else:
        kv_len = past[0].shape[1] + T
        bk = past[1].shape[4]
        kcat = jnp.concatenate([past[0], kcat], axis=1)
        kcat = jnp.pad(kcat, ((0, 0), (0, -kv_len % bk), (0, 0)))
        vt = jnp.pad(vt, ((0, 0), (0, 0), (0, 0), (0, 0), (0, bk - vt.shape[4])))
        vt = jnp.concatenate([past[1], vt], axis=2)
        mla = _attention(qcat, kcat, vt, kv_len=kv_len, chunk_causal=False)

    x0 = jnp.concatenate([s5_state[..., 0].reshape(B, S5_N), s5_state[..., 1].reshape(B, S5_N)], axis=1)
    s5_out, x_last = _s5(u_tm, x0, w['s5'])
    new_s5 = jnp.stack([x_last[:, :S5_N].reshape(B, S5_GROUPS, S5_STATE),
                        x_last[:, S5_N:].reshape(B, S5_GROUPS, S5_STATE)], axis=-1)

    rwkv, new_rwkv, shift = _rwkv_mixer(p_tm, shift0.reshape(B, RWKV_PROJ), rwkv_state, w['rwkv'])
    x = _outproj(mla, s5_out, rwkv, x, w['out'], w['rwkv'], alpha)
    x = _mlp(x, w['mlp'], alpha)
    return x, ckv, krope, new_s5, new_rwkv, shift.reshape(B, 1, RWKV_PROJ)


def kernel(x_prompt, x_sample, cache_mla_ckv, cache_mla_krope, state_s5, state_rwkv, state_rwkv_shift, w_in, q_norm_g, w_qb, kv_norm_g, w_kvb, lam_re, lam_im, log_dt, b_re, b_im, c_re, c_im, s5_d, w_glu, b_glu, mu_shift, w0, w_w2, a0, w_a2, w_g2, k_k, k_a, r_k, gn_g, gn_b, w_out, ln1_g, ln1_b, w_up, w_down, ln2_g, ln2_b):
    depth = w_in.shape[0]
    alpha = (2 * depth) ** 0.25
    bp, sp = x_prompt.shape[:2]
    ts = x_sample.shape[1]
    past = cache_mla_ckv.shape[2]
    tabs_p = _rope_tables(jnp.arange(sp))
    tabs_s = _rope_tables(past + jnp.arange(ts))
    names = dict(w_in=w_in, q_norm_g=q_norm_g, w_qb=w_qb, kv_norm_g=kv_norm_g, w_kvb=w_kvb, lam_re=lam_re,
                 lam_im=lam_im, log_dt=log_dt, b_re=b_re, b_im=b_im, c_re=c_re, c_im=c_im, s5_d=s5_d, w_glu=w_glu,
                 b_glu=b_glu, mu_shift=mu_shift, w0=w0, w_w2=w_w2, a0=a0, w_a2=w_a2, w_g2=w_g2, k_k=k_k, k_a=k_a,
                 r_k=r_k, gn_g=gn_g, gn_b=gn_b, w_out=w_out, ln1_g=ln1_g, ln1_b=ln1_b, w_up=w_up, w_down=w_down,
                 ln2_g=ln2_g, ln2_b=ln2_b)
    s5_zero = jnp.zeros((bp, S5_GROUPS, S5_STATE, 2), F32)
    rwkv_zero = jnp.zeros((bp, RWKV_HEADS, RWKV_HEAD, RWKV_HEAD), F32)
    shift_zero = jnp.zeros((bp, 1, RWKV_PROJ), F32)

    xp, xs = x_prompt, x_sample
    outs_p = [[] for _ in range(5)]
    outs_s = [[] for _ in range(5)]
    for l in range(depth):
        w = _prep_layer({k: v[l] for k, v in names.items()})
        xp, *rest = _trunk_layer(xp, tabs_p, None, s5_zero, rwkv_zero, shift_zero, w, alpha)
        for acc, val in zip(outs_p, rest):
            acc.append(val)
        past_kv = _kvup(cache_mla_ckv[l], cache_mla_krope[l], w['kvup'])
        xs, *rest = _trunk_layer(xs, tabs_s, past_kv, state_s5[l], state_rwkv[l], state_rwkv_shift[l], w, alpha)
        for acc, val in zip(outs_s, rest):
            acc.append(val)
    return (xp, xs, *[jnp.stack(a) for a in outs_p], *[jnp.stack(a) for a in outs_s])
```

```python
import functools
import math

import jax
import jax.numpy as jnp
from jax import lax
from jax.experimental import pallas as pl
from jax.experimental.pallas import tpu as pltpu

F32 = jnp.float32
BF16 = jnp.bfloat16

CHUNK = 64
MLA_HEADS = 6
NOPE_DIM = 64
ROPE_DIM = 32
V_DIM = 64
Q_RANK = 256
KV_RANK = 128
ROPE_BASE = 10000.0
MLA_SCALE = (NOPE_DIM + ROPE_DIM) ** -0.5
S5_GROUP_CH = 16
S5_W = 256
S5_GROUPS = S5_W // S5_GROUP_CH
S5_STATE = 64
S5_N = S5_GROUPS * S5_STATE
RWKV_HEADS = 6
RWKV_HEAD = 64
RWKV_W = RWKV_HEADS * RWKV_HEAD
DECAY_LORA = 32
AAA_LORA = 32
GATE_LORA = 64
RWKV_PROJ = 3 * RWKV_W + DECAY_LORA + AAA_LORA + GATE_LORA
LN_EPS = 1e-5
RMS_EPS = 1e-6
GN_EPS = 64e-5
NEG_INF = -1e30

V7X_LANES = 128
V7X_VMEM_BYTES = 64 * 1024 * 1024
VMEM_LIMIT_BYTES = V7X_VMEM_BYTES - 8 * 1024 * 1024
HEAD_PAD = V7X_LANES
ATTN_BLOCK = 512
ROW_TILE = 128
ATTN_HEAD_GROUP = 6
V_ROWS = V_DIM + 16
LOG2_E = 1.4426950408889634
RWKV_PACK = 8


def _cparams(*sem):
    return pltpu.CompilerParams(dimension_semantics=sem, vmem_limit_bytes=VMEM_LIMIT_BYTES)


def _const_spec(shape):
    zeros = (0,) * len(shape)
    return pl.BlockSpec(shape, lambda *_: zeros)


def _tile(n, target):
    if n <= target:
        return n
    t = target
    while n % t:
        t //= 2
    return t


def _layer_norm(y, g, b):
    mu = jnp.mean(y, -1, keepdims=True)
    yc = y - mu
    var = jnp.mean(yc * yc, -1, keepdims=True)
    return yc * lax.rsqrt(var + LN_EPS) * g + b


def _inproj_kernel(x_ref, wm_ref, wu_ref, wp_ref, wq1_ref, wq2_ref, qg_ref, kg_ref, csk_ref, cq_ref, sq_ref,
                   qcat_ref, ckv_ref, krope_ref, u_ref, p_ref):
    B, tt, D = x_ref.shape
    x = x_ref[...].reshape(B * tt, D).astype(BF16)
    per_b = lambda tab_ref: jnp.concatenate([tab_ref[...]] * B, axis=0)
    m = jnp.dot(x, wm_ref[...], preferred_element_type=F32)
    q_lat = m[:, :Q_RANK]
    qn = q_lat * lax.rsqrt(jnp.mean(q_lat * q_lat, -1, keepdims=True) + RMS_EPS) * qg_ref[...]
    kv_lat = m[:, Q_RANK:Q_RANK + KV_RANK]
    ckv = kv_lat * lax.rsqrt(jnp.mean(kv_lat * kv_lat, -1, keepdims=True) + RMS_EPS) * kg_ref[...]
    ckv_ref[...] = ckv.reshape(B, tt, KV_RANK)
    kr = m[:, Q_RANK + KV_RANK:] * per_b(csk_ref)
    kr = kr + pltpu.roll(kr, V7X_LANES - ROPE_DIM, 1)
    krope_ref[...] = kr[:, :ROPE_DIM].reshape(B, tt, ROPE_DIM)
    qb = qn.astype(BF16)
    q1 = jnp.dot(qb, wq1_ref[...], preferred_element_type=F32)
    q2 = jnp.dot(qb, wq2_ref[...], preferred_element_type=F32)
    cq = per_b(cq_ref)
    sq = per_b(sq_ref)
    for h in range(MLA_HEADS):
        sl = slice(h * HEAD_PAD, (h + 1) * HEAD_PAD)
        qcat_ref[:, :, sl] = (q1[:, sl] * cq + q2[:, sl] * sq).astype(BF16).reshape(B, tt, HEAD_PAD)
    u = jnp.dot(x, wu_ref[...], preferred_element_type=F32)
    p = jnp.dot(x, wp_ref[...], preferred_element_type=F32)
    for b in range(B):
        u_ref[:, b, :] = u[b * tt:(b + 1) * tt]
        p_ref[:, b, :] = p[b * tt:(b + 1) * tt]


def _inproj(x, w, tabs):
    B, T, D = x.shape
    tt = _tile(T, ROW_TILE)
    nq = MLA_HEADS * HEAD_PAD
    row = lambda i: (0, i, 0)
    tab = lambda i: (i, 0)
    tm = lambda i: (i, 0, 0)
    return pl.pallas_call(
        _inproj_kernel,
        grid=(T // tt,),
        in_specs=[
            pl.BlockSpec((B, tt, D), row),
            _const_spec(w['wm'].shape), _const_spec(w['wu'].shape), _const_spec(w['wp'].shape),
            _const_spec(w['wq1'].shape), _const_spec(w['wq2'].shape),
            _const_spec((1, Q_RANK)), _const_spec((1, KV_RANK)),
            pl.BlockSpec((tt, V7X_LANES), tab), pl.BlockSpec((tt, HEAD_PAD), tab), pl.BlockSpec((tt, HEAD_PAD), tab),
        ],
        out_specs=[
            pl.BlockSpec((B, tt, nq), row),
            pl.BlockSpec((B, tt, KV_RANK), row),
            pl.BlockSpec((B, tt, ROPE_DIM), row),
            pl.BlockSpec((tt, B, S5_W), tm),
            pl.BlockSpec((tt, B, RWKV_PROJ), tm),
        ],
        out_shape=[
            jax.ShapeDtypeStruct((B, T, nq), BF16),
            jax.ShapeDtypeStruct((B, T, KV_RANK), F32),
            jax.ShapeDtypeStruct((B, T, ROPE_DIM), F32),
            jax.ShapeDtypeStruct((T, B, S5_W), F32),
            jax.ShapeDtypeStruct((T, B, RWKV_PROJ), F32),
        ],
        compiler_params=_cparams("parallel"),
        name="inproj",
    )(x, w['wm'], w['wu'], w['wp'], w['wq1'], w['wq2'], w['qg'], w['kg'], tabs['csk'], tabs['cq'], tabs['sq'])


def _kvup_kernel(ckv_ref, kr_ref, wk_ref, pk_ref, wvt_ref, kcat_ref, vt_ref):
    c = ckv_ref[0].astype(BF16)
    kr = kr_ref[0].astype(BF16)
    kcat = jnp.dot(c, wk_ref[...], preferred_element_type=F32) + jnp.dot(kr, pk_ref[...], preferred_element_type=F32)
    kcat_ref[0] = kcat.astype(BF16)
    vt = lax.dot_general(wvt_ref[...], c, (((1,), (1,)), ((), ())), preferred_element_type=F32).astype(BF16)
    ones = jnp.ones((V_ROWS - V_DIM, vt.shape[1]), BF16)
    for h in range(MLA_HEADS):
        vt_ref[0, h, 0, :V_DIM, :] = vt[h * V_DIM:(h + 1) * V_DIM, :]
        vt_ref[0, h, 0, V_DIM:, :] = ones


def _kvup(ckv, krope, w):
    B, T, _ = ckv.shape
    tt = _tile(T, ATTN_BLOCK)
    nk = MLA_HEADS * HEAD_PAD
    row = lambda b, i: (b, i, 0)
    return pl.pallas_call(
        _kvup_kernel,
        grid=(B, T // tt),
        in_specs=[pl.BlockSpec((1, tt, KV_RANK), row), pl.BlockSpec((1, tt, ROPE_DIM), row),
                  _const_spec(w['wk'].shape), _const_spec(w['pk'].shape), _const_spec(w['wvt'].shape)],
        out_specs=[pl.BlockSpec((1, tt, nk), row),
                   pl.BlockSpec((1, MLA_HEADS, 1, V_ROWS, tt), lambda b, i: (b, 0, i, 0, 0))],
        out_shape=[jax.ShapeDtypeStruct((B, T, nk), BF16),
                   jax.ShapeDtypeStruct((B, MLA_HEADS, T // tt, V_ROWS, tt), BF16)],
        compiler_params=_cparams("parallel", "parallel"),
        name="kvup",
    )(ckv, krope, w['wk'], w['pk'], w['wvt'])


def _attn_kernel(q_ref, k_ref, v_ref, *rest, bq, bk, kv_len, chunk_causal, extra):
    if extra:
        kx_ref, vx_ref, o_ref, sa_ref, sb_ref, bias_ref, m_ref, acc_ref = rest
    else:
        o_ref, sa_ref, sb_ref, bias_ref, m_ref, acc_ref = rest
    hg = ATTN_HEAD_GROUP
    qi = pl.program_id(2)
    last = qi if chunk_causal else jnp.int32(-(-kv_len // bk) - 1)
    mask_last = chunk_causal or kv_len % bk != 0
    nt = (((1,), (1,)), ((), ()))

    def scores(j, s_ref, h):
        r0 = pl.multiple_of(j * bk, bk)
        sl = slice(h * HEAD_PAD, (h + 1) * HEAD_PAD)
        s_ref[h] = lax.dot_general(k_ref[0, pl.ds(r0, bk), sl], q_ref[0, :, sl], nt,
                                   preferred_element_type=F32)

    def online_softmax(s, v, h):
        m = m_ref[h]
        m_new = jnp.maximum(m, jnp.max(s, axis=0, keepdims=True))
        p = jnp.exp2(s - m_new)
        alpha = jnp.exp2(m - m_new)
        acc_ref[h] = alpha * acc_ref[h] + jnp.dot(v, p.astype(BF16), preferred_element_type=F32)
        m_ref[h] = m_new

    def softmax_pv(j, s_ref, h, masked):
        s = s_ref[h]
        if masked:
            s = s + bias_ref[...]
        online_softmax(s, v_ref[0, h, j], h)

    @pl.when(qi == 0)
    def _():
        krel = lax.broadcasted_iota(jnp.int32, (bk, bq), 0)
        if chunk_causal:
            qrel = lax.broadcasted_iota(jnp.int32, (bk, bq), 1)
            ok = (krel // CHUNK) <= (qrel // CHUNK)
        else:
            ok = krel < kv_len - (-(-kv_len // bk) - 1) * bk
        bias_ref[...] = jnp.where(ok, 0.0, NEG_INF)

    def advance(j_next, s_next, j, s_cur, masked=False):
        for h in range(hg):
            if j_next is not None:
                scores(j_next, s_next, h)
            softmax_pv(j, s_cur, h, masked)

    m_ref[...] = jnp.full(m_ref.shape, NEG_INF, F32)
    acc_ref[...] = jnp.zeros(acc_ref.shape, F32)
    for h in range(hg):
        scores(0, sa_ref, h)

    def pair(p, _):
        j = 2 * p
        advance(j + 1, sb_ref, j, sa_ref)
        advance(j + 2, sa_ref, j + 1, sb_ref)
        return 0

    lax.fori_loop(0, last // 2, pair, 0)

    @pl.when(last % 2 == 0)
    def _():
        advance(None, None, last, sa_ref, mask_last)

    @pl.when(last % 2 == 1)
    def _():
        advance(last, sb_ref, last - 1, sa_ref)
        advance(None, None, last, sb_ref, mask_last)

    if extra:
        for h in range(hg):
            sl = slice(h * HEAD_PAD, (h + 1) * HEAD_PAD)
            s = lax.dot_general(kx_ref[0, :, sl], q_ref[0, :, sl], nt, preferred_element_type=F32)
            online_softmax(s, vx_ref[0, h, 0], h)

    for h in range(hg):
        o_ref[0, h] = (acc_ref[h, :V_DIM, :] / acc_ref[h, V_DIM:V_DIM + 1, :]).astype(BF16)


def _attention(qcat, kcat, vt, *, kv_len, chunk_causal, extra=None):
    B, Tq, _ = qcat.shape
    Tk = kcat.shape[1]
    nblk, bk = vt.shape[2], vt.shape[4]
    assert nblk * bk == Tk
    bq = bk if chunk_causal else Tq
    hg = ATTN_HEAD_GROUP
    in_specs = [
        pl.BlockSpec((1, bq, hg * HEAD_PAD), lambda b, g, i: (b, i, g)),
        pl.BlockSpec((1, Tk, hg * HEAD_PAD), lambda b, g, i: (b, 0, g), pipeline_mode=pl.Buffered(1)),
        pl.BlockSpec((1, hg, nblk, V_ROWS, bk), lambda b, g, i: (b, g, 0, 0, 0), pipeline_mode=pl.Buffered(1)),
    ]
    operands = [qcat, kcat, vt]
    if extra is not None:
        tx = extra[0].shape[1]
        in_specs += [pl.BlockSpec((1, tx, hg * HEAD_PAD), lambda b, g, i: (b, 0, g)),
                     pl.BlockSpec((1, hg, 1, V_ROWS, tx), lambda b, g, i: (b, g, 0, 0, 0))]
        operands += list(extra)
    out_t = pl.pallas_call(
        functools.partial(_attn_kernel, bq=bq, bk=bk, kv_len=kv_len, chunk_causal=chunk_causal,
                          extra=extra is not None),
        grid=(B, MLA_HEADS // hg, Tq // bq),
        in_specs=in_specs,
        out_specs=pl.BlockSpec((1, hg, V_DIM, bq), lambda b, g, i: (b, g, 0, i)),
        out_shape=jax.ShapeDtypeStruct((B, MLA_HEADS, V_DIM, Tq), BF16),
        scratch_shapes=[pltpu.VMEM((hg, bk, bq), F32), pltpu.VMEM((hg, bk, bq), F32), pltpu.VMEM((bk, bq), F32),
                        pltpu.VMEM((hg, 1, bq), F32), pltpu.VMEM((hg, V_ROWS, bq), F32)],
        compiler_params=_cparams("parallel", "parallel", "arbitrary"),
        name="mla_attention",
    )(*operands)
    return out_t


def _gelu_tanh(x):
    return 0.5 * x * (1.0 + jnp.tanh(math.sqrt(2.0 / math.pi) * (x + 0.044715 * (x * x * x))))


def _s5_kernel(u_ref, x0_ref, a_ref, wb_ref, wc_ref, d_ref, wg_ref, bg_ref, o_ref, xl_ref, xs_ref, st_ref, *, L, B):
    @pl.when(pl.program_id(0) == 0)
    def _():
        st_ref[...] = x0_ref[...]

    u = u_ref[...].reshape(L * B, S5_W)
    xs_ref[...] = jnp.dot(u.astype(BF16), wb_ref[...], preferred_element_type=F32)
    a_re = a_ref[0:1, :]
    a_im = a_ref[1:2, :]

    def body(t, carry):
        x_re, x_im = carry
        r0 = pl.multiple_of(t * B, B)
        n_re = a_re * x_re - a_im * x_im + xs_ref[pl.ds(r0, B), :S5_N]
        n_im = a_re * x_im + a_im * x_re + xs_ref[pl.ds(r0, B), S5_N:]
        xs_ref[pl.ds(r0, B), :S5_N] = n_re
        xs_ref[pl.ds(r0, B), S5_N:] = n_im
        return n_re, n_im

    x_re, x_im = lax.fori_loop(0, L, body, (st_ref[:, :S5_N], st_ref[:, S5_N:]), unroll=8)
    st_ref[:, :S5_N] = x_re
    st_ref[:, S5_N:] = x_im
    xl_ref[...] = st_ref[...]

    y = jnp.dot(xs_ref[...].astype(BF16), wc_ref[...], preferred_element_type=F32) + d_ref[...] * u
    z = _gelu_tanh(y)
    gate = jax.nn.sigmoid(jnp.dot(z.astype(BF16), wg_ref[...], preferred_element_type=F32) + bg_ref[...])
    o_ref[...] = (z * gate).reshape(L, B, S5_W)


def _s5(u_tm, x0, w):
    T, B, _ = u_tm.shape
    L = _tile(T, 1024 // B)
    return pl.pallas_call(
        functools.partial(_s5_kernel, L=L, B=B),
        grid=(T // L,),
        in_specs=[
            pl.BlockSpec((L, B, S5_W), lambda i: (i, 0, 0)),
            _const_spec((B, 2 * S5_N)), _const_spec((2, S5_N)),
            _const_spec((S5_W, 2 * S5_N)), _const_spec((2 * S5_N, S5_W)),
            _const_spec((1, S5_W)), _const_spec((S5_W, S5_W)), _const_spec((1, S5_W)),
        ],
        out_specs=[pl.BlockSpec((L, B, S5_W), lambda i: (i, 0, 0)), _const_spec((B, 2 * S5_N))],
        out_shape=[jax.ShapeDtypeStruct((T, B, S5_W), F32), jax.ShapeDtypeStruct((B, 2 * S5_N), F32)],
        scratch_shapes=[pltpu.VMEM((L * B, 2 * S5_N), F32), pltpu.VMEM((B, 2 * S5_N), F32)],
        compiler_params=_cparams("arbitrary"),
        name="s5_scan",
    )(u_tm, x0, w['a'], w['wb'], w['wc'], w['d'], w['wglu'], w['bglu'])


def _softplus(x):
    return jnp.maximum(x, 0.0) + jnp.log(1.0 + jnp.exp(-jnp.abs(x)))


def _head_sum(x, ones_ref):
    hi = x.astype(BF16)
    lo = (x - hi.astype(F32)).astype(BF16)
    ones = ones_ref[...]
    return jnp.dot(hi, ones, preferred_element_type=F32) + jnp.dot(lo, ones, preferred_element_type=F32)


def _pack_rows(x, o_ref, zs_ref, B):
    rows = RWKV_PACK * B
    for m in range(x.shape[0] // rows):
        xm = x[m * rows:(m + 1) * rows]
        for h in range(RWKV_HEADS):
            c = h * RWKV_HEAD // V7X_LANES
            piece = xm[:, c * V7X_LANES:(c + 1) * V7X_LANES]
            if (h * RWKV_HEAD) % V7X_LANES:
                piece = pltpu.roll(piece, V7X_LANES - (h * RWKV_HEAD) % V7X_LANES, 1)
            zs_ref[pl.ds(h, rows, stride=RWKV_HEADS), :] = piece
        o_ref[m] = zs_ref[...].T[:RWKV_HEAD, :]


def _rwkv_prep_kernel(p_ref, sh0_ref, mu_ref, ww_ref, wa_ref, wg_ref, w0_ref, a0_ref, ka_ref, rk_ref,
                      ones_ref, r_o, w_o, k_o, v_o, a_o, g_o, bon_o, sh_o, last_ref, zs_ref, *, L, B):
    @pl.when(pl.program_id(0) == 0)
    def _():
        last_ref[...] = sh0_ref[...]

    p = p_ref[...].reshape(L * B, RWKV_PROJ)
    if L > 1:
        prev = jnp.concatenate([last_ref[...], p[:(L - 1) * B]], axis=0)
    else:
        prev = last_ref[...]
    last_ref[...] = p[(L - 1) * B:]
    sh_o[...] = p[(L - 1) * B:]
    ps = p + (prev - p) * mu_ref[...]
    r = ps[:, :RWKV_W]
    k = ps[:, RWKV_W:2 * RWKV_W]
    v = ps[:, 2 * RWKV_W:3 * RWKV_W]
    tail = ps[:, 3 * RWKV_W:]
    lw = jnp.dot(jnp.tanh(tail).astype(BF16), ww_ref[...], preferred_element_type=F32)
    la = jnp.dot(tail.astype(BF16), wa_ref[...], preferred_element_type=F32)
    g = jnp.dot(jax.nn.sigmoid(tail).astype(BF16), wg_ref[...], preferred_element_type=F32)
    w_log = -_softplus(-(w0_ref[...] + lw)) - 0.5
    decay = jnp.exp(-jnp.exp(w_log))
    a = jax.nn.sigmoid(a0_ref[...] + la)
    k2 = k * (1.0 + (a - 1.0) * ka_ref[...])
    bonus = _head_sum(r * k2 * rk_ref[...], ones_ref) * v
    for val, o_ref in ((r, r_o), (decay, w_o), (k, k_o), (v, v_o), (a, a_o)):
        _pack_rows(val, o_ref, zs_ref, B)
    shp = (L, B, RWKV_W)
    g_o[...] = g.reshape(shp)
    bon_o[...] = bonus.reshape(shp)


def _rwkv_prep(p_tm, shift0, w):
    T, B, _ = p_tm.shape
    L = _tile(T, 512 // B)
    assert L % RWKV_PACK == 0
    blk = pl.BlockSpec((L, B, RWKV_W), lambda i: (i, 0, 0))
    W = RWKV_PACK * B * RWKV_HEADS
    pblk = pl.BlockSpec((L // RWKV_PACK, RWKV_HEAD, W), lambda i: (i, 0, 0))
    packed = jax.ShapeDtypeStruct((T // RWKV_PACK, RWKV_HEAD, W), F32)
    vec = _const_spec((1, RWKV_W))
    outs = pl.pallas_call(
        functools.partial(_rwkv_prep_kernel, L=L, B=B),
        grid=(T // L,),
        in_specs=[pl.BlockSpec((L, B, RWKV_PROJ), lambda i: (i, 0, 0)), _const_spec((B, RWKV_PROJ)),
                  _const_spec((1, RWKV_PROJ)),
                  _const_spec((V7X_LANES, RWKV_W)), _const_spec((V7X_LANES, RWKV_W)), _const_spec((V7X_LANES, RWKV_W)),
                  vec, vec, vec, vec, _const_spec((RWKV_W, RWKV_W))],
        out_specs=[pblk] * 5 + [blk] * 2 + [_const_spec((B, RWKV_PROJ))],
        out_shape=[packed] * 5 + [jax.ShapeDtypeStruct((T, B, RWKV_W), F32)] * 2
        + [jax.ShapeDtypeStruct((B, RWKV_PROJ), F32)],
        scratch_shapes=[pltpu.VMEM((B, RWKV_PROJ), F32), pltpu.VMEM((RWKV_PACK * B * RWKV_HEADS, V7X_LANES), F32)],
        compiler_params=_cparams("arbitrary"),
        name="rwkv_prep",
    )(p_tm, shift0, w['mu'], w['ww'], w['wa'], w['wg'], w['w0'], w['a0'], w['ka'], w['rk'], w['ones'])
    return outs


def _lane_window(tile, off, width, dst, lane):
    v0, lo = divmod(off, V7X_LANES)
    shift = (dst - lo) % V7X_LANES
    a = tile(v0)
    r = pltpu.roll(a, shift, 1) if shift else a
    if lo + width > V7X_LANES:
        b = tile(v0 + 1)
        r2 = pltpu.roll(b, shift, 1) if shift else b
        r = jnp.where(lane < dst + (V7X_LANES - lo), r, r2)
    return r


def _chain_lanes(pieces, bh, lane):
    rep = len(pieces)
    out = jnp.where(lane < rep * bh, pieces[rep - 1], 0.0)
    for r in range(rep - 2, -1, -1):
        out = jnp.where(lane < (r + 1) * bh, pieces[r], out)
    return out


def _rwkv_scan_kernel(w_ref, a_ref, k_ref, kx_ref, r_ref, v_ref, kkt_ref, kat_ref, s0_ref, y_ref, so_ref,
                      s_ref, sa_ref, e_ref, ev_ref, ys_ref, kall_ref, *, Tm, IL, bh, rep):
    lane = lax.broadcasted_iota(jnp.int32, (RWKV_HEAD, V7X_LANES), 1)
    lane_il = lax.broadcasted_iota(jnp.int32, (IL, V7X_LANES), 1)
    ncol = RWKV_PACK * bh // V7X_LANES

    def tiles(slab, rows=slice(None)):
        return lambda c: slab[rows, c * V7X_LANES:(c + 1) * V7X_LANES]

    def expand(slab, t8):
        return _chain_lanes([_lane_window(tiles(slab), t8 * bh, bh, r * bh, lane) for r in range(rep)], bh, lane)

    def unit_kk(k_raw):
        kk = k_raw * kkt_ref[...]
        nrm = jnp.sqrt(jnp.sum(kk * kk, axis=0, keepdims=True))
        return kk / jnp.maximum(nrm, 1e-12)

    kall_ref[0:Tm] = k_ref[...]
    kall_ref[Tm] = kx_ref[0]

    @pl.when(pl.program_id(0) == 0)
    def _():
        s_ref[...] = s0_ref[...]
        e_ref[0, 4] = expand(-unit_kk(k_ref[0]), 0)
        acc = jnp.zeros((IL, V7X_LANES), F32)
        for j in range(RWKV_HEAD):
            acc = acc + s0_ref[j] * e_ref[0, 4, j:j + 1, :]
        sa_ref[...] = acc

    def body(m, sa):
        k_raw = kall_ref[m]
        a_s = a_ref[m]
        kkn = unit_kk(k_raw)
        slabs = [w_ref[m], kkn * a_s, k_raw * (1.0 + (a_s - 1.0) * kat_ref[...]), r_ref[m]]
        nn0 = -kkn
        nn1 = -unit_kk(kall_ref[m + 1])
        vs = v_ref[m]
        for t8 in range(RWKV_PACK):
            for o, slab in enumerate(slabs):
                e_ref[t8, o] = expand(slab, t8)
            e_ref[t8, 4] = expand(nn0, t8 + 1) if t8 + 1 < RWKV_PACK else expand(nn1, 0)
            ev_ref[t8] = _chain_lanes([_lane_window(tiles(vs, slice(r * IL, (r + 1) * IL)), t8 * bh, bh, r * bh,
                                                    lane_il) for r in range(rep)], bh, lane_il)
        for t8 in range(RWKV_PACK):
            vt = ev_ref[t8]
            yacc = jnp.zeros((IL, V7X_LANES), F32)
            san = jnp.zeros((IL, V7X_LANES), F32)
            for j in range(RWKV_HEAD):
                row = lambda o: e_ref[t8, o, j:j + 1, :]
                sn = s_ref[j] * row(0) + sa * row(1) + vt * row(2)
                s_ref[j] = sn
                yacc = yacc + sn * row(3)
                san = san + sn * row(4)
            sa = san
            ys_ref[t8] = yacc
        for c in range(ncol):
            for r in range(rep):
                col = jnp.zeros((IL, V7X_LANES), F32)
                for t8 in range(RWKV_PACK):
                    off = t8 * bh
                    lo, hi = max(off, c * V7X_LANES), min(off + bh, (c + 1) * V7X_LANES)
                    if lo < hi:
                        shift = (off - r * bh) % V7X_LANES
                        rolled = pltpu.roll(ys_ref[t8], shift, 1) if shift else ys_ref[t8]
                        inside = jnp.logical_and(lane_il >= lo - c * V7X_LANES, lane_il < hi - c * V7X_LANES)
                        col = jnp.where(inside, rolled, col)
                y_ref[m, r * IL:(r + 1) * IL, c * V7X_LANES:(c + 1) * V7X_LANES] = col
        return sa

    sa_ref[...] = lax.fori_loop(0, Tm, body, sa_ref[...])
    so_ref[...] = s_ref[...]


def _rwkv_scan(w_p, a_p, k_p, r_p, v_p, kk_t, ka_t, s0_l, bh, rep):
    TP, _, W = v_p.shape
    IL = RWKV_HEAD // rep
    Tm = _tile(TP, 8)
    blk = pl.BlockSpec((Tm, RWKV_HEAD, W), lambda i: (i, 0, 0))
    nxt = pl.BlockSpec((1, RWKV_HEAD, W), lambda i: (jnp.minimum((i + 1) * Tm, TP - 1), 0, 0))
    sblk = _const_spec((RWKV_HEAD, IL, V7X_LANES))
    par = _const_spec((RWKV_HEAD, W))
    return pl.pallas_call(
        functools.partial(_rwkv_scan_kernel, Tm=Tm, IL=IL, bh=bh, rep=rep),
        grid=(TP // Tm,),
        in_specs=[blk, blk, blk, nxt, blk, blk, par, par, sblk],
        out_specs=[blk, sblk],
        out_shape=[jax.ShapeDtypeStruct((TP, RWKV_HEAD, W), F32),
                   jax.ShapeDtypeStruct((RWKV_HEAD, IL, V7X_LANES), F32)],
        scratch_shapes=[pltpu.VMEM((RWKV_HEAD, IL, V7X_LANES), F32), pltpu.VMEM((IL, V7X_LANES), F32),
                        pltpu.VMEM((RWKV_PACK, 5, RWKV_HEAD, V7X_LANES), F32),
                        pltpu.VMEM((RWKV_PACK, IL, V7X_LANES), F32), pltpu.VMEM((RWKV_PACK, IL, V7X_LANES), F32),
                        pltpu.VMEM((Tm + 1, RWKV_HEAD, W), F32)],
        compiler_params=_cparams("arbitrary"),
        name="rwkv_scan",
    )(w_p, a_p, k_p, k_p, r_p, v_p, kk_t, ka_t, s0_l)


def _rwkv_layout(B):
    bh = B * RWKV_HEADS
    rep = 1
    while 2 * rep * bh <= V7X_LANES and RWKV_HEAD % (2 * rep) == 0:
        rep *= 2
    return bh, rep, RWKV_HEAD // rep


def _rwkv_mixer(p_tm, shift0, s0, w):
    T, B, _ = p_tm.shape
    bh, rep, IL = _rwkv_layout(B)
    lanes = rep * bh
    r_p, dec_p, k_p, v_p, a_p, g, bonus, shift = _rwkv_prep(p_tm, shift0, w)
    TP = T // RWKV_PACK

    def par_tile(vec):
        t = vec.reshape(RWKV_HEADS, RWKV_HEAD).T[:, None, None, :]
        return jnp.broadcast_to(t, (RWKV_HEAD, RWKV_PACK, B, RWKV_HEADS)).reshape(RWKV_HEAD, RWKV_PACK * bh)

    s0_l = s0.reshape(bh, rep, IL, RWKV_HEAD).transpose(3, 2, 1, 0).reshape(RWKV_HEAD, IL, lanes)
    s0_l = jnp.pad(s0_l, ((0, 0), (0, 0), (0, V7X_LANES - lanes)))
    y_p, s_l = _rwkv_scan(dec_p, a_p, k_p, r_p, v_p, par_tile(w['kk']), par_tile(w['ka']), s0_l, bh, rep)
    y_tm = y_p.reshape(TP, RWKV_HEAD, RWKV_PACK, bh).transpose(0, 2, 3, 1).reshape(T, B, RWKV_W)
    s_last = s_l[:, :, :lanes].reshape(RWKV_HEAD, IL, rep, bh).transpose(3, 2, 1, 0)
    s_last = s_last.reshape(B, RWKV_HEADS, RWKV_HEAD, RWKV_HEAD)
    return (y_tm, bonus, g), s_last, shift


def _outproj_kernel(m_ref, s_ref, y_ref, bon_ref, gate_ref, x_ref, wm_ref, ws_ref, wr_ref, gng_ref, gnb_ref,
                    ones_ref, g_ref, b_ref, o_ref, *, alpha):
    B, tt, D = x_ref.shape
    mla = jnp.concatenate([m_ref[b].reshape(MLA_HEADS * V_DIM, tt).T for b in range(B)], axis=0)
    rows = lambda ref: jnp.concatenate([ref[:, b, :] for b in range(B)], axis=0)
    acc = jnp.dot(mla, wm_ref[...], preferred_element_type=F32)
    acc = acc + jnp.dot(rows(s_ref).astype(BF16), ws_ref[...], preferred_element_type=F32)
    y = rows(y_ref)
    inv_n = 1.0 / RWKV_HEAD
    mu = _head_sum(y, ones_ref) * inv_n
    yc = y - mu
    var = _head_sum(yc * yc, ones_ref) * inv_n
    yn = yc * lax.rsqrt(var + GN_EPS) * gng_ref[...] + gnb_ref[...]
    rw = (yn + rows(bon_ref)) * rows(gate_ref)
    acc = acc + jnp.dot(rw.astype(BF16), wr_ref[...], preferred_element_type=F32)
    out = _layer_norm(alpha * x_ref[...].reshape(B * tt, D) + acc, g_ref[...], b_ref[...])
    o_ref[...] = out.reshape(B, tt, D)


def _outproj(mla_t, s5_tm, rwkv, x, w, wr, alpha):
    B, T, D = x.shape
    tt = _tile(T, ROW_TILE)
    row = lambda i: (0, i, 0)
    tm = lambda i: (i, 0, 0)
    rblk = pl.BlockSpec((tt, B, RWKV_W), tm)
    vec = _const_spec((1, RWKV_W))
    return pl.pallas_call(
        functools.partial(_outproj_kernel, alpha=alpha),
        grid=(T // tt,),
        in_specs=[pl.BlockSpec((B, MLA_HEADS, V_DIM, tt), lambda i: (0, 0, 0, i)), pl.BlockSpec((tt, B, S5_W), tm),
                  rblk, rblk, rblk, pl.BlockSpec((B, tt, D), row),
                  _const_spec(w['wo_m'].shape), _const_spec(w['wo_s'].shape), _const_spec(w['wo_r'].shape),
                  vec, vec, _const_spec((RWKV_W, RWKV_W)), _const_spec((1, D)), _const_spec((1, D))],
        out_specs=pl.BlockSpec((B, tt, D), row),
        out_shape=jax.ShapeDtypeStruct((B, T, D), F32),
        compiler_params=_cparams("parallel"),
        name="outproj_ln",
    )(mla_t, s5_tm, *rwkv, x, w['wo_m'], w['wo_s'], w['wo_r'], wr['gng'], wr['gnb'], wr['ones'], w['ln1g'],
      w['ln1b'])


def _mlp_kernel(x_ref, wu_ref, wd_ref, g_ref, b_ref, o_ref, *, alpha, fc):
    x = x_ref[0]
    xb = x.astype(BF16)
    acc = alpha * x
    for c in range(wu_ref.shape[1] // fc):
        h = jnp.maximum(jnp.dot(xb, wu_ref[:, c * fc:(c + 1) * fc], preferred_element_type=F32), 0.0)
        acc = acc + jnp.dot((h * h).astype(BF16), wd_ref[c * fc:(c + 1) * fc, :], preferred_element_type=F32)
    o_ref[0] = _layer_norm(acc, g_ref[...], b_ref[...])


def _mlp(x, w, alpha):
    B, T, D = x.shape
    tt = _tile(T, 512)
    row = lambda b, i: (b, i, 0)
    return pl.pallas_call(
        functools.partial(_mlp_kernel, alpha=alpha, fc=1024),
        grid=(B, T // tt),
        in_specs=[pl.BlockSpec((1, tt, D), row), _const_spec(w['wup'].shape), _const_spec(w['wdown'].shape),
                  _const_spec((1, D)), _const_spec((1, D))],
        out_specs=pl.BlockSpec((1, tt, D), row),
        out_shape=jax.ShapeDtypeStruct((B, T, D), F32),
        compiler_params=_cparams("parallel", "parallel"),
        name="mlp_ln",
    )(x, w['wup'], w['wdown'], w['ln2g'], w['ln2b'])


def _rot_cols(wr):
    half = ROPE_DIM // 2
    return jnp.concatenate([-wr[..., half:], wr[..., :half]], axis=-1)


def _prep_layer(p):
    D = p['w_in'].shape[0]
    w_in = p['w_in']
    o_kv = Q_RANK
    o_kr = Q_RANK + KV_RANK
    o_u = o_kr + ROPE_DIM
    o_p = o_u + S5_W
    w_kr = w_in[:, o_kr:o_u]
    wm = jnp.concatenate([w_in[:, :o_kr], w_kr, _rot_cols(w_kr),
                          jnp.zeros((D, V7X_LANES - 2 * ROPE_DIM), F32)], axis=1)
    wqb = p['w_qb'].reshape(Q_RANK, MLA_HEADS, NOPE_DIM + ROPE_DIM) * (MLA_SCALE * LOG2_E)
    zq = jnp.zeros((Q_RANK, MLA_HEADS, HEAD_PAD - NOPE_DIM - ROPE_DIM), F32)
    wq1 = jnp.concatenate([wqb, zq], axis=-1).reshape(Q_RANK, MLA_HEADS * HEAD_PAD)
    wq2 = jnp.concatenate([jnp.zeros((Q_RANK, MLA_HEADS, NOPE_DIM), F32), _rot_cols(wqb[..., NOPE_DIM:]), zq],
                          axis=-1).reshape(Q_RANK, MLA_HEADS * HEAD_PAD)
    wkvb = p['w_kvb'].reshape(KV_RANK, MLA_HEADS, NOPE_DIM + V_DIM)
    wk = jnp.concatenate([wkvb[..., :NOPE_DIM], jnp.zeros((KV_RANK, MLA_HEADS, HEAD_PAD - NOPE_DIM), F32)],
                         axis=-1).reshape(KV_RANK, MLA_HEADS * HEAD_PAD)
    pk = jnp.concatenate([jnp.zeros((ROPE_DIM, NOPE_DIM), F32), jnp.eye(ROPE_DIM, dtype=F32),
                          jnp.zeros((ROPE_DIM, HEAD_PAD - NOPE_DIM - ROPE_DIM), F32)], axis=1)
    pk = jnp.tile(pk, (1, MLA_HEADS))
    wvt = wkvb[..., NOPE_DIM:].reshape(KV_RANK, MLA_HEADS * V_DIM).T

    lr, li = p['lam_re'], p['lam_im']
    dt = jnp.exp(p['log_dt'])[:, None]
    mag = jnp.exp(lr * dt)
    ar, ai = mag * jnp.cos(li * dt), mag * jnp.sin(li * dt)
    den = lr * lr + li * li
    cr = ((ar - 1.0) * lr + ai * li) / den
    ci = (ai * lr - (ar - 1.0) * li) / den
    bbr = cr[..., None] * p['b_re'] - ci[..., None] * p['b_im']
    bbi = cr[..., None] * p['b_im'] + ci[..., None] * p['b_re']
    eye_g = jnp.eye(S5_GROUPS, dtype=F32)
    bd_in = lambda m: (m.transpose(0, 2, 1)[:, :, None, :] * eye_g[:, None, :, None]).reshape(S5_W, S5_N)
    bd_out = lambda m: (m.transpose(0, 2, 1)[:, :, None, :] * eye_g[:, None, :, None]).reshape(S5_N, S5_W)
    wb = jnp.concatenate([bd_in(bbr), bd_in(bbi)], axis=1)
    wc = jnp.concatenate([bd_out(p['c_re']), -bd_out(p['c_im'])], axis=0)
    a = jnp.stack([ar.reshape(S5_N), ai.reshape(S5_N)])

    def lora_pad(w, off):
        z = jnp.zeros((V7X_LANES, RWKV_W), F32)
        return z.at[off:off + w.shape[0]].set(w)

    seg = jnp.arange(RWKV_W) // RWKV_HEAD
    ones = (seg[:, None] == seg[None, :]).astype(F32)
    row = lambda v: v.reshape(1, -1).astype(F32)
    wo = p['w_out']
    return dict(
        inproj=dict(wm=wm.astype(BF16), wu=w_in[:, o_u:o_p].astype(BF16), wp=w_in[:, o_p:].astype(BF16),
                    wq1=wq1.astype(BF16), wq2=wq2.astype(BF16), qg=row(p['q_norm_g']), kg=row(p['kv_norm_g'])),
        kvup=dict(wk=wk.astype(BF16), pk=pk.astype(BF16), wvt=wvt.astype(BF16)),
        s5=dict(a=a, wb=wb.astype(BF16), wc=wc.astype(BF16), d=row(p['s5_d']), wglu=p['w_glu'].astype(BF16),
                bglu=row(p['b_glu'])),
        rwkv=dict(mu=row(p['mu_shift']), ww=lora_pad(p['w_w2'], 0).astype(BF16),
                  wa=lora_pad(p['w_a2'], DECAY_LORA).astype(BF16),
                  wg=lora_pad(p['w_g2'], DECAY_LORA + AAA_LORA).astype(BF16),
                  w0=row(p['w0']), a0=row(p['a0']), kk=row(p['k_k']), ka=row(p['k_a']), rk=row(p['r_k']),
                  gng=row(p['gn_g']), gnb=row(p['gn_b']), ones=ones.astype(BF16)),
        out=dict(wo_m=wo[:MLA_HEADS * V_DIM].astype(BF16),
                 wo_s=wo[MLA_HEADS * V_DIM:MLA_HEADS * V_DIM + S5_W].astype(BF16),
                 wo_r=wo[MLA_HEADS * V_DIM + S5_W:].astype(BF16), ln1g=row(p['ln1_g']), ln1b=row(p['ln1_b'])),
        mlp=dict(wup=p['w_up'].astype(BF16), wdown=p['w_down'].astype(BF16), ln2g=row(p['ln2_g']),
                 ln2b=row(p['ln2_b'])),
    )


def _rope_tables(pos):
    inv_freq = ROPE_BASE ** (-jnp.arange(0, ROPE_DIM, 2, dtype=F32) / ROPE_DIM)
    ang = pos.astype(F32)[:, None] * inv_freq[None, :]
    ang = jnp.concatenate([ang, ang], -1)
    cos, sin = jnp.cos(ang), jnp.sin(ang)
    T = pos.shape[0]
    z = lambda n: jnp.zeros((T, n), F32)
    return dict(
        csk=jnp.concatenate([cos, sin, z(V7X_LANES - 2 * ROPE_DIM)], axis=1),
        cq=jnp.concatenate([jnp.ones((T, NOPE_DIM), F32), cos, z(HEAD_PAD - NOPE_DIM - ROPE_DIM)], axis=1),
        sq=jnp.concatenate([z(NOPE_DIM), sin, z(HEAD_PAD - NOPE_DIM - ROPE_DIM)], axis=1),
    )


def _trunk_layer(x, tabs, past, s5_state, rwkv_state, shift0, w, alpha):
    B, T, D = x.shape
    qcat, ckv, krope, u_tm, p_tm = _inproj(x, w['inproj'], tabs)
    kcat, vt = _kvup(ckv, krope, w['kvup'])
    if past is None:
        mla = _attention(qcat, kcat, vt, kv_len=T, chunk_causal=True)
    else:
        assert vt.shape[2] == 1
        mla = _attention(qcat, past[0], past[1], kv_len=past[0].shape[1], chunk_causal=False, extra=(kcat, vt))

    x0 = jnp.concatenate([s5_state[..., 0].reshape(B, S5_N), s5_state[..., 1].reshape(B, S5_N)], axis=1)
    s5_out, x_last = _s5(u_tm, x0, w['s5'])
    new_s5 = jnp.stack([x_last[:, :S5_N].reshape(B, S5_GROUPS, S5_STATE),
                        x_last[:, S5_N:].reshape(B, S5_GROUPS, S5_STATE)], axis=-1)

    rwkv, new_rwkv, shift = _rwkv_mixer(p_tm, shift0.reshape(B, RWKV_PROJ), rwkv_state, w['rwkv'])
    x = _outproj(mla, s5_out, rwkv, x, w['out'], w['rwkv'], alpha)
    x = _mlp(x, w['mlp'], alpha)
    return x, ckv, krope, new_s5, new_rwkv, shift.reshape(B, 1, RWKV_PROJ)


def kernel(x_prompt, x_sample, cache_mla_ckv, cache_mla_krope, state_s5, state_rwkv, state_rwkv_shift, w_in, q_norm_g, w_qb, kv_norm_g, w_kvb, lam_re, lam_im, log_dt, b_re, b_im, c_re, c_im, s5_d, w_glu, b_glu, mu_shift, w0, w_w2, a0, w_a2, w_g2, k_k, k_a, r_k, gn_g, gn_b, w_out, ln1_g, ln1_b, w_up, w_down, ln2_g, ln2_b):
    depth = w_in.shape[0]
    alpha = (2 * depth) ** 0.25
    bp, sp = x_prompt.shape[:2]
    ts = x_sample.shape[1]
    past = cache_mla_ckv.shape[2]
    tabs_p = _rope_tables(jnp.arange(sp))
    tabs_s = _rope_tables(past + jnp.arange(ts))
    names = dict(w_in=w_in, q_norm_g=q_norm_g, w_qb=w_qb, kv_norm_g=kv_norm_g, w_kvb=w_kvb, lam_re=lam_re,
                 lam_im=lam_im, log_dt=log_dt, b_re=b_re, b_im=b_im, c_re=c_re, c_im=c_im, s5_d=s5_d, w_glu=w_glu,
                 b_glu=b_glu, mu_shift=mu_shift, w0=w0, w_w2=w_w2, a0=a0, w_a2=w_a2, w_g2=w_g2, k_k=k_k, k_a=k_a,
                 r_k=r_k, gn_g=gn_g, gn_b=gn_b, w_out=w_out, ln1_g=ln1_g, ln1_b=ln1_b, w_up=w_up, w_down=w_down,
                 ln2_g=ln2_g, ln2_b=ln2_b)
    s5_zero = jnp.zeros((bp, S5_GROUPS, S5_STATE, 2), F32)
    rwkv_zero = jnp.zeros((bp, RWKV_HEADS, RWKV_HEAD, RWKV_HEAD), F32)
    shift_zero = jnp.zeros((bp, 1, RWKV_PROJ), F32)

    xp, xs = x_prompt, x_sample
    outs_p = [[] for _ in range(5)]
    outs_s = [[] for _ in range(5)]
    for l in range(depth):
        w = _prep_layer({k: v[l] for k, v in names.items()})
        xp, *rest = _trunk_layer(xp, tabs_p, None, s5_zero, rwkv_zero, shift_zero, w, alpha)
        for acc, val in zip(outs_p, rest):
            acc.append(val)
        past_kv = _kvup(cache_mla_ckv[l], cache_mla_krope[l], w['kvup'])
        xs, *rest = _trunk_layer(xs, tabs_s, past_kv, state_s5[l], state_rwkv[l], state_rwkv_shift[l], w, alpha)
        for acc, val in zip(outs_s, rest):
            acc.append(val)
    return (xp, xs, *[jnp.stack(a) for a in outs_p], *[jnp.stack(a) for a in outs_s])
```

```python
import functools
import math

import jax
import jax.numpy as jnp
from jax import lax
from jax.experimental import pallas as pl
from jax.experimental.pallas import tpu as pltpu

F32 = jnp.float32
BF16 = jnp.bfloat16

CHUNK = 64
MLA_HEADS = 6
NOPE_DIM = 64
ROPE_DIM = 32
V_DIM = 64
Q_RANK = 256
KV_RANK = 128
ROPE_BASE = 10000.0
MLA_SCALE = (NOPE_DIM + ROPE_DIM) ** -0.5
S5_GROUP_CH = 16
S5_W = 256
S5_GROUPS = S5_W // S5_GROUP_CH
S5_STATE = 64
S5_N = S5_GROUPS * S5_STATE
RWKV_HEADS = 6
RWKV_HEAD = 64
RWKV_W = RWKV_HEADS * RWKV_HEAD
DECAY_LORA = 32
AAA_LORA = 32
GATE_LORA = 64
RWKV_PROJ = 3 * RWKV_W + DECAY_LORA + AAA_LORA + GATE_LORA
LN_EPS = 1e-5
RMS_EPS = 1e-6
GN_EPS = 64e-5
NEG_INF = -1e30

V7X_LANES = 128
V7X_VMEM_BYTES = 64 * 1024 * 1024
VMEM_LIMIT_BYTES = V7X_VMEM_BYTES - 8 * 1024 * 1024
HEAD_PAD = V7X_LANES
ATTN_BLOCK = 512
ROW_TILE = 128
ATTN_HEAD_GROUP = 6
V_ROWS = V_DIM + 16
LOG2_E = 1.4426950408889634
RWKV_PACK = 8


def _cparams(*sem):
    return pltpu.CompilerParams(dimension_semantics=sem, vmem_limit_bytes=VMEM_LIMIT_BYTES)


def _const_spec(shape):
    zeros = (0,) * len(shape)
    return pl.BlockSpec(shape, lambda *_: zeros)


def _tile(n, target):
    if n <= target:
        return n
    t = target
    while n % t:
        t //= 2
    return t


def _layer_norm(y, g, b):
    mu = jnp.mean(y, -1, keepdims=True)
    yc = y - mu
    var = jnp.mean(yc * yc, -1, keepdims=True)
    return yc * lax.rsqrt(var + LN_EPS) * g + b


def _inproj_kernel(x_ref, wm_ref, wu_ref, wp_ref, wq1_ref, wq2_ref, qg_ref, kg_ref, csk_ref, cq_ref, sq_ref,
                   qcat_ref, ckv_ref, krope_ref, u_ref, p_ref):
    B, tt, D = x_ref.shape
    x = x_ref[...].reshape(B * tt, D).astype(BF16)
    per_b = lambda tab_ref: jnp.concatenate([tab_ref[...]] * B, axis=0)
    m = jnp.dot(x, wm_ref[...], preferred_element_type=F32)
    q_lat = m[:, :Q_RANK]
    qn = q_lat * lax.rsqrt(jnp.mean(q_lat * q_lat, -1, keepdims=True) + RMS_EPS) * qg_ref[...]
    kv_lat = m[:, Q_RANK:Q_RANK + KV_RANK]
    ckv = kv_lat * lax.rsqrt(jnp.mean(kv_lat * kv_lat, -1, keepdims=True) + RMS_EPS) * kg_ref[...]
    ckv_ref[...] = ckv.reshape(B, tt, KV_RANK)
    kr = m[:, Q_RANK + KV_RANK:] * per_b(csk_ref)
    kr = kr + pltpu.roll(kr, V7X_LANES - ROPE_DIM, 1)
    krope_ref[...] = kr[:, :ROPE_DIM].reshape(B, tt, ROPE_DIM)
    qb = qn.astype(BF16)
    q1 = jnp.dot(qb, wq1_ref[...], preferred_element_type=F32)
    q2 = jnp.dot(qb, wq2_ref[...], preferred_element_type=F32)
    cq = per_b(cq_ref)
    sq = per_b(sq_ref)
    for h in range(MLA_HEADS):
        sl = slice(h * HEAD_PAD, (h + 1) * HEAD_PAD)
        qcat_ref[:, :, sl] = (q1[:, sl] * cq + q2[:, sl] * sq).astype(BF16).reshape(B, tt, HEAD_PAD)
    u = jnp.dot(x, wu_ref[...], preferred_element_type=F32)
    p = jnp.dot(x, wp_ref[...], preferred_element_type=F32)
    for b in range(B):
        u_ref[:, b, :] = u[b * tt:(b + 1) * tt]
        p_ref[:, b, :] = p[b * tt:(b + 1) * tt]


def _inproj(x, w, tabs):
    B, T, D = x.shape
    tt = _tile(T, ROW_TILE)
    nq = MLA_HEADS * HEAD_PAD
    row = lambda i: (0, i, 0)
    tab = lambda i: (i, 0)
    tm = lambda i: (i, 0, 0)
    return pl.pallas_call(
        _inproj_kernel,
        grid=(T // tt,),
        in_specs=[
            pl.BlockSpec((B, tt, D), row),
            _const_spec(w['wm'].shape), _const_spec(w['wu'].shape), _const_spec(w['wp'].shape),
            _const_spec(w['wq1'].shape), _const_spec(w['wq2'].shape),
            _const_spec((1, Q_RANK)), _const_spec((1, KV_RANK)),
            pl.BlockSpec((tt, V7X_LANES), tab), pl.BlockSpec((tt, HEAD_PAD), tab), pl.BlockSpec((tt, HEAD_PAD), tab),
        ],
        out_specs=[
            pl.BlockSpec((B, tt, nq), row),
            pl.BlockSpec((B, tt, KV_RANK), row),
            pl.BlockSpec((B, tt, ROPE_DIM), row),
            pl.BlockSpec((tt, B, S5_W), tm),
            pl.BlockSpec((tt, B, RWKV_PROJ), tm),
        ],
        out_shape=[
            jax.ShapeDtypeStruct((B, T, nq), BF16),
            jax.ShapeDtypeStruct((B, T, KV_RANK), F32),
            jax.ShapeDtypeStruct((B, T, ROPE_DIM), F32),
            jax.ShapeDtypeStruct((T, B, S5_W), F32),
            jax.ShapeDtypeStruct((T, B, RWKV_PROJ), F32),
        ],
        compiler_params=_cparams("parallel"),
        name="inproj",
    )(x, w['wm'], w['wu'], w['wp'], w['wq1'], w['wq2'], w['qg'], w['kg'], tabs['csk'], tabs['cq'], tabs['sq'])


def _kvup_kernel(ckv_ref, kr_ref, wk_ref, pk_ref, wvt_ref, kcat_ref, vt_ref):
    c = ckv_ref[0].astype(BF16)
    kr = kr_ref[0].astype(BF16)
    kcat = jnp.dot(c, wk_ref[...], preferred_element_type=F32) + jnp.dot(kr, pk_ref[...], preferred_element_type=F32)
    kcat_ref[0] = kcat.astype(BF16)
    vt = lax.dot_general(wvt_ref[...], c, (((1,), (1,)), ((), ())), preferred_element_type=F32).astype(BF16)
    ones = jnp.ones((V_ROWS - V_DIM, vt.shape[1]), BF16)
    for h in range(MLA_HEADS):
        vt_ref[0, h, 0, :V_DIM, :] = vt[h * V_DIM:(h + 1) * V_DIM, :]
        vt_ref[0, h, 0, V_DIM:, :] = ones


def _kvup(ckv, krope, w):
    B, T, _ = ckv.shape
    tt = _tile(T, ATTN_BLOCK)
    nk = MLA_HEADS * HEAD_PAD
    row = lambda b, i: (b, i, 0)
    return pl.pallas_call(
        _kvup_kernel,
        grid=(B, T // tt),
        in_specs=[pl.BlockSpec((1, tt, KV_RANK), row), pl.BlockSpec((1, tt, ROPE_DIM), row),
                  _const_spec(w['wk'].shape), _const_spec(w['pk'].shape), _const_spec(w['wvt'].shape)],
        out_specs=[pl.BlockSpec((1, tt, nk), row),
                   pl.BlockSpec((1, MLA_HEADS, 1, V_ROWS, tt), lambda b, i: (b, 0, i, 0, 0))],
        out_shape=[jax.ShapeDtypeStruct((B, T, nk), BF16),
                   jax.ShapeDtypeStruct((B, MLA_HEADS, T // tt, V_ROWS, tt), BF16)],
        compiler_params=_cparams("parallel", "parallel"),
        name="kvup",
    )(ckv, krope, w['wk'], w['pk'], w['wvt'])


def _attn_kernel(q_ref, k_ref, v_ref, *rest, bq, bk, kv_len, chunk_causal, extra):
    if extra:
        kx_ref, vx_ref, o_ref, sa_ref, sb_ref, bias_ref, m_ref, acc_ref = rest
    else:
        o_ref, sa_ref, sb_ref, bias_ref, m_ref, acc_ref = rest
    hg = ATTN_HEAD_GROUP
    qi = pl.program_id(2)
    last = qi if chunk_causal else jnp.int32(-(-kv_len // bk) - 1)
    mask_last = chunk_causal or kv_len % bk != 0
    nt = (((1,), (1,)), ((), ()))

    def scores(j, s_ref, h):
        r0 = pl.multiple_of(j * bk, bk)
        sl = slice(h * HEAD_PAD, (h + 1) * HEAD_PAD)
        s_ref[h] = lax.dot_general(k_ref[0, pl.ds(r0, bk), sl], q_ref[0, :, sl], nt,
                                   preferred_element_type=F32)

    def online_softmax(s, v, h):
        m = m_ref[h]
        m_new = jnp.maximum(m, jnp.max(s, axis=0, keepdims=True))
        p = jnp.exp2(s - m_new)
        alpha = jnp.exp2(m - m_new)
        acc_ref[h] = alpha * acc_ref[h] + jnp.dot(v, p.astype(BF16), preferred_element_type=F32)
        m_ref[h] = m_new

    def softmax_pv(j, s_ref, h, masked):
        s = s_ref[h]
        if masked:
            s = s + bias_ref[...]
        online_softmax(s, v_ref[0, h, j], h)

    @pl.when(qi == 0)
    def _():
        krel = lax.broadcasted_iota(jnp.int32, (bk, bq), 0)
        if chunk_causal:
            qrel = lax.broadcasted_iota(jnp.int32, (bk, bq), 1)
            ok = (krel // CHUNK) <= (qrel // CHUNK)
        else:
            ok = krel < kv_len - (-(-kv_len // bk) - 1) * bk
        bias_ref[...] = jnp.where(ok, 0.0, NEG_INF)

    def advance(j_next, s_next, j, s_cur, masked=False):
        for h in range(hg):
            if j_next is not None:
                scores(j_next, s_next, h)
            softmax_pv(j, s_cur, h, masked)

    m_ref[...] = jnp.full(m_ref.shape, NEG_INF, F32)
    acc_ref[...] = jnp.zeros(acc_ref.shape, F32)
    for h in range(hg):
        scores(0, sa_ref, h)

    def pair(p, _):
        j = 2 * p
        advance(j + 1, sb_ref, j, sa_ref)
        advance(j + 2, sa_ref, j + 1, sb_ref)
        return 0

    lax.fori_loop(0, last // 2, pair, 0)

    @pl.when(last % 2 == 0)
    def _():
        advance(None, None, last, sa_ref, mask_last)

    @pl.when(last % 2 == 1)
    def _():
        advance(last, sb_ref, last - 1, sa_ref)
        advance(None, None, last, sb_ref, mask_last)

    if extra:
        for h in range(hg):
            sl = slice(h * HEAD_PAD, (h + 1) * HEAD_PAD)
            s = lax.dot_general(kx_ref[0, :, sl], q_ref[0, :, sl], nt, preferred_element_type=F32)
            online_softmax(s, vx_ref[0, h, 0], h)

    for h in range(hg):
        o_ref[0, h] = (acc_ref[h, :V_DIM, :] / acc_ref[h, V_DIM:V_DIM + 1, :]).astype(BF16)


def _attention(qcat, kcat, vt, *, kv_len, chunk_causal, extra=None):
    B, Tq, _ = qcat.shape
    Tk = kcat.shape[1]
    nblk, bk = vt.shape[2], vt.shape[4]
    assert nblk * bk == Tk
    bq = bk if chunk_causal else Tq
    hg = ATTN_HEAD_GROUP
    in_specs = [
        pl.BlockSpec((1, bq, hg * HEAD_PAD), lambda b, g, i: (b, i, g)),
        pl.BlockSpec((1, Tk, hg * HEAD_PAD), lambda b, g, i: (b, 0, g), pipeline_mode=pl.Buffered(1)),
        pl.BlockSpec((1, hg, nblk, V_ROWS, bk), lambda b, g, i: (b, g, 0, 0, 0), pipeline_mode=pl.Buffered(1)),
    ]
    operands = [qcat, kcat, vt]
    if extra is not None:
        tx = extra[0].shape[1]
        in_specs += [pl.BlockSpec((1, tx, hg * HEAD_PAD), lambda b, g, i: (b, 0, g)),
                     pl.BlockSpec((1, hg, 1, V_ROWS, tx), lambda b, g, i: (b, g, 0, 0, 0))]
        operands += list(extra)
    out_t = pl.pallas_call(
        functools.partial(_attn_kernel, bq=bq, bk=bk, kv_len=kv_len, chunk_causal=chunk_causal,
                          extra=extra is not None),
        grid=(B, MLA_HEADS // hg, Tq // bq),
        in_specs=in_specs,
        out_specs=pl.BlockSpec((1, hg, V_DIM, bq), lambda b, g, i: (b, g, 0, i)),
        out_shape=jax.ShapeDtypeStruct((B, MLA_HEADS, V_DIM, Tq), BF16),
        scratch_shapes=[pltpu.VMEM((hg, bk, bq), F32), pltpu.VMEM((hg, bk, bq), F32), pltpu.VMEM((bk, bq), F32),
                        pltpu.VMEM((hg, 1, bq), F32), pltpu.VMEM((hg, V_ROWS, bq), F32)],
        compiler_params=_cparams("parallel", "parallel", "arbitrary"),
        name="mla_attention",
    )(*operands)
    return out_t


def _gelu_tanh(x):
    return 0.5 * x * (1.0 + jnp.tanh(math.sqrt(2.0 / math.pi) * (x + 0.044715 * (x * x * x))))


def _s5_kernel(u_ref, x0_ref, a_ref, wb_ref, wc_ref, d_ref, wg_ref, bg_ref, o_ref, xl_ref, xs_ref, st_ref, *, L, B):
    @pl.when(pl.program_id(0) == 0)
    def _():
        st_ref[...] = x0_ref[...]

    u = u_ref[...].reshape(L * B, S5_W)
    xs_ref[...] = jnp.dot(u.astype(BF16), wb_ref[...], preferred_element_type=F32)
    a_re = a_ref[0:1, :]
    a_im = a_ref[1:2, :]

    def body(t, carry):
        x_re, x_im = carry
        r0 = pl.multiple_of(t * B, B)
        n_re = a_re * x_re - a_im * x_im + xs_ref[pl.ds(r0, B), :S5_N]
        n_im = a_re * x_im + a_im * x_re + xs_ref[pl.ds(r0, B), S5_N:]
        xs_ref[pl.ds(r0, B), :S5_N] = n_re
        xs_ref[pl.ds(r0, B), S5_N:] = n_im
        return n_re, n_im

    x_re, x_im = lax.fori_loop(0, L, body, (st_ref[:, :S5_N], st_ref[:, S5_N:]), unroll=8)
    st_ref[:, :S5_N] = x_re
    st_ref[:, S5_N:] = x_im
    xl_ref[...] = st_ref[...]

    y = jnp.dot(xs_ref[...].astype(BF16), wc_ref[...], preferred_element_type=F32) + d_ref[...] * u
    z = _gelu_tanh(y)
    gate = jax.nn.sigmoid(jnp.dot(z.astype(BF16), wg_ref[...], preferred_element_type=F32) + bg_ref[...])
    o_ref[...] = (z * gate).reshape(L, B, S5_W)


def _s5(u_tm, x0, w):
    T, B, _ = u_tm.shape
    L = _tile(T, 1024 // B)
    return pl.pallas_call(
        functools.partial(_s5_kernel, L=L, B=B),
        grid=(T // L,),
        in_specs=[
            pl.BlockSpec((L, B, S5_W), lambda i: (i, 0, 0)),
            _const_spec((B, 2 * S5_N)), _const_spec((2, S5_N)),
            _const_spec((S5_W, 2 * S5_N)), _const_spec((2 * S5_N, S5_W)),
            _const_spec((1, S5_W)), _const_spec((S5_W, S5_W)), _const_spec((1, S5_W)),
        ],
        out_specs=[pl.BlockSpec((L, B, S5_W), lambda i: (i, 0, 0)), _const_spec((B, 2 * S5_N))],
        out_shape=[jax.ShapeDtypeStruct((T, B, S5_W), F32), jax.ShapeDtypeStruct((B, 2 * S5_N), F32)],
        scratch_shapes=[pltpu.VMEM((L * B, 2 * S5_N), F32), pltpu.VMEM((B, 2 * S5_N), F32)],
        compiler_params=_cparams("arbitrary"),
        name="s5_scan",
    )(u_tm, x0, w['a'], w['wb'], w['wc'], w['d'], w['wglu'], w['bglu'])


def _softplus(x):
    return jnp.maximum(x, 0.0) + jnp.log(1.0 + jnp.exp(-jnp.abs(x)))


def _head_sum(x, ones_ref):
    hi = x.astype(BF16)
    lo = (x - hi.astype(F32)).astype(BF16)
    ones = ones_ref[...]
    return jnp.dot(hi, ones, preferred_element_type=F32) + jnp.dot(lo, ones, preferred_element_type=F32)


def _pack_rows(x, o_ref, zs_ref, B):
    rows = RWKV_PACK * B
    lane = lax.broadcasted_iota(jnp.int32, (RWKV_HEAD, rows * RWKV_HEADS), 1)
    for m in range(x.shape[0] // rows):
        xm = x[m * rows:(m + 1) * rows]
        for h in range(RWKV_HEADS):
            c = h * RWKV_HEAD // V7X_LANES
            zs_ref[pl.ds(h, rows, stride=RWKV_HEADS), :] = xm[:, c * V7X_LANES:(c + 1) * V7X_LANES]
        t = zs_ref[...].T
        o_ref[m] = jnp.where(lane % 2 == 0, t[:RWKV_HEAD, :], t[RWKV_HEAD:, :])


def _rwkv_prep_kernel(p_ref, sh0_ref, mu_ref, ww_ref, wa_ref, wg_ref, w0_ref, a0_ref, ka_ref, rk_ref,
                      ones_ref, r_o, w_o, k_o, v_o, a_o, g_o, bon_o, sh_o, last_ref, zs_ref, *, L, B):
    @pl.when(pl.program_id(0) == 0)
    def _():
        last_ref[...] = sh0_ref[...]

    p = p_ref[...].reshape(L * B, RWKV_PROJ)
    if L > 1:
        prev = jnp.concatenate([last_ref[...], p[:(L - 1) * B]], axis=0)
    else:
        prev = last_ref[...]
    last_ref[...] = p[(L - 1) * B:]
    sh_o[...] = p[(L - 1) * B:]
    ps = p + (prev - p) * mu_ref[...]
    r = ps[:, :RWKV_W]
    k = ps[:, RWKV_W:2 * RWKV_W]
    v = ps[:, 2 * RWKV_W:3 * RWKV_W]
    tail = ps[:, 3 * RWKV_W:]
    lw = jnp.dot(jnp.tanh(tail).astype(BF16), ww_ref[...], preferred_element_type=F32)
    la = jnp.dot(tail.astype(BF16), wa_ref[...], preferred_element_type=F32)
    g = jnp.dot(jax.nn.sigmoid(tail).astype(BF16), wg_ref[...], preferred_element_type=F32)
    w_log = -_softplus(-(w0_ref[...] + lw)) - 0.5
    decay = jnp.exp(-jnp.exp(w_log))
    a = jax.nn.sigmoid(a0_ref[...] + la)
    k2 = k * (1.0 + (a - 1.0) * ka_ref[...])
    bonus = _head_sum(r * k2 * rk_ref[...], ones_ref) * v
    for val, o_ref in ((r, r_o), (decay, w_o), (k, k_o), (v, v_o), (a, a_o)):
        _pack_rows(val, o_ref, zs_ref, B)
    shp = (L, B, RWKV_W)
    g_o[...] = g.reshape(shp)
    bon_o[...] = bonus.reshape(shp)


def _rwkv_prep(p_tm, shift0, w):
    T, B, _ = p_tm.shape
    L = _tile(T, 512 // B)
    assert L % RWKV_PACK == 0
    blk = pl.BlockSpec((L, B, RWKV_W), lambda i: (i, 0, 0))
    W = RWKV_PACK * B * RWKV_HEADS
    pblk = pl.BlockSpec((L // RWKV_PACK, RWKV_HEAD, W), lambda i: (i, 0, 0))
    packed = jax.ShapeDtypeStruct((T // RWKV_PACK, RWKV_HEAD, W), F32)
    vec = _const_spec((1, RWKV_W))
    outs = pl.pallas_call(
        functools.partial(_rwkv_prep_kernel, L=L, B=B),
        grid=(T // L,),
        in_specs=[pl.BlockSpec((L, B, RWKV_PROJ), lambda i: (i, 0, 0)), _const_spec((B, RWKV_PROJ)),
                  _const_spec((1, RWKV_PROJ)),
                  _const_spec((V7X_LANES, RWKV_W)), _const_spec((V7X_LANES, RWKV_W)), _const_spec((V7X_LANES, RWKV_W)),
                  vec, vec, vec, vec, _const_spec((RWKV_W, RWKV_W))],
        out_specs=[pblk] * 5 + [blk] * 2 + [_const_spec((B, RWKV_PROJ))],
        out_shape=[packed] * 5 + [jax.ShapeDtypeStruct((T, B, RWKV_W), F32)] * 2
        + [jax.ShapeDtypeStruct((B, RWKV_PROJ), F32)],
        scratch_shapes=[pltpu.VMEM((B, RWKV_PROJ), F32), pltpu.VMEM((RWKV_PACK * B * RWKV_HEADS, V7X_LANES), F32)],
        compiler_params=_cparams("arbitrary"),
        name="rwkv_prep",
    )(p_tm, shift0, w['mu'], w['ww'], w['wa'], w['wg'], w['w0'], w['a0'], w['ka'], w['rk'], w['ones'])
    return outs


def _lane_window(tile, off, width, dst, lane):
    v0, lo = divmod(off, V7X_LANES)
    shift = (dst - lo) % V7X_LANES
    a = tile(v0)
    r = pltpu.roll(a, shift, 1) if shift else a
    if lo + width > V7X_LANES:
        b = tile(v0 + 1)
        r2 = pltpu.roll(b, shift, 1) if shift else b
        r = jnp.where(lane < dst + (V7X_LANES - lo), r, r2)
    return r


def _chain_lanes(pieces, bh, lane):
    rep = len(pieces)
    out = jnp.where(lane < rep * bh, pieces[rep - 1], 0.0)
    for r in range(rep - 2, -1, -1):
        out = jnp.where(lane < (r + 1) * bh, pieces[r], out)
    return out


def _rwkv_scan_kernel(w_ref, a_ref, k_ref, kx_ref, r_ref, v_ref, kkt_ref, kat_ref, s0_ref, y_ref, so_ref,
                      s_ref, sa_ref, e_ref, ev_ref, ys_ref, kall_ref, *, Tm, IL, bh, rep):
    lane = lax.broadcasted_iota(jnp.int32, (RWKV_HEAD, V7X_LANES), 1)
    lane_il = lax.broadcasted_iota(jnp.int32, (IL, V7X_LANES), 1)
    ncol = RWKV_PACK * bh // V7X_LANES

    def tiles(slab, rows=slice(None)):
        return lambda c: slab[rows, c * V7X_LANES:(c + 1) * V7X_LANES]

    def expand(slab, t8):
        return _chain_lanes([_lane_window(tiles(slab), t8 * bh, bh, r * bh, lane) for r in range(rep)], bh, lane)

    def unit_kk(k_raw):
        kk = k_raw * kkt_ref[...]
        nrm = jnp.sqrt(jnp.sum(kk * kk, axis=0, keepdims=True))
        return kk / jnp.maximum(nrm, 1e-12)

    kall_ref[0:Tm] = k_ref[...]
    kall_ref[Tm] = kx_ref[0]

    @pl.when(pl.program_id(0) == 0)
    def _():
        s_ref[...] = s0_ref[...]
        e_ref[0, 4] = expand(-unit_kk(k_ref[0]), 0)
        acc = jnp.zeros((IL, V7X_LANES), F32)
        for j in range(RWKV_HEAD):
            acc = acc + s0_ref[j] * e_ref[0, 4, j:j + 1, :]
        sa_ref[...] = acc

    def body(m, sa):
        k_raw = kall_ref[m]
        a_s = a_ref[m]
        kkn = unit_kk(k_raw)
        slabs = [w_ref[m], kkn * a_s, k_raw * (1.0 + (a_s - 1.0) * kat_ref[...]), r_ref[m]]
        nn0 = -kkn
        nn1 = -unit_kk(kall_ref[m + 1])
        vs = v_ref[m]
        for t8 in range(RWKV_PACK):
            for o, slab in enumerate(slabs):
                e_ref[t8, o] = expand(slab, t8)
            e_ref[t8, 4] = expand(nn0, t8 + 1) if t8 + 1 < RWKV_PACK else expand(nn1, 0)
            ev_ref[t8] = _chain_lanes([_lane_window(tiles(vs, slice(r * IL, (r + 1) * IL)), t8 * bh, bh, r * bh,
                                                    lane_il) for r in range(rep)], bh, lane_il)
        for t8 in range(RWKV_PACK):
            vt = ev_ref[t8]
            yacc = jnp.zeros((IL, V7X_LANES), F32)
            san = jnp.zeros((IL, V7X_LANES), F32)
            for j in range(RWKV_HEAD):
                row = lambda o: e_ref[t8, o, j:j + 1, :]
                sn = s_ref[j] * row(0) + sa * row(1) + vt * row(2)
                s_ref[j] = sn
                yacc = yacc + sn * row(3)
                san = san + sn * row(4)
            sa = san
            ys_ref[t8] = yacc
        for c in range(ncol):
            for r in range(rep):
                col = jnp.zeros((IL, V7X_LANES), F32)
                for t8 in range(RWKV_PACK):
                    off = t8 * bh
                    lo, hi = max(off, c * V7X_LANES), min(off + bh, (c + 1) * V7X_LANES)
                    if lo < hi:
                        shift = (off - r * bh) % V7X_LANES
                        rolled = pltpu.roll(ys_ref[t8], shift, 1) if shift else ys_ref[t8]
                        inside = jnp.logical_and(lane_il >= lo - c * V7X_LANES, lane_il < hi - c * V7X_LANES)
                        col = jnp.where(inside, rolled, col)
                y_ref[m, r * IL:(r + 1) * IL, c * V7X_LANES:(c + 1) * V7X_LANES] = col
        return sa

    sa_ref[...] = lax.fori_loop(0, Tm, body, sa_ref[...])
    so_ref[...] = s_ref[...]


def _rwkv_scan(w_p, a_p, k_p, r_p, v_p, kk_t, ka_t, s0_l, bh, rep):
    TP, _, W = v_p.shape
    IL = RWKV_HEAD // rep
    Tm = _tile(TP, 8)
    blk = pl.BlockSpec((Tm, RWKV_HEAD, W), lambda i: (i, 0, 0))
    nxt = pl.BlockSpec((1, RWKV_HEAD, W), lambda i: (jnp.minimum((i + 1) * Tm, TP - 1), 0, 0))
    sblk = _const_spec((RWKV_HEAD, IL, V7X_LANES))
    par = _const_spec((RWKV_HEAD, W))
    return pl.pallas_call(
        functools.partial(_rwkv_scan_kernel, Tm=Tm, IL=IL, bh=bh, rep=rep),
        grid=(TP // Tm,),
        in_specs=[blk, blk, blk, nxt, blk, blk, par, par, sblk],
        out_specs=[blk, sblk],
        out_shape=[jax.ShapeDtypeStruct((TP, RWKV_HEAD, W), F32),
                   jax.ShapeDtypeStruct((RWKV_HEAD, IL, V7X_LANES), F32)],
        scratch_shapes=[pltpu.VMEM((RWKV_HEAD, IL, V7X_LANES), F32), pltpu.VMEM((IL, V7X_LANES), F32),
                        pltpu.VMEM((RWKV_PACK, 5, RWKV_HEAD, V7X_LANES), F32),
                        pltpu.VMEM((RWKV_PACK, IL, V7X_LANES), F32), pltpu.VMEM((RWKV_PACK, IL, V7X_LANES), F32),
                        pltpu.VMEM((Tm + 1, RWKV_HEAD, W), F32)],
        compiler_params=_cparams("arbitrary"),
        name="rwkv_scan",
    )(w_p, a_p, k_p, k_p, r_p, v_p, kk_t, ka_t, s0_l)


def _rwkv_layout(B):
    bh = B * RWKV_HEADS
    rep = 1
    while 2 * rep * bh <= V7X_LANES and RWKV_HEAD % (2 * rep) == 0:
        rep *= 2
    return bh, rep, RWKV_HEAD // rep


def _rwkv_mixer(p_tm, shift0, s0, w):
    T, B, _ = p_tm.shape
    bh, rep, IL = _rwkv_layout(B)
    lanes = rep * bh
    r_p, dec_p, k_p, v_p, a_p, g, bonus, shift = _rwkv_prep(p_tm, shift0, w)
    TP = T // RWKV_PACK

    def par_tile(vec):
        t = vec.reshape(RWKV_HEADS, RWKV_HEAD).T[:, None, None, :]
        return jnp.broadcast_to(t, (RWKV_HEAD, RWKV_PACK, B, RWKV_HEADS)).reshape(RWKV_HEAD, RWKV_PACK * bh)

    s0_l = s0.reshape(bh, rep, IL, RWKV_HEAD).transpose(3, 2, 1, 0).reshape(RWKV_HEAD, IL, lanes)
    s0_l = jnp.pad(s0_l, ((0, 0), (0, 0), (0, V7X_LANES - lanes)))
    y_p, s_l = _rwkv_scan(dec_p, a_p, k_p, r_p, v_p, par_tile(w['kk']), par_tile(w['ka']), s0_l, bh, rep)
    y_tm = y_p.reshape(TP, RWKV_HEAD, RWKV_PACK, bh).transpose(0, 2, 3, 1).reshape(T, B, RWKV_W)
    s_last = s_l[:, :, :lanes].reshape(RWKV_HEAD, IL, rep, bh).transpose(3, 2, 1, 0)
    s_last = s_last.reshape(B, RWKV_HEADS, RWKV_HEAD, RWKV_HEAD)
    return (y_tm, bonus, g), s_last, shift


def _outproj_kernel(m_ref, s_ref, y_ref, bon_ref, gate_ref, x_ref, wm_ref, ws_ref, wr_ref, gng_ref, gnb_ref,
                    ones_ref, g_ref, b_ref, o_ref, *, alpha):
    B, tt, D = x_ref.shape
    mla = jnp.concatenate([m_ref[b].reshape(MLA_HEADS * V_DIM, tt).T for b in range(B)], axis=0)
    rows = lambda ref: jnp.concatenate([ref[:, b, :] for b in range(B)], axis=0)
    acc = jnp.dot(mla, wm_ref[...], preferred_element_type=F32)
    acc = acc + jnp.dot(rows(s_ref).astype(BF16), ws_ref[...], preferred_element_type=F32)
    y = rows(y_ref)
    inv_n = 1.0 / RWKV_HEAD
    mu = _head_sum(y, ones_ref) * inv_n
    yc = y - mu
    var = _head_sum(yc * yc, ones_ref) * inv_n
    yn = yc * lax.rsqrt(var + GN_EPS) * gng_ref[...] + gnb_ref[...]
    rw = (yn + rows(bon_ref)) * rows(gate_ref)
    acc = acc + jnp.dot(rw.astype(BF16), wr_ref[...], preferred_element_type=F32)
    out = _layer_norm(alpha * x_ref[...].reshape(B * tt, D) + acc, g_ref[...], b_ref[...])
    o_ref[...] = out.reshape(B, tt, D)


def _outproj(mla_t, s5_tm, rwkv, x, w, wr, alpha):
    B, T, D = x.shape
    tt = _tile(T, ROW_TILE)
    row = lambda i: (0, i, 0)
    tm = lambda i: (i, 0, 0)
    rblk = pl.BlockSpec((tt, B, RWKV_W), tm)
    vec = _const_spec((1, RWKV_W))
    return pl.pallas_call(
        functools.partial(_outproj_kernel, alpha=alpha),
        grid=(T // tt,),
        in_specs=[pl.BlockSpec((B, MLA_HEADS, V_DIM, tt), lambda i: (0, 0, 0, i)), pl.BlockSpec((tt, B, S5_W), tm),
                  rblk, rblk, rblk, pl.BlockSpec((B, tt, D), row),
                  _const_spec(w['wo_m'].shape), _const_spec(w['wo_s'].shape), _const_spec(w['wo_r'].shape),
                  vec, vec, _const_spec((RWKV_W, RWKV_W)), _const_spec((1, D)), _const_spec((1, D))],
        out_specs=pl.BlockSpec((B, tt, D), row),
        out_shape=jax.ShapeDtypeStruct((B, T, D), F32),
        compiler_params=_cparams("parallel"),
        name="outproj_ln",
    )(mla_t, s5_tm, *rwkv, x, w['wo_m'], w['wo_s'], w['wo_r'], wr['gng'], wr['gnb'], wr['ones'], w['ln1g'],
      w['ln1b'])


def _mlp_kernel(x_ref, wu_ref, wd_ref, g_ref, b_ref, o_ref, *, alpha, fc):
    x = x_ref[0]
    xb = x.astype(BF16)
    acc = alpha * x
    for c in range(wu_ref.shape[1] // fc):
        h = jnp.maximum(jnp.dot(xb, wu_ref[:, c * fc:(c + 1) * fc], preferred_element_type=F32), 0.0)
        acc = acc + jnp.dot((h * h).astype(BF16), wd_ref[c * fc:(c + 1) * fc, :], preferred_element_type=F32)
    o_ref[0] = _layer_norm(acc, g_ref[...], b_ref[...])


def _mlp(x, w, alpha):
    B, T, D = x.shape
    tt = _tile(T, 512)
    row = lambda b, i: (b, i, 0)
    return pl.pallas_call(
        functools.partial(_mlp_kernel, alpha=alpha, fc=1024),
        grid=(B, T // tt),
        in_specs=[pl.BlockSpec((1, tt, D), row), _const_spec(w['wup'].shape), _const_spec(w['wdown'].shape),
                  _const_spec((1, D)), _const_spec((1, D))],
        out_specs=pl.BlockSpec((1, tt, D), row),
        out_shape=jax.ShapeDtypeStruct((B, T, D), F32),
        compiler_params=_cparams("parallel", "parallel"),
        name="mlp_ln",
    )(x, w['wup'], w['wdown'], w['ln2g'], w['ln2b'])


def _rot_cols(wr):
    half = ROPE_DIM // 2
    return jnp.concatenate([-wr[..., half:], wr[..., :half]], axis=-1)


def _prep_layer(p):
    D = p['w_in'].shape[0]
    w_in = p['w_in']
    o_kv = Q_RANK
    o_kr = Q_RANK + KV_RANK
    o_u = o_kr + ROPE_DIM
    o_p = o_u + S5_W
    w_kr = w_in[:, o_kr:o_u]
    wm = jnp.concatenate([w_in[:, :o_kr], w_kr, _rot_cols(w_kr),
                          jnp.zeros((D, V7X_LANES - 2 * ROPE_DIM), F32)], axis=1)
    wqb = p['w_qb'].reshape(Q_RANK, MLA_HEADS, NOPE_DIM + ROPE_DIM) * (MLA_SCALE * LOG2_E)
    zq = jnp.zeros((Q_RANK, MLA_HEADS, HEAD_PAD - NOPE_DIM - ROPE_DIM), F32)
    wq1 = jnp.concatenate([wqb, zq], axis=-1).reshape(Q_RANK, MLA_HEADS * HEAD_PAD)
    wq2 = jnp.concatenate([jnp.zeros((Q_RANK, MLA_HEADS, NOPE_DIM), F32), _rot_cols(wqb[..., NOPE_DIM:]), zq],
                          axis=-1).reshape(Q_RANK, MLA_HEADS * HEAD_PAD)
    wkvb = p['w_kvb'].reshape(KV_RANK, MLA_HEADS, NOPE_DIM + V_DIM)
    wk = jnp.concatenate([wkvb[..., :NOPE_DIM], jnp.zeros((KV_RANK, MLA_HEADS, HEAD_PAD - NOPE_DIM), F32)],
                         axis=-1).reshape(KV_RANK, MLA_HEADS * HEAD_PAD)
    pk = jnp.concatenate([jnp.zeros((ROPE_DIM, NOPE_DIM), F32), jnp.eye(ROPE_DIM, dtype=F32),
                          jnp.zeros((ROPE_DIM, HEAD_PAD - NOPE_DIM - ROPE_DIM), F32)], axis=1)
    pk = jnp.tile(pk, (1, MLA_HEADS))
    wvt = wkvb[..., NOPE_DIM:].reshape(KV_RANK, MLA_HEADS * V_DIM).T

    lr, li = p['lam_re'], p['lam_im']
    dt = jnp.exp(p['log_dt'])[:, None]
    mag = jnp.exp(lr * dt)
    ar, ai = mag * jnp.cos(li * dt), mag * jnp.sin(li * dt)
    den = lr * lr + li * li
    cr = ((ar - 1.0) * lr + ai * li) / den
    ci = (ai * lr - (ar - 1.0) * li) / den
    bbr = cr[..., None] * p['b_re'] - ci[..., None] * p['b_im']
    bbi = cr[..., None] * p['b_im'] + ci[..., None] * p['b_re']
    eye_g = jnp.eye(S5_GROUPS, dtype=F32)
    bd_in = lambda m: (m.transpose(0, 2, 1)[:, :, None, :] * eye_g[:, None, :, None]).reshape(S5_W, S5_N)
    bd_out = lambda m: (m.transpose(0, 2, 1)[:, :, None, :] * eye_g[:, None, :, None]).reshape(S5_N, S5_W)
    wb = jnp.concatenate([bd_in(bbr), bd_in(bbi)], axis=1)
    wc = jnp.concatenate([bd_out(p['c_re']), -bd_out(p['c_im'])], axis=0)
    a = jnp.stack([ar.reshape(S5_N), ai.reshape(S5_N)])

    def lora_pad(w, off):
        z = jnp.zeros((V7X_LANES, RWKV_W), F32)
        return z.at[off:off + w.shape[0]].set(w)

    seg = jnp.arange(RWKV_W) // RWKV_HEAD
    ones = (seg[:, None] == seg[None, :]).astype(F32)
    row = lambda v: v.reshape(1, -1).astype(F32)
    wo = p['w_out']
    return dict(
        inproj=dict(wm=wm.astype(BF16), wu=w_in[:, o_u:o_p].astype(BF16), wp=w_in[:, o_p:].astype(BF16),
                    wq1=wq1.astype(BF16), wq2=wq2.astype(BF16), qg=row(p['q_norm_g']), kg=row(p['kv_norm_g'])),
        kvup=dict(wk=wk.astype(BF16), pk=pk.astype(BF16), wvt=wvt.astype(BF16)),
        s5=dict(a=a, wb=wb.astype(BF16), wc=wc.astype(BF16), d=row(p['s5_d']), wglu=p['w_glu'].astype(BF16),
                bglu=row(p['b_glu'])),
        rwkv=dict(mu=row(p['mu_shift']), ww=lora_pad(p['w_w2'], 0).astype(BF16),
                  wa=lora_pad(p['w_a2'], DECAY_LORA).astype(BF16),
                  wg=lora_pad(p['w_g2'], DECAY_LORA + AAA_LORA).astype(BF16),
                  w0=row(p['w0']), a0=row(p['a0']), kk=row(p['k_k']), ka=row(p['k_a']), rk=row(p['r_k']),
                  gng=row(p['gn_g']), gnb=row(p['gn_b']), ones=ones.astype(BF16)),
        out=dict(wo_m=wo[:MLA_HEADS * V_DIM].astype(BF16),
                 wo_s=wo[MLA_HEADS * V_DIM:MLA_HEADS * V_DIM + S5_W].astype(BF16),
                 wo_r=wo[MLA_HEADS * V_DIM + S5_W:].astype(BF16), ln1g=row(p['ln1_g']), ln1b=row(p['ln1_b'])),
        mlp=dict(wup=p['w_up'].astype(BF16), wdown=p['w_down'].astype(BF16), ln2g=row(p['ln2_g']),
                 ln2b=row(p['ln2_b'])),
    )


def _rope_tables(pos):
    inv_freq = ROPE_BASE ** (-jnp.arange(0, ROPE_DIM, 2, dtype=F32) / ROPE_DIM)
    ang = pos.astype(F32)[:, None] * inv_freq[None, :]
    ang = jnp.concatenate([ang, ang], -1)
    cos, sin = jnp.cos(ang), jnp.sin(ang)
    T = pos.shape[0]
    z = lambda n: jnp.zeros((T, n), F32)
    return dict(
        csk=jnp.concatenate([cos, sin, z(V7X_LANES - 2 * ROPE_DIM)], axis=1),
        cq=jnp.concatenate([jnp.ones((T, NOPE_DIM), F32), cos, z(HEAD_PAD - NOPE_DIM - ROPE_DIM)], axis=1),
        sq=jnp.concatenate([z(NOPE_DIM), sin, z(HEAD_PAD - NOPE_DIM - ROPE_DIM)], axis=1),
    )


def _trunk_layer(x, tabs, past, s5_state, rwkv_state, shift0, w, alpha):
    B, T, D = x.shape
    qcat, ckv, krope, u_tm, p_tm = _inproj(x, w['inproj'], tabs)
    kcat, vt = _kvup(ckv, krope, w['kvup'])
    if past is None:
        mla = _attention(qcat, kcat, vt, kv_len=T, chunk_causal=True)
    else:
        assert vt.shape[2] == 1
        mla = _attention(qcat, past[0], past[1], kv_len=past[0].shape[1], chunk_causal=False, extra=(kcat, vt))

    x0 = jnp.concatenate([s5_state[..., 0].reshape(B, S5_N), s5_state[..., 1].reshape(B, S5_N)], axis=1)
    s5_out, x_last = _s5(u_tm, x0, w['s5'])
    new_s5 = jnp.stack([x_last[:, :S5_N].reshape(B, S5_GROUPS, S5_STATE),
                        x_last[:, S5_N:].reshape(B, S5_GROUPS, S5_STATE)], axis=-1)

    rwkv, new_rwkv, shift = _rwkv_mixer(p_tm, shift0.reshape(B, RWKV_PROJ), rwkv_state, w['rwkv'])
    x = _outproj(mla, s5_out, rwkv, x, w['out'], w['rwkv'], alpha)
    x = _mlp(x, w['mlp'], alpha)
    return x, ckv, krope, new_s5, new_rwkv, shift.reshape(B, 1, RWKV_PROJ)


def kernel(x_prompt, x_sample, cache_mla_ckv, cache_mla_krope, state_s5, state_rwkv, state_rwkv_shift, w_in, q_norm_g, w_qb, kv_norm_g, w_kvb, lam_re, lam_im, log_dt, b_re, b_im, c_re, c_im, s5_d, w_glu, b_glu, mu_shift, w0, w_w2, a0, w_a2, w_g2, k_k, k_a, r_k, gn_g, gn_b, w_out, ln1_g, ln1_b, w_up, w_down, ln2_g, ln2_b):
    depth = w_in.shape[0]
    alpha = (2 * depth) ** 0.25
    bp, sp = x_prompt.shape[:2]
    ts = x_sample.shape[1]
    past = cache_mla_ckv.shape[2]
    tabs_p = _rope_tables(jnp.arange(sp))
    tabs_s = _rope_tables(past + jnp.arange(ts))
    names = dict(w_in=w_in, q_norm_g=q_norm_g, w_qb=w_qb, kv_norm_g=kv_norm_g, w_kvb=w_kvb, lam_re=lam_re,
                 lam_im=lam_im, log_dt=log_dt, b_re=b_re, b_im=b_im, c_re=c_re, c_im=c_im, s5_d=s5_d, w_glu=w_glu,
                 b_glu=b_glu, mu_shift=mu_shift, w0=w0, w_w2=w_w2, a0=a0, w_a2=w_a2, w_g2=w_g2, k_k=k_k, k_a=k_a,
                 r_k=r_k, gn_g=gn_g, gn_b=gn_b, w_out=w_out, ln1_g=ln1_g, ln1_b=ln1_b, w_up=w_up, w_down=w_down,
                 ln2_g=ln2_g, ln2_b=ln2_b)
    s5_zero = jnp.zeros((bp, S5_GROUPS, S5_STATE, 2), F32)
    rwkv_zero = jnp.zeros((bp, RWKV_HEADS, RWKV_HEAD, RWKV_HEAD), F32)
    shift_zero = jnp.zeros((bp, 1, RWKV_PROJ), F32)

    xp, xs = x_prompt, x_sample
    outs_p = [[] for _ in range(5)]
    outs_s = [[] for _ in range(5)]
    for l in range(depth):
        w = _prep_layer({k: v[l] for k, v in names.items()})
        xp, *rest = _trunk_layer(xp, tabs_p, None, s5_zero, rwkv_zero, shift_zero, w, alpha)
        for acc, val in zip(outs_p, rest):
            acc.append(val)
        past_kv = _kvup(cache_mla_ckv[l], cache_mla_krope[l], w['kvup'])
        xs, *rest = _trunk_layer(xs, tabs_s, past_kv, state_s5[l], state_rwkv[l], state_rwkv_shift[l], w, alpha)
        for acc, val in zip(outs_s, rest):
            acc.append(val)
    return (xp, xs, *[jnp.stack(a) for a in outs_p], *[jnp.stack(a) for a in outs_s])
```

```python
import functools
import math

import jax
import jax.numpy as jnp
from jax import lax
from jax.experimental import pallas as pl
from jax.experimental.pallas import tpu as pltpu

F32 = jnp.float32
BF16 = jnp.bfloat16

CHUNK = 64
MLA_HEADS = 6
NOPE_DIM = 64
ROPE_DIM = 32
V_DIM = 64
Q_RANK = 256
KV_RANK = 128
ROPE_BASE = 10000.0
MLA_SCALE = (NOPE_DIM + ROPE_DIM) ** -0.5
S5_GROUP_CH = 16
S5_W = 256
S5_GROUPS = S5_W // S5_GROUP_CH
S5_STATE = 64
S5_N = S5_GROUPS * S5_STATE
RWKV_HEADS = 6
RWKV_HEAD = 64
RWKV_W = RWKV_HEADS * RWKV_HEAD
DECAY_LORA = 32
AAA_LORA = 32
GATE_LORA = 64
RWKV_PROJ = 3 * RWKV_W + DECAY_LORA + AAA_LORA + GATE_LORA
LN_EPS = 1e-5
RMS_EPS = 1e-6
GN_EPS = 64e-5
NEG_INF = -1e30

V7X_LANES = 128
V7X_VMEM_BYTES = 64 * 1024 * 1024
VMEM_LIMIT_BYTES = V7X_VMEM_BYTES - 8 * 1024 * 1024
HEAD_PAD = V7X_LANES
ATTN_BLOCK = 512
ROW_TILE = 128
ATTN_HEAD_GROUP = 6
V_ROWS = V_DIM + 16
LOG2_E = 1.4426950408889634
RWKV_PACK = 8


def _cparams(*sem):
    return pltpu.CompilerParams(dimension_semantics=sem, vmem_limit_bytes=VMEM_LIMIT_BYTES)


def _const_spec(shape):
    zeros = (0,) * len(shape)
    return pl.BlockSpec(shape, lambda *_: zeros)


def _tile(n, target):
    if n <= target:
        return n
    t = target
    while n % t:
        t //= 2
    return t


def _layer_norm(y, g, b):
    mu = jnp.mean(y, -1, keepdims=True)
    yc = y - mu
    var = jnp.mean(yc * yc, -1, keepdims=True)
    return yc * lax.rsqrt(var + LN_EPS) * g + b


def _inproj_kernel(x_ref, wm_ref, wu_ref, wp_ref, wq1_ref, wq2_ref, qg_ref, kg_ref, csk_ref, cq_ref, sq_ref,
                   wk_ref, pk_ref, wvt_ref, qcat_ref, ckv_ref, krope_ref, u_ref, p_ref, kcat_ref, vt_ref):
    B, tt, D = x_ref.shape
    x = x_ref[...].reshape(B * tt, D).astype(BF16)
    per_b = lambda tab_ref: jnp.concatenate([tab_ref[...]] * B, axis=0)
    m = jnp.dot(x, wm_ref[...], preferred_element_type=F32)
    q_lat = m[:, :Q_RANK]
    qn = q_lat * lax.rsqrt(jnp.mean(q_lat * q_lat, -1, keepdims=True) + RMS_EPS) * qg_ref[...]
    kv_lat = m[:, Q_RANK:Q_RANK + KV_RANK]
    ckv = kv_lat * lax.rsqrt(jnp.mean(kv_lat * kv_lat, -1, keepdims=True) + RMS_EPS) * kg_ref[...]
    ckv_ref[...] = ckv.reshape(B, tt, KV_RANK)
    kr = m[:, Q_RANK + KV_RANK:] * per_b(csk_ref)
    kr = kr + pltpu.roll(kr, V7X_LANES - ROPE_DIM, 1)
    krope_ref[...] = kr[:, :ROPE_DIM].reshape(B, tt, ROPE_DIM)
    cb = ckv.astype(BF16)
    kcat = (jnp.dot(cb, wk_ref[...], preferred_element_type=F32)
            + jnp.dot(kr[:, :ROPE_DIM].astype(BF16), pk_ref[...], preferred_element_type=F32))
    kcat_ref[...] = kcat.astype(BF16).reshape(B, tt, MLA_HEADS * HEAD_PAD)
    ones = jnp.ones((V_ROWS - V_DIM, tt), BF16)
    for b in range(B):
        vt = lax.dot_general(wvt_ref[...], cb[b * tt:(b + 1) * tt], (((1,), (1,)), ((), ())),
                             preferred_element_type=F32).astype(BF16)
        for h in range(MLA_HEADS):
            vt_ref[b, h, 0, :V_DIM, :] = vt[h * V_DIM:(h + 1) * V_DIM, :]
            vt_ref[b, h, 0, V_DIM:, :] = ones
    qb = qn.astype(BF16)
    q1 = jnp.dot(qb, wq1_ref[...], preferred_element_type=F32)
    q2 = jnp.dot(qb, wq2_ref[...], preferred_element_type=F32)
    cq = per_b(cq_ref)
    sq = per_b(sq_ref)
    for h in range(MLA_HEADS):
        sl = slice(h * HEAD_PAD, (h + 1) * HEAD_PAD)
        qcat_ref[:, :, sl] = (q1[:, sl] * cq + q2[:, sl] * sq).astype(BF16).reshape(B, tt, HEAD_PAD)
    u = jnp.dot(x, wu_ref[...], preferred_element_type=F32)
    p = jnp.dot(x, wp_ref[...], preferred_element_type=F32)
    for b in range(B):
        u_ref[:, b, :] = u[b * tt:(b + 1) * tt]
        p_ref[:, b, :] = p[b * tt:(b + 1) * tt]


def _inproj(x, w, wkv, tabs):
    B, T, D = x.shape
    tt = _tile(T, ROW_TILE)
    nq = MLA_HEADS * HEAD_PAD
    kb = _tile(T, ATTN_BLOCK)
    per = kb // tt
    row = lambda i: (0, i, 0)
    tab = lambda i: (i, 0)
    tm = lambda i: (i, 0, 0)
    return pl.pallas_call(
        _inproj_kernel,
        grid=(T // tt,),
        in_specs=[
            pl.BlockSpec((B, tt, D), row),
            _const_spec(w['wm'].shape), _const_spec(w['wu'].shape), _const_spec(w['wp'].shape),
            _const_spec(w['wq1'].shape), _const_spec(w['wq2'].shape),
            _const_spec((1, Q_RANK)), _const_spec((1, KV_RANK)),
            pl.BlockSpec((tt, V7X_LANES), tab), pl.BlockSpec((tt, HEAD_PAD), tab), pl.BlockSpec((tt, HEAD_PAD), tab),
            _const_spec(wkv['wk'].shape), _const_spec(wkv['pk'].shape), _const_spec(wkv['wvt'].shape),
        ],
        out_specs=[
            pl.BlockSpec((B, tt, nq), row),
            pl.BlockSpec((B, tt, KV_RANK), row),
            pl.BlockSpec((B, tt, ROPE_DIM), row),
            pl.BlockSpec((tt, B, S5_W), tm),
            pl.BlockSpec((tt, B, RWKV_PROJ), tm),
            pl.BlockSpec((B, tt, nq), row),
            pl.BlockSpec((B, MLA_HEADS, 1, V_ROWS, tt), lambda i: (0, 0, i // per, 0, i % per)),
        ],
        out_shape=[
            jax.ShapeDtypeStruct((B, T, nq), BF16),
            jax.ShapeDtypeStruct((B, T, KV_RANK), F32),
            jax.ShapeDtypeStruct((B, T, ROPE_DIM), F32),
            jax.ShapeDtypeStruct((T, B, S5_W), F32),
            jax.ShapeDtypeStruct((T, B, RWKV_PROJ), F32),
            jax.ShapeDtypeStruct((B, T, nq), BF16),
            jax.ShapeDtypeStruct((B, MLA_HEADS, T // kb, V_ROWS, kb), BF16),
        ],
        compiler_params=_cparams("parallel"),
        name="inproj",
    )(x, w['wm'], w['wu'], w['wp'], w['wq1'], w['wq2'], w['qg'], w['kg'], tabs['csk'], tabs['cq'], tabs['sq'],
      wkv['wk'], wkv['pk'], wkv['wvt'])


def _kvup_kernel(ckv_ref, kr_ref, wk_ref, pk_ref, wvt_ref, kcat_ref, vt_ref):
    c = ckv_ref[0].astype(BF16)
    kr = kr_ref[0].astype(BF16)
    kcat = jnp.dot(c, wk_ref[...], preferred_element_type=F32) + jnp.dot(kr, pk_ref[...], preferred_element_type=F32)
    kcat_ref[0] = kcat.astype(BF16)
    vt = lax.dot_general(wvt_ref[...], c, (((1,), (1,)), ((), ())), preferred_element_type=F32).astype(BF16)
    ones = jnp.ones((V_ROWS - V_DIM, vt.shape[1]), BF16)
    for h in range(MLA_HEADS):
        vt_ref[0, h, 0, :V_DIM, :] = vt[h * V_DIM:(h + 1) * V_DIM, :]
        vt_ref[0, h, 0, V_DIM:, :] = ones


def _kvup(ckv, krope, w):
    B, T, _ = ckv.shape
    tt = _tile(T, ATTN_BLOCK)
    nk = MLA_HEADS * HEAD_PAD
    row = lambda b, i: (b, i, 0)
    return pl.pallas_call(
        _kvup_kernel,
        grid=(B, T // tt),
        in_specs=[pl.BlockSpec((1, tt, KV_RANK), row), pl.BlockSpec((1, tt, ROPE_DIM), row),
                  _const_spec(w['wk'].shape), _const_spec(w['pk'].shape), _const_spec(w['wvt'].shape)],
        out_specs=[pl.BlockSpec((1, tt, nk), row),
                   pl.BlockSpec((1, MLA_HEADS, 1, V_ROWS, tt), lambda b, i: (b, 0, i, 0, 0))],
        out_shape=[jax.ShapeDtypeStruct((B, T, nk), BF16),
                   jax.ShapeDtypeStruct((B, MLA_HEADS, T // tt, V_ROWS, tt), BF16)],
        compiler_params=_cparams("parallel", "parallel"),
        name="kvup",
    )(ckv, krope, w['wk'], w['pk'], w['wvt'])


def _attn_kernel(q_ref, k_ref, v_ref, *rest, bq, bk, kv_len, chunk_causal, extra):
    if extra:
        kx_ref, vx_ref, o_ref, sa_ref, sb_ref, bias_ref, m_ref, acc_ref = rest
    else:
        o_ref, sa_ref, sb_ref, bias_ref, m_ref, acc_ref = rest
    hg = ATTN_HEAD_GROUP
    qi = pl.program_id(2)
    last = qi if chunk_causal else jnp.int32(-(-kv_len // bk) - 1)
    mask_last = chunk_causal or kv_len % bk != 0
    nt = (((1,), (1,)), ((), ()))

    def scores(j, s_ref, h):
        r0 = pl.multiple_of(j * bk, bk)
        sl = slice(h * HEAD_PAD, (h + 1) * HEAD_PAD)
        s_ref[h] = lax.dot_general(k_ref[0, pl.ds(r0, bk), sl], q_ref[0, :, sl], nt,
                                   preferred_element_type=F32)

    def online_softmax(s, v, h):
        m = m_ref[h]
        m_new = jnp.maximum(m, jnp.max(s, axis=0, keepdims=True))
        p = jnp.exp2(s - m_new)
        alpha = jnp.exp2(m - m_new)
        acc_ref[h] = alpha * acc_ref[h] + jnp.dot(v, p.astype(BF16), preferred_element_type=F32)
        m_ref[h] = m_new

    def softmax_pv(j, s_ref, h, masked):
        s = s_ref[h]
        if masked:
            s = s + bias_ref[...]
        online_softmax(s, v_ref[0, h, j], h)

    @pl.when(qi == 0)
    def _():
        krel = lax.broadcasted_iota(jnp.int32, (bk, bq), 0)
        if chunk_causal:
            qrel = lax.broadcasted_iota(jnp.int32, (bk, bq), 1)
            ok = (krel // CHUNK) <= (qrel // CHUNK)
        else:
            ok = krel < kv_len - (-(-kv_len // bk) - 1) * bk
        bias_ref[...] = jnp.where(ok, 0.0, NEG_INF)

    def advance(j_next, s_next, j, s_cur, masked=False):
        for h in range(hg):
            if j_next is not None:
                scores(j_next, s_next, h)
            softmax_pv(j, s_cur, h, masked)

    m_ref[...] = jnp.full(m_ref.shape, NEG_INF, F32)
    acc_ref[...] = jnp.zeros(acc_ref.shape, F32)
    for h in range(hg):
        scores(0, sa_ref, h)

    def pair(p, _):
        j = 2 * p
        advance(j + 1, sb_ref, j, sa_ref)
        advance(j + 2, sa_ref, j + 1, sb_ref)
        return 0

    lax.fori_loop(0, last // 2, pair, 0)

    @pl.when(last % 2 == 0)
    def _():
        advance(None, None, last, sa_ref, mask_last)

    @pl.when(last % 2 == 1)
    def _():
        advance(last, sb_ref, last - 1, sa_ref)
        advance(None, None, last, sb_ref, mask_last)

    if extra:
        for h in range(hg):
            sl = slice(h * HEAD_PAD, (h + 1) * HEAD_PAD)
            s = lax.dot_general(kx_ref[0, :, sl], q_ref[0, :, sl], nt, preferred_element_type=F32)
            online_softmax(s, vx_ref[0, h, 0], h)

    for h in range(hg):
        o_ref[0, h] = (acc_ref[h, :V_DIM, :] / acc_ref[h, V_DIM:V_DIM + 1, :]).astype(BF16)


def _attention(qcat, kcat, vt, *, kv_len, chunk_causal, extra=None):
    B, Tq, _ = qcat.shape
    Tk = kcat.shape[1]
    nblk, bk = vt.shape[2], vt.shape[4]
    assert nblk * bk == Tk
    bq = bk if chunk_causal else Tq
    hg = ATTN_HEAD_GROUP
    in_specs = [
        pl.BlockSpec((1, bq, hg * HEAD_PAD), lambda b, g, i: (b, i, g)),
        pl.BlockSpec((1, Tk, hg * HEAD_PAD), lambda b, g, i: (b, 0, g), pipeline_mode=pl.Buffered(1)),
        pl.BlockSpec((1, hg, nblk, V_ROWS, bk), lambda b, g, i: (b, g, 0, 0, 0), pipeline_mode=pl.Buffered(1)),
    ]
    operands = [qcat, kcat, vt]
    if extra is not None:
        tx = extra[0].shape[1]
        in_specs += [pl.BlockSpec((1, tx, hg * HEAD_PAD), lambda b, g, i: (b, 0, g)),
                     pl.BlockSpec((1, hg, 1, V_ROWS, tx), lambda b, g, i: (b, g, 0, 0, 0))]
        operands += list(extra)
    out_t = pl.pallas_call(
        functools.partial(_attn_kernel, bq=bq, bk=bk, kv_len=kv_len, chunk_causal=chunk_causal,
                          extra=extra is not None),
        grid=(B, MLA_HEADS // hg, Tq // bq),
        in_specs=in_specs,
        out_specs=pl.BlockSpec((1, hg, V_DIM, bq), lambda b, g, i: (b, g, 0, i)),
        out_shape=jax.ShapeDtypeStruct((B, MLA_HEADS, V_DIM, Tq), BF16),
        scratch_shapes=[pltpu.VMEM((hg, bk, bq), F32), pltpu.VMEM((hg, bk, bq), F32), pltpu.VMEM((bk, bq), F32),
                        pltpu.VMEM((hg, 1, bq), F32), pltpu.VMEM((hg, V_ROWS, bq), F32)],
        compiler_params=_cparams("parallel", "parallel", "arbitrary"),
        name="mla_attention",
    )(*operands)
    return out_t


def _gelu_tanh(x):
    return 0.5 * x * (1.0 + jnp.tanh(math.sqrt(2.0 / math.pi) * (x + 0.044715 * (x * x * x))))


def _s5_kernel(u_ref, x0_ref, a_ref, wb_ref, wc_ref, d_ref, wg_ref, bg_ref, o_ref, xl_ref, xs_ref, st_ref, *, L, B):
    @pl.when(pl.program_id(0) == 0)
    def _():
        st_ref[...] = x0_ref[...]

    u = u_ref[...].reshape(L * B, S5_W)
    xs_ref[...] = jnp.dot(u.astype(BF16), wb_ref[...], preferred_element_type=F32)
    a_re = a_ref[0:1, :]
    a_im = a_ref[1:2, :]

    def body(t, carry):
        x_re, x_im = carry
        r0 = pl.multiple_of(t * B, B)
        n_re = a_re * x_re - a_im * x_im + xs_ref[pl.ds(r0, B), :S5_N]
        n_im = a_re * x_im + a_im * x_re + xs_ref[pl.ds(r0, B), S5_N:]
        xs_ref[pl.ds(r0, B), :S5_N] = n_re
        xs_ref[pl.ds(r0, B), S5_N:] = n_im
        return n_re, n_im

    x_re, x_im = lax.fori_loop(0, L, body, (st_ref[:, :S5_N], st_ref[:, S5_N:]), unroll=8)
    st_ref[:, :S5_N] = x_re
    st_ref[:, S5_N:] = x_im
    xl_ref[...] = st_ref[...]

    y = jnp.dot(xs_ref[...].astype(BF16), wc_ref[...], preferred_element_type=F32) + d_ref[...] * u
    z = _gelu_tanh(y)
    gate = jax.nn.sigmoid(jnp.dot(z.astype(BF16), wg_ref[...], preferred_element_type=F32) + bg_ref[...])
    o_ref[...] = (z * gate).reshape(L, B, S5_W)


def _s5(u_tm, x0, w):
    T, B, _ = u_tm.shape
    L = _tile(T, 1024 // B)
    return pl.pallas_call(
        functools.partial(_s5_kernel, L=L, B=B),
        grid=(T // L,),
        in_specs=[
            pl.BlockSpec((L, B, S5_W), lambda i: (i, 0, 0)),
            _const_spec((B, 2 * S5_N)), _const_spec((2, S5_N)),
            _const_spec((S5_W, 2 * S5_N)), _const_spec((2 * S5_N, S5_W)),
            _const_spec((1, S5_W)), _const_spec((S5_W, S5_W)), _const_spec((1, S5_W)),
        ],
        out_specs=[pl.BlockSpec((L, B, S5_W), lambda i: (i, 0, 0)), _const_spec((B, 2 * S5_N))],
        out_shape=[jax.ShapeDtypeStruct((T, B, S5_W), F32), jax.ShapeDtypeStruct((B, 2 * S5_N), F32)],
        scratch_shapes=[pltpu.VMEM((L * B, 2 * S5_N), F32), pltpu.VMEM((B, 2 * S5_N), F32)],
        compiler_params=_cparams("arbitrary"),
        name="s5_scan",
    )(u_tm, x0, w['a'], w['wb'], w['wc'], w['d'], w['wglu'], w['bglu'])


def _softplus(x):
    return jnp.maximum(x, 0.0) + jnp.log(1.0 + jnp.exp(-jnp.abs(x)))


def _head_sum(x, ones_ref):
    hi = x.astype(BF16)
    lo = (x - hi.astype(F32)).astype(BF16)
    ones = ones_ref[...]
    return jnp.dot(hi, ones, preferred_element_type=F32) + jnp.dot(lo, ones, preferred_element_type=F32)


def _pack_rows(x, o_ref, zs_ref, B):
    rows = RWKV_PACK * B
    lane = lax.broadcasted_iota(jnp.int32, (RWKV_HEAD, rows * RWKV_HEADS), 1)
    for m in range(x.shape[0] // rows):
        xm = x[m * rows:(m + 1) * rows]
        for h in range(RWKV_HEADS):
            c = h * RWKV_HEAD // V7X_LANES
            zs_ref[pl.ds(h, rows, stride=RWKV_HEADS), :] = xm[:, c * V7X_LANES:(c + 1) * V7X_LANES]
        t = zs_ref[...].T
        o_ref[m] = jnp.where(lane % 2 == 0, t[:RWKV_HEAD, :], t[RWKV_HEAD:, :])


def _rwkv_prep_kernel(p_ref, sh0_ref, mu_ref, ww_ref, wa_ref, wg_ref, w0_ref, a0_ref, ka_ref, rk_ref,
                      ones_ref, r_o, w_o, k_o, v_o, a_o, g_o, bon_o, sh_o, last_ref, zs_ref, *, L, B):
    @pl.when(pl.program_id(0) == 0)
    def _():
        last_ref[...] = sh0_ref[...]

    p = p_ref[...].reshape(L * B, RWKV_PROJ)
    if L > 1:
        prev = jnp.concatenate([last_ref[...], p[:(L - 1) * B]], axis=0)
    else:
        prev = last_ref[...]
    last_ref[...] = p[(L - 1) * B:]
    sh_o[...] = p[(L - 1) * B:]
    ps = p + (prev - p) * mu_ref[...]
    r = ps[:, :RWKV_W]
    k = ps[:, RWKV_W:2 * RWKV_W]
    v = ps[:, 2 * RWKV_W:3 * RWKV_W]
    tail = ps[:, 3 * RWKV_W:]
    lw = jnp.dot(jnp.tanh(tail).astype(BF16), ww_ref[...], preferred_element_type=F32)
    la = jnp.dot(tail.astype(BF16), wa_ref[...], preferred_element_type=F32)
    g = jnp.dot(jax.nn.sigmoid(tail).astype(BF16), wg_ref[...], preferred_element_type=F32)
    w_log = -_softplus(-(w0_ref[...] + lw)) - 0.5
    decay = jnp.exp(-jnp.exp(w_log))
    a = jax.nn.sigmoid(a0_ref[...] + la)
    k2 = k * (1.0 + (a - 1.0) * ka_ref[...])
    bonus = _head_sum(r * k2 * rk_ref[...], ones_ref) * v
    for val, o_ref in ((r, r_o), (decay, w_o), (k, k_o), (v, v_o), (a, a_o)):
        _pack_rows(val, o_ref, zs_ref, B)
    shp = (L, B, RWKV_W)
    g_o[...] = g.reshape(shp)
    bon_o[...] = bonus.reshape(shp)


def _rwkv_prep(p_tm, shift0, w):
    T, B, _ = p_tm.shape
    L = _tile(T, 512 // B)
    assert L % RWKV_PACK == 0
    blk = pl.BlockSpec((L, B, RWKV_W), lambda i: (i, 0, 0))
    W = RWKV_PACK * B * RWKV_HEADS
    pblk = pl.BlockSpec((L // RWKV_PACK, RWKV_HEAD, W), lambda i: (i, 0, 0))
    packed = jax.ShapeDtypeStruct((T // RWKV_PACK, RWKV_HEAD, W), F32)
    vec = _const_spec((1, RWKV_W))
    outs = pl.pallas_call(
        functools.partial(_rwkv_prep_kernel, L=L, B=B),
        grid=(T // L,),
        in_specs=[pl.BlockSpec((L, B, RWKV_PROJ), lambda i: (i, 0, 0)), _const_spec((B, RWKV_PROJ)),
                  _const_spec((1, RWKV_PROJ)),
                  _const_spec((V7X_LANES, RWKV_W)), _const_spec((V7X_LANES, RWKV_W)), _const_spec((V7X_LANES, RWKV_W)),
                  vec, vec, vec, vec, _const_spec((RWKV_W, RWKV_W))],
        out_specs=[pblk] * 5 + [blk] * 2 + [_const_spec((B, RWKV_PROJ))],
        out_shape=[packed] * 5 + [jax.ShapeDtypeStruct((T, B, RWKV_W), F32)] * 2
        + [jax.ShapeDtypeStruct((B, RWKV_PROJ), F32)],
        scratch_shapes=[pltpu.VMEM((B, RWKV_PROJ), F32), pltpu.VMEM((RWKV_PACK * B * RWKV_HEADS, V7X_LANES), F32)],
        compiler_params=_cparams("arbitrary"),
        name="rwkv_prep",
    )(p_tm, shift0, w['mu'], w['ww'], w['wa'], w['wg'], w['w0'], w['a0'], w['ka'], w['rk'], w['ones'])
    return outs


def _lane_window(tile, off, width, dst, lane):
    v0, lo = divmod(off, V7X_LANES)
    shift = (dst - lo) % V7X_LANES
    a = tile(v0)
    r = pltpu.roll(a, shift, 1) if shift else a
    if lo + width > V7X_LANES:
        b = tile(v0 + 1)
        r2 = pltpu.roll(b, shift, 1) if shift else b
        r = jnp.where(lane < dst + (V7X_LANES - lo), r, r2)
    return r


def _chain_lanes(pieces, bh, lane):
    rep = len(pieces)
    out = jnp.where(lane < rep * bh, pieces[rep - 1], 0.0)
    for r in range(rep - 2, -1, -1):
        out = jnp.where(lane < (r + 1) * bh, pieces[r], out)
    return out


def _rwkv_scan_kernel(w_ref, a_ref, k_ref, kx_ref, r_ref, v_ref, kkt_ref, kat_ref, s0_ref, y_ref, so_ref,
                      s_ref, sa_ref, e_ref, ev_ref, ys_ref, kall_ref, *, Tm, IL, bh, rep):
    lane = lax.broadcasted_iota(jnp.int32, (RWKV_HEAD, V7X_LANES), 1)
    lane_il = lax.broadcasted_iota(jnp.int32, (IL, V7X_LANES), 1)
    ncol = RWKV_PACK * bh // V7X_LANES

    def tiles(slab, rows=slice(None)):
        return lambda c: slab[rows, c * V7X_LANES:(c + 1) * V7X_LANES]

    def expand(slab, t8):
        return _chain_lanes([_lane_window(tiles(slab), t8 * bh, bh, r * bh, lane) for r in range(rep)], bh, lane)

    def unit_kk(k_raw):
        kk = k_raw * kkt_ref[...]
        nrm = jnp.sqrt(jnp.sum(kk * kk, axis=0, keepdims=True))
        return kk / jnp.maximum(nrm, 1e-12)

    kall_ref[0:Tm] = k_ref[...]
    kall_ref[Tm] = kx_ref[0]

    @pl.when(pl.program_id(0) == 0)
    def _():
        s_ref[...] = s0_ref[...]
        e_ref[0, 4] = expand(-unit_kk(k_ref[0]), 0)
        acc = jnp.zeros((IL, V7X_LANES), F32)
        for j in range(RWKV_HEAD):
            acc = acc + s0_ref[j] * e_ref[0, 4, j:j + 1, :]
        sa_ref[...] = acc

    def body(m, sa):
        k_raw = kall_ref[m]
        a_s = a_ref[m]
        kkn = unit_kk(k_raw)
        slabs = [w_ref[m], kkn * a_s, k_raw * (1.0 + (a_s - 1.0) * kat_ref[...]), r_ref[m]]
        nn0 = -kkn
        nn1 = -unit_kk(kall_ref[m + 1])
        vs = v_ref[m]
        for t8 in range(RWKV_PACK):
            for o, slab in enumerate(slabs):
                e_ref[t8, o] = expand(slab, t8)
            e_ref[t8, 4] = expand(nn0, t8 + 1) if t8 + 1 < RWKV_PACK else expand(nn1, 0)
            ev_ref[t8] = _chain_lanes([_lane_window(tiles(vs, slice(r * IL, (r + 1) * IL)), t8 * bh, bh, r * bh,
                                                    lane_il) for r in range(rep)], bh, lane_il)
        for t8 in range(RWKV_PACK):
            vt = ev_ref[t8]
            yacc = jnp.zeros((IL, V7X_LANES), F32)
            san = jnp.zeros((IL, V7X_LANES), F32)
            for j in range(RWKV_HEAD):
                row = lambda o: e_ref[t8, o, j:j + 1, :]
                sn = s_ref[j] * row(0) + sa * row(1) + vt * row(2)
                s_ref[j] = sn
                yacc = yacc + sn * row(3)
                san = san + sn * row(4)
            sa = san
            ys_ref[t8] = yacc
        for c in range(ncol):
            for r in range(rep):
                col = jnp.zeros((IL, V7X_LANES), F32)
                for t8 in range(RWKV_PACK):
                    off = t8 * bh
                    lo, hi = max(off, c * V7X_LANES), min(off + bh, (c + 1) * V7X_LANES)
                    if lo < hi:
                        shift = (off - r * bh) % V7X_LANES
                        rolled = pltpu.roll(ys_ref[t8], shift, 1) if shift else ys_ref[t8]
                        inside = jnp.logical_and(lane_il >= lo - c * V7X_LANES, lane_il < hi - c * V7X_LANES)
                        col = jnp.where(inside, rolled, col)
                y_ref[m, r * IL:(r + 1) * IL, c * V7X_LANES:(c + 1) * V7X_LANES] = col
        return sa

    sa_ref[...] = lax.fori_loop(0, Tm, body, sa_ref[...])
    so_ref[...] = s_ref[...]


def _rwkv_scan(w_p, a_p, k_p, r_p, v_p, kk_t, ka_t, s0_l, bh, rep):
    TP, _, W = v_p.shape
    IL = RWKV_HEAD // rep
    Tm = _tile(TP, 8)
    blk = pl.BlockSpec((Tm, RWKV_HEAD, W), lambda i: (i, 0, 0))
    nxt = pl.BlockSpec((1, RWKV_HEAD, W), lambda i: (jnp.minimum((i + 1) * Tm, TP - 1), 0, 0))
    sblk = _const_spec((RWKV_HEAD, IL, V7X_LANES))
    par = _const_spec((RWKV_HEAD, W))
    return pl.pallas_call(
        functools.partial(_rwkv_scan_kernel, Tm=Tm, IL=IL, bh=bh, rep=rep),
        grid=(TP // Tm,),
        in_specs=[blk, blk, blk, nxt, blk, blk, par, par, sblk],
        out_specs=[blk, sblk],
        out_shape=[jax.ShapeDtypeStruct((TP, RWKV_HEAD, W), F32),
                   jax.ShapeDtypeStruct((RWKV_HEAD, IL, V7X_LANES), F32)],
        scratch_shapes=[pltpu.VMEM((RWKV_HEAD, IL, V7X_LANES), F32), pltpu.VMEM((IL, V7X_LANES), F32),
                        pltpu.VMEM((RWKV_PACK, 5, RWKV_HEAD, V7X_LANES), F32),
                        pltpu.VMEM((RWKV_PACK, IL, V7X_LANES), F32), pltpu.VMEM((RWKV_PACK, IL, V7X_LANES), F32),
                        pltpu.VMEM((Tm + 1, RWKV_HEAD, W), F32)],
        compiler_params=_cparams("arbitrary"),
        name="rwkv_scan",
    )(w_p, a_p, k_p, k_p, r_p, v_p, kk_t, ka_t, s0_l)


def _rwkv_layout(B):
    bh = B * RWKV_HEADS
    rep = 1
    while 2 * rep * bh <= V7X_LANES and RWKV_HEAD % (2 * rep) == 0:
        rep *= 2
    return bh, rep, RWKV_HEAD // rep


def _rwkv_mixer(p_tm, shift0, s0, w):
    T, B, _ = p_tm.shape
    bh, rep, IL = _rwkv_layout(B)
    lanes = rep * bh
    r_p, dec_p, k_p, v_p, a_p, g, bonus, shift = _rwkv_prep(p_tm, shift0, w)
    TP = T // RWKV_PACK

    def par_tile(vec):
        t = vec.reshape(RWKV_HEADS, RWKV_HEAD).T[:, None, None, :]
        return jnp.broadcast_to(t, (RWKV_HEAD, RWKV_PACK, B, RWKV_HEADS)).reshape(RWKV_HEAD, RWKV_PACK * bh)

    s0_l = s0.reshape(bh, rep, IL, RWKV_HEAD).transpose(3, 2, 1, 0).reshape(RWKV_HEAD, IL, lanes)
    s0_l = jnp.pad(s0_l, ((0, 0), (0, 0), (0, V7X_LANES - lanes)))
    y_p, s_l = _rwkv_scan(dec_p, a_p, k_p, r_p, v_p, par_tile(w['kk']), par_tile(w['ka']), s0_l, bh, rep)
    y_tm = y_p.reshape(TP, RWKV_HEAD, RWKV_PACK, bh).transpose(0, 2, 3, 1).reshape(T, B, RWKV_W)
    s_last = s_l[:, :, :lanes].reshape(RWKV_HEAD, IL, rep, bh).transpose(3, 2, 1, 0)
    s_last = s_last.reshape(B, RWKV_HEADS, RWKV_HEAD, RWKV_HEAD)
    return (y_tm, bonus, g), s_last, shift


def _outproj_kernel(m_ref, s_ref, y_ref, bon_ref, gate_ref, x_ref, wm_ref, ws_ref, wr_ref, gng_ref, gnb_ref,
                    ones_ref, g_ref, b_ref, o_ref, *, alpha):
    B, tt, D = x_ref.shape
    mla = jnp.concatenate([m_ref[b].reshape(MLA_HEADS * V_DIM, tt).T for b in range(B)], axis=0)
    rows = lambda ref: jnp.concatenate([ref[:, b, :] for b in range(B)], axis=0)
    acc = jnp.dot(mla, wm_ref[...], preferred_element_type=F32)
    acc = acc + jnp.dot(rows(s_ref).astype(BF16), ws_ref[...], preferred_element_type=F32)
    y = rows(y_ref)
    inv_n = 1.0 / RWKV_HEAD
    mu = _head_sum(y, ones_ref) * inv_n
    yc = y - mu
    var = _head_sum(yc * yc, ones_ref) * inv_n
    yn = yc * lax.rsqrt(var + GN_EPS) * gng_ref[...] + gnb_ref[...]
    rw = (yn + rows(bon_ref)) * rows(gate_ref)
    acc = acc + jnp.dot(rw.astype(BF16), wr_ref[...], preferred_element_type=F32)
    out = _layer_norm(alpha * x_ref[...].reshape(B * tt, D) + acc, g_ref[...], b_ref[...])
    o_ref[...] = out.reshape(B, tt, D)


def _outproj(mla_t, s5_tm, rwkv, x, w, wr, alpha):
    B, T, D = x.shape
    tt = _tile(T, ROW_TILE)
    row = lambda i: (0, i, 0)
    tm = lambda i: (i, 0, 0)
    rblk = pl.BlockSpec((tt, B, RWKV_W), tm)
    vec = _const_spec((1, RWKV_W))
    return pl.pallas_call(
        functools.partial(_outproj_kernel, alpha=alpha),
        grid=(T // tt,),
        in_specs=[pl.BlockSpec((B, MLA_HEADS, V_DIM, tt), lambda i: (0, 0, 0, i)), pl.BlockSpec((tt, B, S5_W), tm),
                  rblk, rblk, rblk, pl.BlockSpec((B, tt, D), row),
                  _const_spec(w['wo_m'].shape), _const_spec(w['wo_s'].shape), _const_spec(w['wo_r'].shape),
                  vec, vec, _const_spec((RWKV_W, RWKV_W)), _const_spec((1, D)), _const_spec((1, D))],
        out_specs=pl.BlockSpec((B, tt, D), row),
        out_shape=jax.ShapeDtypeStruct((B, T, D), F32),
        compiler_params=_cparams("parallel"),
        name="outproj_ln",
    )(mla_t, s5_tm, *rwkv, x, w['wo_m'], w['wo_s'], w['wo_r'], wr['gng'], wr['gnb'], wr['ones'], w['ln1g'],
      w['ln1b'])


def _mlp_kernel(x_ref, wu_ref, wd_ref, g_ref, b_ref, o_ref, *, alpha, fc):
    x = x_ref[0]
    xb = x.astype(BF16)
    acc = alpha * x
    for c in range(wu_ref.shape[1] // fc):
        h = jnp.maximum(jnp.dot(xb, wu_ref[:, c * fc:(c + 1) * fc], preferred_element_type=F32), 0.0)
        acc = acc + jnp.dot((h * h).astype(BF16), wd_ref[c * fc:(c + 1) * fc, :], preferred_element_type=F32)
    o_ref[0] = _layer_norm(acc, g_ref[...], b_ref[...])


def _mlp(x, w, alpha):
    B, T, D = x.shape
    tt = _tile(T, 512)
    row = lambda b, i: (b, i, 0)
    return pl.pallas_call(
        functools.partial(_mlp_kernel, alpha=alpha, fc=1024),
        grid=(B, T // tt),
        in_specs=[pl.BlockSpec((1, tt, D), row), _const_spec(w['wup'].shape), _const_spec(w['wdown'].shape),
                  _const_spec((1, D)), _const_spec((1, D))],
        out_specs=pl.BlockSpec((1, tt, D), row),
        out_shape=jax.ShapeDtypeStruct((B, T, D), F32),
        compiler_params=_cparams("parallel", "parallel"),
        name="mlp_ln",
    )(x, w['wup'], w['wdown'], w['ln2g'], w['ln2b'])


def _rot_cols(wr):
    half = ROPE_DIM // 2
    return jnp.concatenate([-wr[..., half:], wr[..., :half]], axis=-1)


def _prep_layer(p):
    D = p['w_in'].shape[0]
    w_in = p['w_in']
    o_kv = Q_RANK
    o_kr = Q_RANK + KV_RANK
    o_u = o_kr + ROPE_DIM
    o_p = o_u + S5_W
    w_kr = w_in[:, o_kr:o_u]
    wm = jnp.concatenate([w_in[:, :o_kr], w_kr, _rot_cols(w_kr),
                          jnp.zeros((D, V7X_LANES - 2 * ROPE_DIM), F32)], axis=1)
    wqb = p['w_qb'].reshape(Q_RANK, MLA_HEADS, NOPE_DIM + ROPE_DIM) * (MLA_SCALE * LOG2_E)
    zq = jnp.zeros((Q_RANK, MLA_HEADS, HEAD_PAD - NOPE_DIM - ROPE_DIM), F32)
    wq1 = jnp.concatenate([wqb, zq], axis=-1).reshape(Q_RANK, MLA_HEADS * HEAD_PAD)
    wq2 = jnp.concatenate([jnp.zeros((Q_RANK, MLA_HEADS, NOPE_DIM), F32), _rot_cols(wqb[..., NOPE_DIM:]), zq],
                          axis=-1).reshape(Q_RANK, MLA_HEADS * HEAD_PAD)
    wkvb = p['w_kvb'].reshape(KV_RANK, MLA_HEADS, NOPE_DIM + V_DIM)
    wk = jnp.concatenate([wkvb[..., :NOPE_DIM], jnp.zeros((KV_RANK, MLA_HEADS, HEAD_PAD - NOPE_DIM), F32)],
                         axis=-1).reshape(KV_RANK, MLA_HEADS * HEAD_PAD)
    pk = jnp.concatenate([jnp.zeros((ROPE_DIM, NOPE_DIM), F32), jnp.eye(ROPE_DIM, dtype=F32),
                          jnp.zeros((ROPE_DIM, HEAD_PAD - NOPE_DIM - ROPE_DIM), F32)], axis=1)
    pk = jnp.tile(pk, (1, MLA_HEADS))
    wvt = wkvb[..., NOPE_DIM:].reshape(KV_RANK, MLA_HEADS * V_DIM).T

    lr, li = p['lam_re'], p['lam_im']
    dt = jnp.exp(p['log_dt'])[:, None]
    mag = jnp.exp(lr * dt)
    ar, ai = mag * jnp.cos(li * dt), mag * jnp.sin(li * dt)
    den = lr * lr + li * li
    cr = ((ar - 1.0) * lr + ai * li) / den
    ci = (ai * lr - (ar - 1.0) * li) / den
    bbr = cr[..., None] * p['b_re'] - ci[..., None] * p['b_im']
    bbi = cr[..., None] * p['b_im'] + ci[..., None] * p['b_re']
    eye_g = jnp.eye(S5_GROUPS, dtype=F32)
    bd_in = lambda m: (m.transpose(0, 2, 1)[:, :, None, :] * eye_g[:, None, :, None]).reshape(S5_W, S5_N)
    bd_out = lambda m: (m.transpose(0, 2, 1)[:, :, None, :] * eye_g[:, None, :, None]).reshape(S5_N, S5_W)
    wb = jnp.concatenate([bd_in(bbr), bd_in(bbi)], axis=1)
    wc = jnp.concatenate([bd_out(p['c_re']), -bd_out(p['c_im'])], axis=0)
    a = jnp.stack([ar.reshape(S5_N), ai.reshape(S5_N)])

    def lora_pad(w, off):
        z = jnp.zeros((V7X_LANES, RWKV_W), F32)
        return z.at[off:off + w.shape[0]].set(w)

    seg = jnp.arange(RWKV_W) // RWKV_HEAD
    ones = (seg[:, None] == seg[None, :]).astype(F32)
    row = lambda v: v.reshape(1, -1).astype(F32)
    wo = p['w_out']
    return dict(
        inproj=dict(wm=wm.astype(BF16), wu=w_in[:, o_u:o_p].astype(BF16), wp=w_in[:, o_p:].astype(BF16),
                    wq1=wq1.astype(BF16), wq2=wq2.astype(BF16), qg=row(p['q_norm_g']), kg=row(p['kv_norm_g'])),
        kvup=dict(wk=wk.astype(BF16), pk=pk.astype(BF16), wvt=wvt.astype(BF16)),
        s5=dict(a=a, wb=wb.astype(BF16), wc=wc.astype(BF16), d=row(p['s5_d']), wglu=p['w_glu'].astype(BF16),
                bglu=row(p['b_glu'])),
        rwkv=dict(mu=row(p['mu_shift']), ww=lora_pad(p['w_w2'], 0).astype(BF16),
                  wa=lora_pad(p['w_a2'], DECAY_LORA).astype(BF16),
                  wg=lora_pad(p['w_g2'], DECAY_LORA + AAA_LORA).astype(BF16),
                  w0=row(p['w0']), a0=row(p['a0']), kk=row(p['k_k']), ka=row(p['k_a']), rk=row(p['r_k']),
                  gng=row(p['gn_g']), gnb=row(p['gn_b']), ones=ones.astype(BF16)),
        out=dict(wo_m=wo[:MLA_HEADS * V_DIM].astype(BF16),
                 wo_s=wo[MLA_HEADS * V_DIM:MLA_HEADS * V_DIM + S5_W].astype(BF16),
                 wo_r=wo[MLA_HEADS * V_DIM + S5_W:].astype(BF16), ln1g=row(p['ln1_g']), ln1b=row(p['ln1_b'])),
        mlp=dict(wup=p['w_up'].astype(BF16), wdown=p['w_down'].astype(BF16), ln2g=row(p['ln2_g']),
                 ln2b=row(p['ln2_b'])),
    )


def _rope_tables(pos):
    inv_freq = ROPE_BASE ** (-jnp.arange(0, ROPE_DIM, 2, dtype=F32) / ROPE_DIM)
    ang = pos.astype(F32)[:, None] * inv_freq[None, :]
    ang = jnp.concatenate([ang, ang], -1)
    cos, sin = jnp.cos(ang), jnp.sin(ang)
    T = pos.shape[0]
    z = lambda n: jnp.zeros((T, n), F32)
    return dict(
        csk=jnp.concatenate([cos, sin, z(V7X_LANES - 2 * ROPE_DIM)], axis=1),
        cq=jnp.concatenate([jnp.ones((T, NOPE_DIM), F32), cos, z(HEAD_PAD - NOPE_DIM - ROPE_DIM)], axis=1),
        sq=jnp.concatenate([z(NOPE_DIM), sin, z(HEAD_PAD - NOPE_DIM - ROPE_DIM)], axis=1),
    )


def _trunk_layer(x, tabs, past, s5_state, rwkv_state, shift0, w, alpha):
    B, T, D = x.shape
    qcat, ckv, krope, u_tm, p_tm, kcat, vt = _inproj(x, w['inproj'], w['kvup'], tabs)
    if past is None:
        mla = _attention(qcat, kcat, vt, kv_len=T, chunk_causal=True)
    else:
        assert vt.shape[2] == 1
        mla = _attention(qcat, past[0], past[1], kv_len=past[0].shape[1], chunk_causal=False, extra=(kcat, vt))

    x0 = jnp.concatenate([s5_state[..., 0].reshape(B, S5_N), s5_state[..., 1].reshape(B, S5_N)], axis=1)
    s5_out, x_last = _s5(u_tm, x0, w['s5'])
    new_s5 = jnp.stack([x_last[:, :S5_N].reshape(B, S5_GROUPS, S5_STATE),
                        x_last[:, S5_N:].reshape(B, S5_GROUPS, S5_STATE)], axis=-1)

    rwkv, new_rwkv, shift = _rwkv_mixer(p_tm, shift0.reshape(B, RWKV_PROJ), rwkv_state, w['rwkv'])
    x = _outproj(mla, s5_out, rwkv, x, w['out'], w['rwkv'], alpha)
    x = _mlp(x, w['mlp'], alpha)
    return x, ckv, krope, new_s5, new_rwkv, shift.reshape(B, 1, RWKV_PROJ)


def kernel(x_prompt, x_sample, cache_mla_ckv, cache_mla_krope, state_s5, state_rwkv, state_rwkv_shift, w_in, q_norm_g, w_qb, kv_norm_g, w_kvb, lam_re, lam_im, log_dt, b_re, b_im, c_re, c_im, s5_d, w_glu, b_glu, mu_shift, w0, w_w2, a0, w_a2, w_g2, k_k, k_a, r_k, gn_g, gn_b, w_out, ln1_g, ln1_b, w_up, w_down, ln2_g, ln2_b):
    depth = w_in.shape[0]
    alpha = (2 * depth) ** 0.25
    bp, sp = x_prompt.shape[:2]
    ts = x_sample.shape[1]
    past = cache_mla_ckv.shape[2]
    tabs_p = _rope_tables(jnp.arange(sp))
    tabs_s = _rope_tables(past + jnp.arange(ts))
    names = dict(w_in=w_in, q_norm_g=q_norm_g, w_qb=w_qb, kv_norm_g=kv_norm_g, w_kvb=w_kvb, lam_re=lam_re,
                 lam_im=lam_im, log_dt=log_dt, b_re=b_re, b_im=b_im, c_re=c_re, c_im=c_im, s5_d=s5_d, w_glu=w_glu,
                 b_glu=b_glu, mu_shift=mu_shift, w0=w0, w_w2=w_w2, a0=a0, w_a2=w_a2, w_g2=w_g2, k_k=k_k, k_a=k_a,
                 r_k=r_k, gn_g=gn_g, gn_b=gn_b, w_out=w_out, ln1_g=ln1_g, ln1_b=ln1_b, w_up=w_up, w_down=w_down,
                 ln2_g=ln2_g, ln2_b=ln2_b)
    s5_zero = jnp.zeros((bp, S5_GROUPS, S5_STATE, 2), F32)
    rwkv_zero = jnp.zeros((bp, RWKV_HEADS, RWKV_HEAD, RWKV_HEAD), F32)
    shift_zero = jnp.zeros((bp, 1, RWKV_PROJ), F32)

    xp, xs = x_prompt, x_sample
    outs_p = [[] for _ in range(5)]
    outs_s = [[] for _ in range(5)]
    for l in range(depth):
        w = _prep_layer({k: v[l] for k, v in names.items()})
        xp, *rest = _trunk_layer(xp, tabs_p, None, s5_zero, rwkv_zero, shift_zero, w, alpha)
        for acc, val in zip(outs_p, rest):
            acc.append(val)
        past_kv = _kvup(cache_mla_ckv[l], cache_mla_krope[l], w['kvup'])
        xs, *rest = _trunk_layer(xs, tabs_s, past_kv, state_s5[l], state_rwkv[l], state_rwkv_shift[l], w, alpha)
        for acc, val in zip(outs_s, rest):
            acc.append(val)
    return (xp, xs, *[jnp.stack(a) for a in outs_p], *[jnp.stack(a) for a in outs_s])
```
